```python
import math
import jax, jax.numpy as jnp
from jax import lax
import numpy as np

D_MODEL = 2048
BATCH = 4
SEQ = 4096
DEPTH = 4

N_EVEN = (DEPTH + 1) // 2
N_ODD = DEPTH // 2
HEAD_DIM = 128
FOX_HEADS = (D_MODEL // 2) // HEAD_DIM
RET_HEADS = (D_MODEL // 2) // HEAD_DIM
FOX_WIDTH = FOX_HEADS * HEAD_DIM
RET_WIDTH = RET_HEADS * HEAD_DIM
Q_BLOCK = 128
RET_CHUNK = 128
S5_WIDTH = D_MODEL // 2
S5_GROUP = 16
S5_GROUPS = S5_WIDTH // S5_GROUP
S5_STATE = 64
CONV_WIDTH = D_MODEL // 2
CONV_K = 3
MEM_LEN = 256
XA_HEADS = 4
XA_HEAD_DIM = 128
XA_WIDTH = XA_HEADS * XA_HEAD_DIM
D_FF = 4 * D_MODEL
EVEN_IN = 3 * FOX_WIDTH + FOX_HEADS + 4 * RET_WIDTH
ODD_IN = S5_WIDTH + 3 * CONV_WIDTH
MIX_OUT = FOX_WIDTH + RET_WIDTH
EPS = 1e-6
ROPE_BASE = 10000.0

kernel_name = 'hybrid_fox_retnet_s5_shortconv_trunk'

F32 = jnp.float32


def rmsnorm(x, g):
    xf = x.astype(F32)
    y = xf * lax.rsqrt(jnp.mean(xf * xf, axis=-1, keepdims=True) + EPS)
    return (y * g.astype(F32)).astype(x.dtype)


def to_heads(t, n_heads):
    b, l, _ = t.shape
    return t.reshape(b, l, n_heads, -1).transpose(0, 2, 1, 3)


def fox_attention(q, k, v, f_logit):
    b, h, l, d = q.shape
    nq = l // Q_BLOCK
    c = jnp.cumsum(jax.nn.log_sigmoid(f_logit.astype(F32)), axis=1).transpose(0, 2, 1)
    qb = q.reshape(b, h, nq, Q_BLOCK, d).transpose(2, 0, 1, 3, 4)
    cb = c.reshape(b, h, nq, Q_BLOCK).transpose(2, 0, 1, 3)
    kpos = jnp.arange(l)
    scale = d ** -0.5

    def block(args):
        qi, ci, i = args
        s = jnp.einsum('bhqd,bhkd->bhqk', qi, k, preferred_element_type=F32) * scale
        s = s + ci[..., None] - c[:, :, None, :]
        qpos = i * Q_BLOCK + jnp.arange(Q_BLOCK)
        s = jnp.where(qpos[:, None] >= kpos[None, :], s, -jnp.inf)
        p = jax.nn.softmax(s, axis=-1)
        return jnp.einsum('bhqk,bhkd->bhqd', p.astype(v.dtype), v)

    o = lax.map(block, (qb, cb, jnp.arange(nq)))
    return o.transpose(1, 0, 3, 2, 4).reshape(b, l, h * d)


def rotary(t):
    d = t.shape[-1]
    l = t.shape[2]
    half = d // 2
    inv = ROPE_BASE ** (-jnp.arange(half, dtype=F32) / half)
    ang = jnp.arange(l, dtype=F32)[:, None] * inv[None, :]
    cos, sin = jnp.cos(ang), jnp.sin(ang)
    t1 = t[..., :half].astype(F32)
    t2 = t[..., half:].astype(F32)
    return jnp.concatenate([t1 * cos - t2 * sin, t1 * sin + t2 * cos], axis=-1).astype(t.dtype)


def retention(q, k, v):
    b, h, l, d = q.shape
    cs = RET_CHUNK
    n = l // cs
    q = rotary(q)
    k = rotary(k) * (d ** -0.5)
    log_g = jnp.log1p(-jnp.exp2(-5.0 - jnp.arange(h, dtype=F32)))
    pos = jnp.arange(cs, dtype=F32)
    diff = pos[:, None] - pos[None, :]
    dmask = jnp.where(diff >= 0, jnp.exp(log_g[:, None, None] * jnp.maximum(diff, 0.0)), 0.0)
    zeta = jnp.exp(log_g[:, None] * (cs - 1 - pos)[None, :])
    xi = jnp.exp(log_g[:, None] * (pos + 1)[None, :])
    g_chunk = jnp.exp(log_g * cs)
    qc = q.reshape(b, h, n, cs, d)
    kc = k.reshape(b, h, n, cs, d)
    vc = v.reshape(b, h, n, cs, d)
    s = jnp.einsum('bhnid,bhnjd->bhnij', qc, kc, preferred_element_type=F32) * dmask[None, :, None]
    o_intra = jnp.einsum('bhnij,bhnje->bhnie', s, vc.astype(F32))
    kv = jnp.einsum('bhnjd,hj,bhnje->nbhde', kc.astype(F32), zeta, vc.astype(F32))

    def step(r, kv_n):
        return g_chunk[None, :, None, None] * r + kv_n, r

    _, r_prev = lax.scan(step, jnp.zeros_like(kv[0]), kv)
    o_cross = jnp.einsum('bhnid,nbhde->bhnie', qc.astype(F32), r_prev) * xi[None, :, None, :, None]
    o = (o_intra + o_cross).reshape(b, h, l, d)
    mu = jnp.mean(o, axis=-1, keepdims=True)
    var = jnp.mean(jnp.square(o - mu), axis=-1, keepdims=True)
    o = (o - mu) * lax.rsqrt(var + EPS)
    return o.transpose(0, 2, 1, 3).reshape(b, l, h * d)


def s5_mixer(u, a_re, a_im, b_re, b_im, c_re, c_im, d_skip, log_step, w_glu):
    bsz, l, _ = u.shape
    uf = u.astype(F32).reshape(bsz, l, S5_GROUPS, S5_GROUP)
    lam = lax.complex(jnp.minimum(a_re.astype(F32), -1e-4), a_im.astype(F32))
    step = jnp.exp(log_step.astype(F32))
    a_bar = jnp.exp(lam * step)
    b_bar = ((a_bar - 1.0) / lam)[..., None] * lax.complex(b_re.astype(F32), b_im.astype(F32))
    bu = jnp.einsum('blgc,gpc->blgp', uf.astype(jnp.complex64), b_bar)
    a_seq = jnp.broadcast_to(a_bar, (l,) + a_bar.shape)

    def combine(e1, e2):
        a1, b1 = e1
        a2, b2 = e2
        return a1 * a2, a2 * b1 + b2

    states = jax.vmap(lambda bu_b: lax.associative_scan(combine, (a_seq, bu_b), axis=0)[1])(bu)
    c_mat = lax.complex(c_re.astype(F32), c_im.astype(F32))
    y = jnp.einsum('blgp,gcp->blgc', states, c_mat).real + d_skip.astype(F32) * uf
    y = y.reshape(bsz, l, S5_WIDTH)
    g = jax.nn.gelu(y)
    return (g * jax.nn.sigmoid(g @ w_glu.astype(F32))).astype(u.dtype)


def short_conv_mixer(h, gate_b, gate_c, conv_w):
    z = gate_c * h
    rhs = conv_w[:, None, :].astype(z.dtype)
    y = lax.conv_general_dilated(z, rhs, window_strides=(1,), padding=[(CONV_K - 1, 0)],
                                 dimension_numbers=('NWC', 'WIO', 'NWC'),
                                 feature_group_count=z.shape[-1])
    return gate_b * y


def even_mixer(xn, w_in, b_forget, w_out):
    proj = xn @ w_in
    cuts = [FOX_WIDTH, 2 * FOX_WIDTH, 3 * FOX_WIDTH, 3 * FOX_WIDTH + FOX_HEADS,
            3 * FOX_WIDTH + FOX_HEADS + RET_WIDTH, 3 * FOX_WIDTH + FOX_HEADS + 2 * RET_WIDTH,
            3 * FOX_WIDTH + FOX_HEADS + 3 * RET_WIDTH]
    fq, fk, fv, ff, rq, rk, rv, rg = jnp.split(proj, cuts, axis=-1)
    fox = fox_attention(to_heads(fq, FOX_HEADS), to_heads(fk, FOX_HEADS), to_heads(fv, FOX_HEADS), ff + b_forget)
    ret = retention(to_heads(rq, RET_HEADS), to_heads(rk, RET_HEADS), to_heads(rv, RET_HEADS))
    ret = (jax.nn.silu(rg.astype(F32)) * ret).astype(xn.dtype)
    return jnp.concatenate([fox.astype(xn.dtype), ret], axis=-1) @ w_out


def odd_mixer(xn, w_in, a_re, a_im, b_re, b_im, c_re, c_im, d_skip, log_step, w_glu, conv_w, w_out):
    proj = xn @ w_in
    cuts = [S5_WIDTH, S5_WIDTH + CONV_WIDTH, S5_WIDTH + 2 * CONV_WIDTH]
    u, h, gb, gc = jnp.split(proj, cuts, axis=-1)
    ssm = s5_mixer(u, a_re, a_im, b_re, b_im, c_re, c_im, d_skip, log_step, w_glu)
    conv = short_conv_mixer(h, gb, gc, conv_w).astype(xn.dtype)
    return jnp.concatenate([ssm, conv], axis=-1) @ w_out


def cross_attention(xn, memn, wq, wkv, wo):
    b, l, _ = xn.shape
    m = memn.shape[1]
    q = (xn @ wq).reshape(b, l, XA_HEADS, XA_HEAD_DIM)
    k, v = jnp.split(memn @ wkv, 2, axis=-1)
    k = k.reshape(b, m, XA_HEADS, XA_HEAD_DIM)
    v = v.reshape(b, m, XA_HEADS, XA_HEAD_DIM)
    s = jnp.einsum('blhd,bmhd->bhlm', q, k, preferred_element_type=F32) * (XA_HEAD_DIM ** -0.5)
    p = jax.nn.softmax(s, axis=-1)
    o = jnp.einsum('bhlm,bmhd->blhd', p.astype(v.dtype), v).reshape(b, l, XA_WIDTH)
    return o @ wo


def sq_relu_mlp(xn, w1, w2):
    return jnp.square(jax.nn.relu(xn @ w1)) @ w2


def setup_inputs(seed: int = 0) -> dict:
    key = jax.random.key(seed)
    ks = jax.random.split(key, 32)

    def nrm(k, shape, scale):
        return jax.random.normal(k, shape, F32) * scale

    def gain(k, shape):
        return 1.0 + 0.05 * jax.random.normal(k, shape, F32)

    n_idx = jnp.arange(S5_STATE, dtype=F32)
    a_im0 = jnp.broadcast_to(math.pi * n_idx, (N_ODD, S5_GROUPS, S5_STATE))
    return {
        'x': nrm(ks[0], (BATCH, SEQ, D_MODEL), 1.0),
        'mem': nrm(ks[1], (BATCH, MEM_LEN, D_MODEL), 1.0),
        'norm_mix': gain(ks[2], (DEPTH, D_MODEL)),
        'norm_xattn': gain(ks[3], (DEPTH, D_MODEL)),
        'norm_mlp': gain(ks[4], (DEPTH, D_MODEL)),
        'norm_mem': gain(ks[5], (D_MODEL,)),
        'norm_final': gain(ks[6], (D_MODEL,)),
        'ab_w_in': nrm(ks[7], (N_EVEN, D_MODEL, EVEN_IN), D_MODEL ** -0.5),
        'ab_b_forget': 3.0 + 0.1 * jax.random.normal(ks[8], (N_EVEN, FOX_HEADS), F32),
        'ab_w_out': nrm(ks[9], (N_EVEN, MIX_OUT, D_MODEL), MIX_OUT ** -0.5),
        'cd_w_in': nrm(ks[10], (N_ODD, D_MODEL, ODD_IN), D_MODEL ** -0.5),
        's5_a_re': -0.5 + 0.01 * jax.random.normal(ks[11], (N_ODD, S5_GROUPS, S5_STATE), F32),
        's5_a_im': a_im0 + 0.01 * jax.random.normal(ks[12], (N_ODD, S5_GROUPS, S5_STATE), F32),
        's5_b_re': nrm(ks[13], (N_ODD, S5_GROUPS, S5_STATE, S5_GROUP), (2 * S5_GROUP) ** -0.5),
        's5_b_im': nrm(ks[14], (N_ODD, S5_GROUPS, S5_STATE, S5_GROUP), (2 * S5_GROUP) ** -0.5),
        's5_c_re': nrm(ks[15], (N_ODD, S5_GROUPS, S5_GROUP, S5_STATE), (2 * S5_STATE) ** -0.5),
        's5_c_im': nrm(ks[16], (N_ODD, S5_GROUPS, S5_GROUP, S5_STATE), (2 * S5_STATE) ** -0.5),
        's5_d': nrm(ks[17], (N_ODD, S5_GROUPS, S5_GROUP), 1.0),
        's5_log_step': jax.random.uniform(ks[18], (N_ODD, S5_GROUPS, S5_STATE), F32,
                                          minval=math.log(1e-3), maxval=math.log(1e-1)),
        's5_w_glu': nrm(ks[19], (N_ODD, S5_WIDTH, S5_WIDTH), S5_WIDTH ** -0.5),
        'conv_w': nrm(ks[20], (N_ODD, CONV_K, CONV_WIDTH), CONV_K ** -0.5),
        'cd_w_out': nrm(ks[21], (N_ODD, MIX_OUT, D_MODEL), MIX_OUT ** -0.5),
        'xa_wq': nrm(ks[22], (DEPTH, D_MODEL, XA_WIDTH), D_MODEL ** -0.5),
        'xa_wkv': nrm(ks[23], (DEPTH, D_MODEL, 2 * XA_WIDTH), D_MODEL ** -0.5),
        'xa_wo': nrm(ks[24], (DEPTH, XA_WIDTH, D_MODEL), XA_WIDTH ** -0.5),
        'mlp_w1': nrm(ks[25], (DEPTH, D_MODEL, D_FF), D_MODEL ** -0.5),
        'mlp_w2': nrm(ks[26], (DEPTH, D_FF, D_MODEL), D_FF ** -0.5),
    }


def reference(x, mem, norm_mix, norm_xattn, norm_mlp, norm_mem, norm_final,
              ab_w_in, ab_b_forget, ab_w_out, cd_w_in,
              s5_a_re, s5_a_im, s5_b_re, s5_b_im, s5_c_re, s5_c_im, s5_d, s5_log_step, s5_w_glu,
              conv_w, cd_w_out, xa_wq, xa_wkv, xa_wo, mlp_w1, mlp_w2):
    memn = rmsnorm(mem, norm_mem)
    h = x
    for layer in range(DEPTH):
        xn = rmsnorm(h, norm_mix[layer])
        if layer % 2 == 0:
            e = layer // 2
            mix = even_mixer(xn, ab_w_in[e], ab_b_forget[e], ab_w_out[e])
        else:
            o = layer // 2
            mix = odd_mixer(xn, cd_w_in[o], s5_a_re[o], s5_a_im[o], s5_b_re[o], s5_b_im[o],
                            s5_c_re[o], s5_c_im[o], s5_d[o], s5_log_step[o], s5_w_glu[o],
                            conv_w[o], cd_w_out[o])
        h = h + mix.astype(h.dtype)
        xa = cross_attention(rmsnorm(h, norm_xattn[layer]), memn, xa_wq[layer], xa_wkv[layer], xa_wo[layer])
        h = h + xa.astype(h.dtype)
        ff = sq_relu_mlp(rmsnorm(h, norm_mlp[layer]), mlp_w1[layer], mlp_w2[layer])
        h = h + ff.astype(h.dtype)
    return rmsnorm(h, norm_final)
```

```python
import functools
import math

import jax
import jax.numpy as jnp
from jax import lax
from jax.experimental import pallas as pl
from jax.experimental.pallas import tpu as pltpu

F32 = jnp.float32
BF16 = jnp.bfloat16

EPS = 1e-6
ROPE_BASE = 10000.0
LANES = 128
HEAD_DIM = 128
FOX_HEADS = 8
RET_HEADS = 8
RET_CHUNK = 128
XA_HEADS = 4
S5_GROUP = 16
S5_STATE = 64
S5_CHUNK = 16
CONV_K = 3
GROUPS_PER_BLOCK = LANES // S5_GROUP
STATE_LANES = GROUPS_PER_BLOCK * S5_STATE
VMEM_LIMIT = 56 * 1024 * 1024
NEG_BIG = -1e30


def _params(*sem):
    return pltpu.CompilerParams(dimension_semantics=sem, vmem_limit_bytes=VMEM_LIMIT)


def _rms(x, g):
    ms = jnp.mean(x * x, axis=-1, keepdims=True)
    return x * lax.rsqrt(ms + EPS) * g


def _norm_matmul_kernel(x_ref, g_ref, w_ref, o_ref, xn_ref):
    @pl.when(pl.program_id(1) == 0)
    def _():
        xn_ref[...] = _rms(x_ref[...], g_ref[...]).astype(BF16)

    o_ref[...] = jnp.dot(xn_ref[...], w_ref[...], preferred_element_type=F32).astype(o_ref.dtype)


def norm_matmul(x, g, w, out_dtype, tm, tn):
    t, d = x.shape
    n = w.shape[1]
    return pl.pallas_call(
        _norm_matmul_kernel,
        grid=(t // tm, n // tn),
        in_specs=[
            pl.BlockSpec((tm, d), lambda i, j: (i, 0)),
            pl.BlockSpec((1, d), lambda i, j: (0, 0)),
            pl.BlockSpec((d, tn), lambda i, j: (0, j)),
        ],
        out_specs=pl.BlockSpec((tm, tn), lambda i, j: (i, j)),
        out_shape=jax.ShapeDtypeStruct((t, n), out_dtype),
        scratch_shapes=[pltpu.VMEM((tm, d), BF16)],
        compiler_params=_params("parallel", "arbitrary"),
        name="norm_matmul",
    )(x, g.reshape(1, d), w)


def _mix_out_kernel(h_ref, a1_ref, a2_ref, w1_ref, w2_ref, o_ref):
    acc = jnp.dot(a1_ref[...], w1_ref[...], preferred_element_type=F32)
    acc = acc + jnp.dot(a2_ref[...], w2_ref[...], preferred_element_type=F32)
    o_ref[...] = h_ref[...] + acc


def mix_out(h, a1, a2, w1, w2, tm, tn):
    t, d = h.shape
    k1 = a1.shape[1]
    k2 = a2.shape[1]
    return pl.pallas_call(
        _mix_out_kernel,
        grid=(t // tm, d // tn),
        in_specs=[
            pl.BlockSpec((tm, tn), lambda i, j: (i, j)),
            pl.BlockSpec((tm, k1), lambda i, j: (i, 0)),
            pl.BlockSpec((tm, k2), lambda i, j: (i, 0)),
            pl.BlockSpec((k1, tn), lambda i, j: (0, j)),
            pl.BlockSpec((k2, tn), lambda i, j: (0, j)),
        ],
        out_specs=pl.BlockSpec((tm, tn), lambda i, j: (i, j)),
        out_shape=jax.ShapeDtypeStruct((t, d), F32),
        compiler_params=_params("parallel", "arbitrary"),
        name="mix_out",
    )(h, a1, a2, w1, w2)


def _mlp_kernel(h_ref, g_ref, w1_ref, w2_ref, gf_ref, o_ref, xn_ref, acc_ref, *, final_norm):
    f = pl.program_id(1)

    @pl.when(f == 0)
    def _():
        xn_ref[...] = _rms(h_ref[...], g_ref[...]).astype(BF16)

    a = jnp.dot(xn_ref[...], w1_ref[...], preferred_element_type=F32)
    a = jnp.square(jnp.maximum(a, 0.0)).astype(BF16)
    part = jnp.dot(a, w2_ref[...], preferred_element_type=F32)

    @pl.when(f == 0)
    def _():
        acc_ref[...] = part

    @pl.when(f > 0)
    def _():
        acc_ref[...] += part

    @pl.when(f == pl.num_programs(1) - 1)
    def _():
        y = h_ref[...] + acc_ref[...]
        if final_norm:
            y = _rms(y, gf_ref[...])
        o_ref[...] = y


def mlp_block(h, g, w1, w2, g_final, final_norm, tm, tf):
    t, d = h.shape
    dff = w1.shape[1]
    return pl.pallas_call(
        functools.partial(_mlp_kernel, final_norm=final_norm),
        grid=(t // tm, dff // tf),
        in_specs=[
            pl.BlockSpec((tm, d), lambda i, f: (i, 0)),
            pl.BlockSpec((1, d), lambda i, f: (0, 0)),
            pl.BlockSpec((d, tf), lambda i, f: (0, f)),
            pl.BlockSpec((tf, d), lambda i, f: (f, 0)),
            pl.BlockSpec((1, d), lambda i, f: (0, 0)),
        ],
        out_specs=pl.BlockSpec((tm, d), lambda i, f: (i, 0)),
        out_shape=jax.ShapeDtypeStruct((t, d), F32),
        scratch_shapes=[pltpu.VMEM((tm, d), BF16), pltpu.VMEM((tm, d), F32)],
        compiler_params=_params("parallel", "arbitrary"),
        name="mlp_block",
    )(h, g.reshape(1, d), w1, w2, g_final.reshape(1, d))


def _xattn_kernel(h_ref, g_ref, wq_ref, k_ref, v_ref, wo_ref, o_ref):
    x = h_ref[...]
    xn = _rms(x, g_ref[...]).astype(BF16)
    q = jnp.dot(xn, wq_ref[...], preferred_element_type=F32).astype(BF16)
    heads = []
    for hd in range(XA_HEADS):
        sl = slice(hd * HEAD_DIM, (hd + 1) * HEAD_DIM)
        s = lax.dot_general(q[:, sl], k_ref[0, :, sl], (((1,), (1,)), ((), ())),
                            preferred_element_type=F32)
        m = jnp.max(s, axis=-1, keepdims=True)
        p = jnp.exp(s - m)
        l = jnp.sum(p, axis=-1, keepdims=True)
        oh = jnp.dot(p.astype(BF16), v_ref[0, :, sl], preferred_element_type=F32)
        heads.append((oh * (1.0 / l)).astype(BF16))
    o = jnp.concatenate(heads, axis=-1)
    o_ref[...] = x + jnp.dot(o, wo_ref[...], preferred_element_type=F32)


def xattn_block(h, g, wq, kv, wo, seq, tm):
    t, d = h.shape
    mlen = kv.shape[1]
    xw = wq.shape[1]
    per_batch = seq // tm
    return pl.pallas_call(
        _xattn_kernel,
        grid=(t // tm,),
        in_specs=[
            pl.BlockSpec((tm, d), lambda i: (i, 0)),
            pl.BlockSpec((1, d), lambda i: (0, 0)),
            pl.BlockSpec((d, xw), lambda i: (0, 0)),
            pl.BlockSpec((1, mlen, xw), lambda i: (i // per_batch, 0, 0)),
            pl.BlockSpec((1, mlen, xw), lambda i: (i // per_batch, 0, 1)),
            pl.BlockSpec((xw, d), lambda i: (0, 0)),
        ],
        out_specs=pl.BlockSpec((tm, d), lambda i: (i, 0)),
        out_shape=jax.ShapeDtypeStruct((t, d), F32),
        compiler_params=_params("parallel"),
        name="xattn_block",
    )(h, g.reshape(1, d), wq, kv, kv, wo)


def _split3(x):
    hi = x.astype(BF16)
    r1 = x - hi.astype(F32)
    mid = r1.astype(BF16)
    lo = (r1 - mid.astype(F32)).astype(BF16)
    return hi, mid, lo


def _fox_cumsum_kernel(fl_ref, b_ref, call_ref, crow_ref, *, blk):
    seq = fl_ref.shape[0]
    row = lax.broadcasted_iota(jnp.int32, (blk, blk), 0)
    col = lax.broadcasted_iota(jnp.int32, (blk, blk), 1)
    tri = jnp.where(row >= col, 1.0, 0.0).astype(BF16)
    carry = jnp.zeros((1, LANES), F32)
    for i in range(seq // blk):
        rows = slice(i * blk, (i + 1) * blk)
        x = fl_ref[rows, :] + b_ref[...]
        ls = jnp.minimum(x, 0.0) - jnp.log(1.0 + jnp.exp(-jnp.abs(x)))
        c = carry
        for piece in _split3(ls):
            c = c + jnp.dot(tri, piece, preferred_element_type=F32)
        call_ref[rows, :] = c
        crow_ref[0, :, rows] = jnp.transpose(c)[:FOX_HEADS, :]
        carry = c[blk - 1:blk, :]


def fox_cumsum(fl, bias, batch, seq):
    blk = min(256, seq)
    return pl.pallas_call(
        functools.partial(_fox_cumsum_kernel, blk=blk),
        grid=(batch,),
        in_specs=[
            pl.BlockSpec((seq, LANES), lambda b: (b, 0)),
            pl.BlockSpec((1, LANES), lambda b: (0, 0)),
        ],
        out_specs=[
            pl.BlockSpec((seq, LANES), lambda b: (b, 0)),
            pl.BlockSpec((1, FOX_HEADS, seq), lambda b: (b, 0, 0)),
        ],
        out_shape=[
            jax.ShapeDtypeStruct((batch * seq, LANES), F32),
            jax.ShapeDtypeStruct((batch, FOX_HEADS, seq), F32),
        ],
        compiler_params=_params("parallel"),
        name="fox_cumsum",
    )(fl, bias)


def _fox_attn_kernel(q_ref, k_ref, v_ref, call_ref, crow_ref, o_ref, m_ref, l_ref, acc_ref, *, blk):
    h = pl.program_id(1)
    i = pl.program_id(2)
    q = q_ref[...]
    lane = lax.broadcasted_iota(jnp.int32, (blk, LANES), 1)
    ccol = jnp.sum(jnp.where(lane == h, call_ref[...], 0.0), axis=1, keepdims=True)
    m_ref[...] = jnp.full(m_ref.shape, NEG_BIG, F32)
    l_ref[...] = jnp.zeros(l_ref.shape, F32)
    acc_ref[...] = jnp.zeros(acc_ref.shape, F32)

    def step(j, masked):
        start = pl.multiple_of(j * blk, blk)
        k = k_ref[pl.ds(start, blk), :]
        v = v_ref[pl.ds(start, blk), :]
        s = lax.dot_general(q, k, (((1,), (1,)), ((), ())), preferred_element_type=F32)
        crow = crow_ref[0, pl.ds(h, 1), pl.ds(start, blk)]
        s = s + (ccol - crow)
        if masked:
            r = lax.broadcasted_iota(jnp.int32, (blk, blk), 0)
            c = lax.broadcasted_iota(jnp.int32, (blk, blk), 1)
            s = jnp.where(r >= c, s, -jnp.inf)
        m_prev = m_ref[...]
        m_new = jnp.maximum(m_prev, jnp.max(s, axis=-1, keepdims=True))
        alpha = jnp.exp(m_prev - m_new)
        p = jnp.exp(s - m_new)
        l_ref[...] = alpha * l_ref[...] + jnp.sum(p, axis=-1, keepdims=True)
        acc_ref[...] = alpha * acc_ref[...] + jnp.dot(p.astype(BF16), v, preferred_element_type=F32)
        m_ref[...] = m_new

    def body(j, carry):
        step(j, False)
        return carry

    lax.fori_loop(0, i, body, 0)
    step(i, True)
    o_ref[...] = (acc_ref[...] * (1.0 / l_ref[...])).astype(o_ref.dtype)


def fox_attention(proj, call, crow, batch, seq, q_col, k_col, v_col, blk):
    nq = seq // blk
    return pl.pallas_call(
        functools.partial(_fox_attn_kernel, blk=blk),
        grid=(batch, FOX_HEADS, nq),
        in_specs=[
            pl.BlockSpec((blk, HEAD_DIM), lambda b, h, i: (b * nq + i, q_col + h)),
            pl.BlockSpec((seq, HEAD_DIM), lambda b, h, i: (b, k_col + h)),
            pl.BlockSpec((seq, HEAD_DIM), lambda b, h, i: (b, v_col + h)),
            pl.BlockSpec((blk, LANES), lambda b, h, i: (b * nq + i, 0)),
            pl.BlockSpec((1, FOX_HEADS, seq), lambda b, h, i: (b, 0, 0)),
        ],
        out_specs=pl.BlockSpec((blk, HEAD_DIM), lambda b, h, i: (b * nq + i, h)),
        out_shape=jax.ShapeDtypeStruct((batch * seq, FOX_HEADS * HEAD_DIM), BF16),
        scratch_shapes=[
            pltpu.VMEM((blk, 1), F32),
            pltpu.VMEM((blk, 1), F32),
            pltpu.VMEM((blk, HEAD_DIM), F32),
        ],
        compiler_params=_params("parallel", "parallel", "arbitrary"),
        name="fox_attention",
    )(proj, proj, proj, call, crow)


def _retention_kernel(q_ref, k_ref, v_ref, g_ref, cos_ref, sin_ref, dm_ref, aux_ref, o_ref, r_ref, *, cs):
    seq = q_ref.shape[0]
    r_ref[...] = jnp.zeros(r_ref.shape, F32)
    dm = dm_ref[0]
    aux = aux_ref[0]
    xi = aux[:, 0:1]
    zeta = aux[:, 1:2]
    g_chunk = aux[:, 2:3]
    half = HEAD_DIM // 2

    def body(c, carry):
        rows = pl.ds(pl.multiple_of(c * cs, cs), cs)
        cos = cos_ref[rows, :]
        sin = sin_ref[rows, :]
        q = q_ref[rows, :].astype(F32)
        k = k_ref[rows, :].astype(F32)
        v = v_ref[rows, :]
        qr = q * cos + pltpu.roll(q, half, 1) * sin
        kr = k * cos + pltpu.roll(k, half, 1) * sin
        qb = qr.astype(BF16)
        s = lax.dot_general(qb, kr.astype(BF16), (((1,), (1,)), ((), ())),
                            preferred_element_type=F32) * dm
        o = jnp.dot(s.astype(BF16), v, preferred_element_type=F32)
        r = r_ref[...]
        o = o + jnp.dot(qb, r.astype(BF16), preferred_element_type=F32) * xi
        kz = (kr * zeta).astype(BF16)
        r_ref[...] = g_chunk * r + lax.dot_general(kz, v, (((0,), (0,)), ((), ())),
                                                   preferred_element_type=F32)
        mu = jnp.mean(o, axis=-1, keepdims=True)
        oc = o - mu
        var = jnp.mean(oc * oc, axis=-1, keepdims=True)
        on = oc * lax.rsqrt(var + EPS)
        g = g_ref[rows, :].astype(F32)
        o_ref[rows, :] = (g * jax.nn.sigmoid(g) * on).astype(o_ref.dtype)
        return carry

    lax.fori_loop(0, seq // cs, body, 0)


def retention(proj, cos, sin, dm, aux, batch, seq, q_col, k_col, v_col, g_col):
    def col(c0):
        return pl.BlockSpec((seq, HEAD_DIM), lambda b, h: (b, c0 + h))

    return pl.pallas_call(
        functools.partial(_retention_kernel, cs=RET_CHUNK),
        grid=(batch, RET_HEADS),
        in_specs=[
            col(q_col), col(k_col), col(v_col), col(g_col),
            pl.BlockSpec((seq, HEAD_DIM), lambda b, h: (0, 0)),
            pl.BlockSpec((seq, HEAD_DIM), lambda b, h: (0, 0)),
            pl.BlockSpec((1, RET_CHUNK, RET_CHUNK), lambda b, h: (h, 0, 0)),
            pl.BlockSpec((1, RET_CHUNK, LANES), lambda b, h: (h, 0, 0)),
        ],
        out_specs=pl.BlockSpec((seq, HEAD_DIM), lambda b, h: (b, h)),
        out_shape=jax.ShapeDtypeStruct((batch * seq, RET_HEADS * HEAD_DIM), BF16),
        scratch_shapes=[pltpu.VMEM((HEAD_DIM, HEAD_DIM), F32)],
        compiler_params=_params("parallel", "parallel"),
        name="retention",
    )(proj, proj, proj, proj, cos, sin, dm, aux)


def _retention_tables(seq):
    half = HEAD_DIM // 2
    inv = ROPE_BASE ** (-jnp.arange(half, dtype=F32) / half)
    ang = jnp.arange(seq, dtype=F32)[:, None] * inv[None, :]
    cos, sin = jnp.cos(ang), jnp.sin(ang)
    cos_t = jnp.concatenate([cos, cos], axis=-1)
    sin_t = jnp.concatenate([-sin, sin], axis=-1)
    cs = RET_CHUNK
    log_g = jnp.log1p(-jnp.exp2(-5.0 - jnp.arange(RET_HEADS, dtype=F32)))
    pos = jnp.arange(cs, dtype=F32)
    diff = pos[:, None] - pos[None, :]
    dm = jnp.where(diff >= 0, jnp.exp(log_g[:, None, None] * jnp.maximum(diff, 0.0)), 0.0)
    zeta = jnp.exp(log_g[:, None] * (cs - 1 - pos)[None, :])
    xi = jnp.exp(log_g[:, None] * (pos + 1)[None, :])
    g_chunk = jnp.broadcast_to(jnp.exp(log_g * cs)[:, None], (RET_HEADS, cs))
    aux = jnp.zeros((RET_HEADS, cs, LANES), F32)
    aux = aux.at[:, :, 0].set(xi).at[:, :, 1].set(zeta).at[:, :, 2].set(g_chunk)
    return cos_t, sin_t, dm, aux


def _s5_kernel(u_ref, toep_ref, win_ref, wout_ref, aq_ref, d_ref, y_ref, ucat_ref, inc_ref, x_ref, *, q):
    m = u_ref.shape[0] // q
    sl = STATE_LANES
    for s in range(q):
        ucat_ref[:, s * LANES:(s + 1) * LANES] = u_ref[pl.ds(s, m, stride=q), :].astype(BF16)
    ucat = ucat_ref[...]
    inc_ref[...] = jnp.dot(ucat, win_ref[0], preferred_element_type=F32)
    a_re = aq_ref[0, 0:1, :]
    a_im = aq_ref[0, 1:2, :]

    def body(n, carry):
        x_re, x_im = carry
        x_ref[pl.ds(n, 1), 0:sl] = x_re
        x_ref[pl.ds(n, 1), sl:2 * sl] = x_im
        i_re = inc_ref[pl.ds(n, 1), 0:sl]
        i_im = inc_ref[pl.ds(n, 1), sl:2 * sl]
        return (a_re * x_re - a_im * x_im + i_re, a_re * x_im + a_im * x_re + i_im)

    zero = jnp.zeros((1, sl), F32)
    lax.fori_loop(0, m, body, (zero, zero))
    y = jnp.dot(ucat, toep_ref[0], preferred_element_type=F32)
    y = y + jnp.dot(x_ref[...].astype(BF16), wout_ref[0], preferred_element_type=F32)
    for t in range(q):
        rows = pl.ds(t, m, stride=q)
        y_ref[rows, :] = y[:, t * LANES:(t + 1) * LANES] + d_ref[...] * u_ref[rows, :]


def s5_core(proj, toep, win, wout, aq, d_skip, batch, seq, u_col):
    q = S5_CHUNK
    m = seq // q
    nblk = toep.shape[0]
    return pl.pallas_call(
        functools.partial(_s5_kernel, q=q),
        grid=(nblk, batch),
        in_specs=[
            pl.BlockSpec((seq, LANES), lambda j, b: (b, u_col + j)),
            pl.BlockSpec((1, q * LANES, q * LANES), lambda j, b: (j, 0, 0)),
            pl.BlockSpec((1, q * LANES, 2 * STATE_LANES), lambda j, b: (j, 0, 0)),
            pl.BlockSpec((1, 2 * STATE_LANES, q * LANES), lambda j, b: (j, 0, 0)),
            pl.BlockSpec((1, 2, STATE_LANES), lambda j, b: (j, 0, 0)),
            pl.BlockSpec((1, LANES), lambda j, b: (0, j)),
        ],
        out_specs=pl.BlockSpec((seq, LANES), lambda j, b: (b, j)),
        out_shape=jax.ShapeDtypeStruct((batch * seq, nblk * LANES), F32),
        scratch_shapes=[
            pltpu.VMEM((m, q * LANES), BF16),
            pltpu.VMEM((m, 2 * STATE_LANES), F32),
            pltpu.VMEM((m, 2 * STATE_LANES), F32),
        ],
        compiler_params=_params("parallel", "arbitrary"),
        name="s5_core",
    )(proj, toep, win, wout, aq, d_skip)


def _s5_tables(a_re, a_im, b_re, b_im, c_re, c_im, log_step):
    q = S5_CHUNK
    g, p = a_re.shape
    nblk = g // GROUPS_PER_BLOCK
    gb = GROUPS_PER_BLOCK
    lam = lax.complex(jnp.minimum(a_re.astype(F32), -1e-4), a_im.astype(F32))
    step = jnp.exp(log_step.astype(F32))
    a_bar = jnp.exp(lam * step)
    b_bar = ((a_bar - 1.0) / lam)[..., None] * lax.complex(b_re.astype(F32), b_im.astype(F32))
    c_mat = lax.complex(c_re.astype(F32), c_im.astype(F32))
    tau = jnp.arange(q + 1, dtype=F32)
    a_pow = jnp.exp((lam * step)[None] * tau[:, None, None])
    eye = jnp.eye(gb, dtype=F32)

    taps = jnp.einsum('gcp,tgp,gpd->tgcd', c_mat, a_pow[:q], b_bar).real
    s_idx = jnp.arange(q)[:, None]
    t_idx = jnp.arange(q)[None, :]
    lag = t_idx - s_idx
    kts = jnp.where((lag >= 0)[:, :, None, None, None],
                    taps[jnp.clip(lag, 0, q - 1)], 0.0)
    kts = kts.reshape(q, q, nblk, gb, S5_GROUP, S5_GROUP)
    toep = jnp.einsum('stjgcd,gh->jsgdthc', kts, eye)
    toep = toep.reshape(nblk, q * LANES, q * LANES)

    a_rev = jnp.exp((lam * step)[None] * (q - 1 - tau[:q])[:, None, None])
    vin = a_rev[:, :, :, None] * b_bar[None]
    vin = vin.reshape(q, nblk, gb, p, S5_GROUP)
    win_re = jnp.einsum('sjgpd,gh->jsgdhp', vin.real, eye).reshape(nblk, q * LANES, gb * p)
    win_im = jnp.einsum('sjgpd,gh->jsgdhp', vin.imag, eye).reshape(nblk, q * LANES, gb * p)
    win = jnp.concatenate([win_re, win_im], axis=-1)

    wo = c_mat[None] * a_pow[1:][:, :, None, :]
    wo = wo.reshape(q, nblk, gb, S5_GROUP, p)
    wout_re = jnp.einsum('tjgcp,gh->jhptgc', wo.real, eye).reshape(nblk, gb * p, q * LANES)
    wout_im = jnp.einsum('tjgcp,gh->jhptgc', -wo.imag, eye).reshape(nblk, gb * p, q * LANES)
    wout = jnp.concatenate([wout_re, wout_im], axis=1)

    aq = a_pow[q].reshape(nblk, 1, gb * p)
    aq = jnp.concatenate([aq.real, aq.imag], axis=1)
    return toep.astype(BF16), win.astype(BF16), wout.astype(BF16), aq.astype(F32)


def _s5_glu_kernel(y_ref, w_ref, o_ref):
    y = y_ref[...]
    g = 0.5 * y * (1.0 + jnp.tanh(math.sqrt(2.0 / math.pi) * (y + 0.044715 * (y * y * y))))
    z = jnp.dot(g.astype(BF16), w_ref[...], preferred_element_type=F32)
    o_ref[...] = (g * jax.nn.sigmoid(z)).astype(o_ref.dtype)


def s5_glu(y, w, tm):
    t, n = y.shape
    return pl.pallas_call(
        _s5_glu_kernel,
        grid=(t // tm,),
        in_specs=[
            pl.BlockSpec((tm, n), lambda i: (i, 0)),
            pl.BlockSpec((n, n), lambda i: (0, 0)),
        ],
        out_specs=pl.BlockSpec((tm, n), lambda i: (i, 0)),
        out_shape=jax.ShapeDtypeStruct((t, n), BF16),
        compiler_params=_params("parallel"),
        name="s5_glu",
    )(y, w)


def _short_conv_kernel(h_ref, gb_ref, gc_ref, w_ref, o_ref):
    z = gc_ref[...] * h_ref[...]
    row = lax.broadcasted_iota(jnp.int32, z.shape, 0)
    y = w_ref[CONV_K - 1:CONV_K, :] * z
    for lag in range(1, CONV_K):
        zl = jnp.where(row >= lag, pltpu.roll(z, lag, 0), 0.0)
        y = y + w_ref[CONV_K - 1 - lag:CONV_K - lag, :] * zl
    o_ref[...] = (gb_ref[...] * y).astype(o_ref.dtype)


def short_conv(proj, conv_w, batch, seq, h_col, gb_col, gc_col):
    nblk = conv_w.shape[1] // LANES

    def col(c0):
        return pl.BlockSpec((seq, LANES), lambda b, j: (b, c0 + j))

    return pl.pallas_call(
        _short_conv_kernel,
        grid=(batch, nblk),
        in_specs=[col(h_col), col(gb_col), col(gc_col),
                  pl.BlockSpec((CONV_K, LANES), lambda b, j: (0, j))],
        out_specs=pl.BlockSpec((seq, LANES), lambda b, j: (b, j)),
        out_shape=jax.ShapeDtypeStruct((batch * seq, nblk * LANES), BF16),
        compiler_params=_params("parallel", "parallel"),
        name="short_conv",
    )(proj, proj, proj, conv_w)


def _even_mixer(h, g, w_in, b_forget, w_out, tables, batch, seq):
    fw = FOX_HEADS * HEAD_DIM
    rw = RET_HEADS * HEAD_DIM
    scale = HEAD_DIM ** -0.5
    c0 = 3 * fw
    c1 = c0 + FOX_HEADS
    w_main = jnp.concatenate([
        w_in[:, :fw] * scale, w_in[:, fw:c0],
        w_in[:, c1:c1 + rw], w_in[:, c1 + rw:c1 + 2 * rw] * scale, w_in[:, c1 + 2 * rw:],
    ], axis=1).astype(BF16)
    w_forget = jnp.pad(w_in[:, c0:c1], ((0, 0), (0, LANES - FOX_HEADS))).astype(BF16)
    bias = jnp.pad(b_forget.astype(F32), (0, LANES - FOX_HEADS)).reshape(1, LANES)

    proj = norm_matmul(h, g, w_main, BF16, tm=512, tn=512)
    fl = norm_matmul(h, g, w_forget, F32, tm=512, tn=LANES)
    call, crow = fox_cumsum(fl, bias, batch, seq)
    nh = FOX_HEADS
    fox = fox_attention(proj, call, crow, batch, seq, 0, nh, 2 * nh, blk=min(512, seq))
    cos_t, sin_t, dm, aux = tables
    ret = retention(proj, cos_t, sin_t, dm, aux, batch, seq, 3 * nh, 4 * nh, 5 * nh, 6 * nh)
    w_o = w_out.astype(BF16)
    return mix_out(h, fox, ret, w_o[:fw], w_o[fw:], tm=512, tn=512)


def _odd_mixer(h, g, w_in, a_re, a_im, b_re, b_im, c_re, c_im, d_skip, log_step, w_glu, conv_w, w_out,
               batch, seq):
    sw = a_re.shape[0] * S5_GROUP
    nblk = sw // LANES
    proj = norm_matmul(h, g, w_in.astype(BF16), F32, tm=512, tn=512)
    toep, win, wout, aq = _s5_tables(a_re, a_im, b_re, b_im, c_re, c_im, log_step)
    y = s5_core(proj, toep, win, wout, aq, d_skip.reshape(1, sw).astype(F32), batch, seq, 0)
    ssm = s5_glu(y, w_glu.astype(BF16), tm=512)
    conv = short_conv(proj, conv_w.astype(F32), batch, seq, nblk, 2 * nblk, 3 * nblk)
    w_o = w_out.astype(BF16)
    return mix_out(h, ssm, conv, w_o[:sw], w_o[sw:], tm=512, tn=512)


def kernel(x, mem, norm_mix, norm_xattn, norm_mlp, norm_mem, norm_final, ab_w_in, ab_b_forget, ab_w_out,
           cd_w_in, s5_a_re, s5_a_im, s5_b_re, s5_b_im, s5_c_re, s5_c_im, s5_d, s5_log_step, s5_w_glu,
           conv_w, cd_w_out, xa_wq, xa_wkv, xa_wo, mlp_w1, mlp_w2):
    batch, seq, d = x.shape
    depth = norm_mix.shape[0]
    mlen = mem.shape[1]
    xw = xa_wq.shape[2]
    h = x.reshape(batch * seq, d)
    memf = mem.reshape(batch * mlen, d)
    tables = _retention_tables(seq)
    xa_scale = HEAD_DIM ** -0.5
    for layer in range(depth):
        if layer % 2 == 0:
            e = layer // 2
            h = _even_mixer(h, norm_mix[layer], ab_w_in[e], ab_b_forget[e], ab_w_out[e], tables, batch, seq)
        else:
            o = layer // 2
            h = _odd_mixer(h, norm_mix[layer], cd_w_in[o], s5_a_re[o], s5_a_im[o], s5_b_re[o], s5_b_im[o],
                           s5_c_re[o], s5_c_im[o], s5_d[o], s5_log_step[o], s5_w_glu[o], conv_w[o],
                           cd_w_out[o], batch, seq)
        kv = norm_matmul(memf, norm_mem, xa_wkv[layer].astype(BF16), BF16, tm=min(512, batch * mlen), tn=512)
        kv = kv.reshape(batch, mlen, 2 * xw)
        h = xattn_block(h, norm_xattn[layer], (xa_wq[layer] * xa_scale).astype(BF16), kv,
                        xa_wo[layer].astype(BF16), seq, tm=512)
        h = mlp_block(h, norm_mlp[layer], mlp_w1[layer].astype(BF16), mlp_w2[layer].astype(BF16),
                      norm_final, layer == depth - 1, tm=512, tf=512)
    return h.reshape(batch, seq, d)
```

```python
import functools
import math

import jax
import jax.numpy as jnp
from jax import lax
from jax.experimental import pallas as pl
from jax.experimental.pallas import tpu as pltpu

F32 = jnp.float32
BF16 = jnp.bfloat16

EPS = 1e-6
ROPE_BASE = 10000.0
LANES = 128
HEAD_DIM = 128
FOX_HEADS = 8
RET_HEADS = 8
RET_CHUNK = 128
XA_HEADS = 4
S5_GROUP = 16
S5_STATE = 64
S5_CHUNK = 16
CONV_K = 3
S5_GROUP_SHIFT = S5_GROUP.bit_length() - 1
S5_STATE_SHIFT = S5_STATE.bit_length() - 1
GROUPS_PER_BLOCK = LANES // S5_GROUP
STATE_LANES = GROUPS_PER_BLOCK * S5_STATE
VMEM_LIMIT = 56 * 1024 * 1024
NEG_BIG = -1e30

PROJ_ROWS, PROJ_COLS = 1024, 512
MIX_ROWS = 512
MLP_ROWS, MLP_COLS = 1024, 512
FOX_BLOCK = 512


def _row_tile(rows, want):
    tile = min(rows, want)
    while rows % tile:
        tile //= 2
    return tile


def _params(*sem):
    return pltpu.CompilerParams(dimension_semantics=sem, vmem_limit_bytes=VMEM_LIMIT)


def _rms(x, g):
    ms = jnp.mean(x * x, axis=-1, keepdims=True)
    return x * lax.rsqrt(ms + EPS) * g


def _norm_matmul_kernel(x_ref, g_ref, w_ref, o_ref, xn_ref):
    @pl.when(pl.program_id(1) == 0)
    def _():
        xn_ref[...] = _rms(x_ref[...], g_ref[...]).astype(BF16)

    o_ref[...] = jnp.dot(xn_ref[...], w_ref[...], preferred_element_type=F32).astype(o_ref.dtype)


def norm_matmul(x, g, w, out_dtype, tm, tn):
    t, d = x.shape
    n = w.shape[1]
    return pl.pallas_call(
        _norm_matmul_kernel,
        grid=(t // tm, n // tn),
        in_specs=[
            pl.BlockSpec((tm, d), lambda i, j: (i, 0)),
            pl.BlockSpec((1, d), lambda i, j: (0, 0)),
            pl.BlockSpec((d, tn), lambda i, j: (0, j)),
        ],
        out_specs=pl.BlockSpec((tm, tn), lambda i, j: (i, j)),
        out_shape=jax.ShapeDtypeStruct((t, n), out_dtype),
        scratch_shapes=[pltpu.VMEM((tm, d), BF16)],
        compiler_params=_params("parallel", "arbitrary"),
        name="norm_matmul",
    )(x, g.reshape(1, d), w)


def _mix_out_kernel(h_ref, a1_ref, a2_ref, w1_ref, w2_ref, o_ref):
    acc = jnp.dot(a1_ref[...], w1_ref[...], preferred_element_type=F32)
    acc = acc + jnp.dot(a2_ref[...], w2_ref[...], preferred_element_type=F32)
    o_ref[...] = h_ref[...] + acc


def mix_out(h, a1, a2, w1, w2, tm, tn):
    t, d = h.shape
    k1 = a1.shape[1]
    k2 = a2.shape[1]
    return pl.pallas_call(
        _mix_out_kernel,
        grid=(t // tm, d // tn),
        in_specs=[
            pl.BlockSpec((tm, tn), lambda i, j: (i, j)),
            pl.BlockSpec((tm, k1), lambda i, j: (i, 0)),
            pl.BlockSpec((tm, k2), lambda i, j: (i, 0)),
            pl.BlockSpec((k1, tn), lambda i, j: (0, j)),
            pl.BlockSpec((k2, tn), lambda i, j: (0, j)),
        ],
        out_specs=pl.BlockSpec((tm, tn), lambda i, j: (i, j)),
        out_shape=jax.ShapeDtypeStruct((t, d), F32),
        compiler_params=_params("parallel", "arbitrary"),
        name="mix_out",
    )(h, a1, a2, w1, w2)


def _mlp_kernel(h_ref, g_ref, w1_ref, w2_ref, gf_ref, o_ref, xn_ref, *, final_norm):
    f = pl.program_id(1)

    @pl.when(f == 0)
    def _():
        xn_ref[...] = _rms(h_ref[...], g_ref[...]).astype(BF16)

    a = jnp.dot(xn_ref[...], w1_ref[...], preferred_element_type=F32)
    a = jnp.square(jnp.maximum(a, 0.0)).astype(BF16)
    part = jnp.dot(a, w2_ref[...], preferred_element_type=F32)

    @pl.when(f == 0)
    def _():
        o_ref[...] = h_ref[...] + part

    @pl.when(f > 0)
    def _():
        o_ref[...] += part

    if final_norm:
        @pl.when(f == pl.num_programs(1) - 1)
        def _():
            o_ref[...] = _rms(o_ref[...], gf_ref[...])


def mlp_block(h, g, w1, w2, g_final, final_norm, tm, tf):
    t, d = h.shape
    dff = w1.shape[1]
    return pl.pallas_call(
        functools.partial(_mlp_kernel, final_norm=final_norm),
        grid=(t // tm, dff // tf),
        in_specs=[
            pl.BlockSpec((tm, d), lambda i, f: (i, 0)),
            pl.BlockSpec((1, d), lambda i, f: (0, 0)),
            pl.BlockSpec((d, tf), lambda i, f: (0, f)),
            pl.BlockSpec((tf, d), lambda i, f: (f, 0)),
            pl.BlockSpec((1, d), lambda i, f: (0, 0)),
        ],
        out_specs=pl.BlockSpec((tm, d), lambda i, f: (i, 0)),
        out_shape=jax.ShapeDtypeStruct((t, d), F32),
        scratch_shapes=[pltpu.VMEM((tm, d), BF16)],
        compiler_params=_params("parallel", "arbitrary"),
        name="mlp_block",
    )(h, g.reshape(1, d), w1, w2, g_final.reshape(1, d))


def _xattn_kernel(h_ref, g_ref, wq_ref, k_ref, v_ref, wo_ref, o_ref):
    x = h_ref[...]
    xn = _rms(x, g_ref[...]).astype(BF16)
    q = jnp.dot(xn, wq_ref[...], preferred_element_type=F32).astype(BF16)
    heads = []
    for hd in range(XA_HEADS):
        sl = slice(hd * HEAD_DIM, (hd + 1) * HEAD_DIM)
        s = lax.dot_general(q[:, sl], k_ref[0, :, sl], (((1,), (1,)), ((), ())),
                            preferred_element_type=F32)
        m = jnp.max(s, axis=-1, keepdims=True)
        p = jnp.exp(s - m)
        l = jnp.sum(p, axis=-1, keepdims=True)
        oh = jnp.dot(p.astype(BF16), v_ref[0, :, sl], preferred_element_type=F32)
        heads.append((oh * (1.0 / l)).astype(BF16))
    o = jnp.concatenate(heads, axis=-1)
    o_ref[...] = x + jnp.dot(o, wo_ref[...], preferred_element_type=F32)


def xattn_block(h, g, wq, kv, wo, seq, tm):
    t, d = h.shape
    mlen = kv.shape[1]
    xw = wq.shape[1]
    per_batch = seq // tm
    return pl.pallas_call(
        _xattn_kernel,
        grid=(t // tm,),
        in_specs=[
            pl.BlockSpec((tm, d), lambda i: (i, 0)),
            pl.BlockSpec((1, d), lambda i: (0, 0)),
            pl.BlockSpec((d, xw), lambda i: (0, 0)),
            pl.BlockSpec((1, mlen, xw), lambda i: (i // per_batch, 0, 0)),
            pl.BlockSpec((1, mlen, xw), lambda i: (i // per_batch, 0, 1)),
            pl.BlockSpec((xw, d), lambda i: (0, 0)),
        ],
        out_specs=pl.BlockSpec((tm, d), lambda i: (i, 0)),
        out_shape=jax.ShapeDtypeStruct((t, d), F32),
        compiler_params=_params("parallel"),
        name="xattn_block",
    )(h, g.reshape(1, d), wq, kv, kv, wo)


def _split3(x):
    hi = x.astype(BF16)
    r1 = x - hi.astype(F32)
    mid = r1.astype(BF16)
    lo = (r1 - mid.astype(F32)).astype(BF16)
    return hi, mid, lo


def _fox_cumsum_kernel(fl_ref, b_ref, call_ref, *, blk):
    seq = fl_ref.shape[0]
    row = lax.broadcasted_iota(jnp.int32, (blk, blk), 0)
    col = lax.broadcasted_iota(jnp.int32, (blk, blk), 1)
    tri = jnp.where(row >= col, 1.0, 0.0).astype(BF16)
    carry = jnp.zeros((1, LANES), F32)
    for i in range(seq // blk):
        rows = slice(i * blk, (i + 1) * blk)
        x = fl_ref[rows, :] + b_ref[...]
        ls = jnp.minimum(x, 0.0) - jnp.log(1.0 + jnp.exp(-jnp.abs(x)))
        c = carry
        for piece in _split3(ls):
            c = c + jnp.dot(tri, piece, preferred_element_type=F32)
        call_ref[rows, :] = c
        carry = c[blk - 1:blk, :]


def fox_cumsum(fl, bias, batch, seq):
    blk = min(256, seq)
    return pl.pallas_call(
        functools.partial(_fox_cumsum_kernel, blk=blk),
        grid=(batch,),
        in_specs=[
            pl.BlockSpec((seq, LANES), lambda b: (b, 0)),
            pl.BlockSpec((1, LANES), lambda b: (0, 0)),
        ],
        out_specs=pl.BlockSpec((seq, LANES), lambda b: (b, 0)),
        out_shape=jax.ShapeDtypeStruct((batch * seq, LANES), F32),
        compiler_params=_params("parallel"),
        name="fox_cumsum",
    )(fl, bias)


def _gate_lanes(c_col, lane, own_first):
    hi, mid, lo = [p.astype(F32) for p in _split3(c_col)]
    base = 0 if own_first else 3
    pieces = jnp.where(lane == base, hi, jnp.where(lane == base + 1, mid, jnp.where(lane == base + 2, lo, 0.0)))
    ones = jnp.where((lane >= 3 - base) & (lane < 6 - base), 1.0, 0.0)
    return (pieces + ones).astype(BF16)


def _fox_attn_kernel(q_ref, k_ref, v_ref, c_ref, o_ref, kaug_ref, vaug_ref, m_ref, acc_ref, *, blk):
    h = pl.program_id(1)
    i = pl.program_id(2)
    seq = k_ref.shape[0]
    nt = (((1,), (1,)), ((), ()))

    @pl.when(i == 0)
    def _():
        lane = lax.broadcasted_iota(jnp.int32, (seq, LANES), 1)
        c_key = jnp.sum(jnp.where(lane == h, c_ref[...], 0.0), axis=1, keepdims=True)
        kaug_ref[:, :HEAD_DIM] = k_ref[...]
        kaug_ref[:, HEAD_DIM:] = _gate_lanes(-c_key, lane, own_first=False)
        vaug_ref[:, :HEAD_DIM] = v_ref[...]
        vaug_ref[:, HEAD_DIM:] = jnp.ones((seq, LANES), BF16)

    lane = lax.broadcasted_iota(jnp.int32, (blk, LANES), 1)
    c_rows = c_ref[pl.ds(pl.multiple_of(i * blk, blk), blk), :]
    c_query = jnp.sum(jnp.where(lane == h, c_rows, 0.0), axis=1, keepdims=True)
    q = jnp.concatenate([q_ref[...], _gate_lanes(c_query, lane, own_first=True)], axis=1)
    m_ref[...] = jnp.full(m_ref.shape, NEG_BIG, F32)
    acc_ref[...] = jnp.zeros(acc_ref.shape, F32)

    def step(j, masked):
        start = pl.multiple_of(j * blk, blk)
        s = lax.dot_general(q, kaug_ref[pl.ds(start, blk), :], nt, preferred_element_type=F32)
        if masked:
            r = lax.broadcasted_iota(jnp.int32, (blk, blk), 0)
            c = lax.broadcasted_iota(jnp.int32, (blk, blk), 1)
            s = jnp.where(r >= c, s, -jnp.inf)
        m_prev = m_ref[...]
        m_new = jnp.maximum(m_prev, jnp.max(s, axis=-1, keepdims=True))
        alpha = jnp.exp(m_prev - m_new)
        p = jnp.exp(s - m_new).astype(BF16)
        acc_ref[...] = alpha * acc_ref[...] + jnp.dot(p, vaug_ref[pl.ds(start, blk), :],
                                                      preferred_element_type=F32)
        m_ref[...] = m_new

    def body(j, carry):
        step(j, False)
        return carry

    lax.fori_loop(0, i, body, 0)
    step(i, True)
    acc = acc_ref[...]
    o_ref[...] = (acc[:, :HEAD_DIM] * (1.0 / acc[:, HEAD_DIM:])).astype(o_ref.dtype)


def fox_attention(proj, call, batch, seq, q_col, k_col, v_col, blk):
    nq = seq // blk
    return pl.pallas_call(
        functools.partial(_fox_attn_kernel, blk=blk),
        grid=(batch, FOX_HEADS, nq),
        in_specs=[
            pl.BlockSpec((blk, HEAD_DIM), lambda b, h, i: (b * nq + i, q_col + h)),
            pl.BlockSpec((seq, HEAD_DIM), lambda b, h, i: (b, k_col + h)),
            pl.BlockSpec((seq, HEAD_DIM), lambda b, h, i: (b, v_col + h)),
            pl.BlockSpec((seq, LANES), lambda b, h, i: (b, 0)),
        ],
        out_specs=pl.BlockSpec((blk, HEAD_DIM), lambda b, h, i: (b * nq + i, h)),
        out_shape=jax.ShapeDtypeStruct((batch * seq, FOX_HEADS * HEAD_DIM), BF16),
        scratch_shapes=[
            pltpu.VMEM((seq, HEAD_DIM + LANES), BF16),
            pltpu.VMEM((seq, HEAD_DIM + LANES), BF16),
            pltpu.VMEM((blk, 1), F32),
            pltpu.VMEM((blk, HEAD_DIM + LANES), F32),
        ],
        compiler_params=_params("parallel", "parallel", "arbitrary"),
        name="fox_attention",
    )(proj, proj, proj, call)


def _retention_kernel(q_ref, k_ref, v_ref, g_ref, cos_ref, sin_ref, dm_ref, aux_ref, o_ref, r_ref, *, cs):
    seq = q_ref.shape[0]
    r_ref[...] = jnp.zeros(r_ref.shape, F32)
    dm = dm_ref[0]
    aux = aux_ref[0]
    xi = aux[:, 0:1]
    zeta = aux[:, 1:2]
    g_chunk = aux[:, 2:3]
    half = HEAD_DIM // 2

    def body(c, carry):
        rows = pl.ds(pl.multiple_of(c * cs, cs), cs)
        cos = cos_ref[rows, :]
        sin = sin_ref[rows, :]
        q = q_ref[rows, :].astype(F32)
        k = k_ref[rows, :].astype(F32)
        v = v_ref[rows, :]
        qr = q * cos + pltpu.roll(q, half, 1) * sin
        kr = k * cos + pltpu.roll(k, half, 1) * sin
        qb = qr.astype(BF16)
        s = lax.dot_general(qb, kr.astype(BF16), (((1,), (1,)), ((), ())),
                            preferred_element_type=F32) * dm
        o = jnp.dot(s.astype(BF16), v, preferred_element_type=F32)
        r = r_ref[...]
        o = o + jnp.dot(qb, r.astype(BF16), preferred_element_type=F32) * xi
        kz = (kr * zeta).astype(BF16)
        r_ref[...] = g_chunk * r + lax.dot_general(kz, v, (((0,), (0,)), ((), ())),
                                                   preferred_element_type=F32)
        mu = jnp.mean(o, axis=-1, keepdims=True)
        oc = o - mu
        var = jnp.mean(oc * oc, axis=-1, keepdims=True)
        on = oc * lax.rsqrt(var + EPS)
        g = g_ref[rows, :].astype(F32)
        o_ref[rows, :] = (g * jax.nn.sigmoid(g) * on).astype(o_ref.dtype)
        return carry

    lax.fori_loop(0, seq // cs, body, 0)


def retention(proj, cos, sin, dm, aux, batch, seq, q_col, k_col, v_col, g_col):
    def col(c0):
        return pl.BlockSpec((seq, HEAD_DIM), lambda b, h: (b, c0 + h))

    return pl.pallas_call(
        functools.partial(_retention_kernel, cs=RET_CHUNK),
        grid=(batch, RET_HEADS),
        in_specs=[
            col(q_col), col(k_col), col(v_col), col(g_col),
            pl.BlockSpec((seq, HEAD_DIM), lambda b, h: (0, 0)),
            pl.BlockSpec((seq, HEAD_DIM), lambda b, h: (0, 0)),
            pl.BlockSpec((1, RET_CHUNK, RET_CHUNK), lambda b, h: (h, 0, 0)),
            pl.BlockSpec((1, RET_CHUNK, LANES), lambda b, h: (h, 0, 0)),
        ],
        out_specs=pl.BlockSpec((seq, HEAD_DIM), lambda b, h: (b, h)),
        out_shape=jax.ShapeDtypeStruct((batch * seq, RET_HEADS * HEAD_DIM), BF16),
        scratch_shapes=[pltpu.VMEM((HEAD_DIM, HEAD_DIM), F32)],
        compiler_params=_params("parallel", "parallel"),
        name="retention",
    )(proj, proj, proj, proj, cos, sin, dm, aux)


def _retention_tables(seq):
    half = HEAD_DIM // 2
    inv = ROPE_BASE ** (-jnp.arange(half, dtype=F32) / half)
    ang = jnp.arange(seq, dtype=F32)[:, None] * inv[None, :]
    cos, sin = jnp.cos(ang), jnp.sin(ang)
    cos_t = jnp.concatenate([cos, cos], axis=-1)
    sin_t = jnp.concatenate([-sin, sin], axis=-1)
    cs = RET_CHUNK
    log_g = jnp.log1p(-jnp.exp2(-5.0 - jnp.arange(RET_HEADS, dtype=F32)))
    pos = jnp.arange(cs, dtype=F32)
    diff = pos[:, None] - pos[None, :]
    dm = jnp.where(diff >= 0, jnp.exp(log_g[:, None, None] * jnp.maximum(diff, 0.0)), 0.0)
    zeta = jnp.exp(log_g[:, None] * (cs - 1 - pos)[None, :])
    xi = jnp.exp(log_g[:, None] * (pos + 1)[None, :])
    g_chunk = jnp.broadcast_to(jnp.exp(log_g * cs)[:, None], (RET_HEADS, cs))
    aux = jnp.zeros((RET_HEADS, cs, LANES), F32)
    aux = aux.at[:, :, 0].set(xi).at[:, :, 1].set(zeta).at[:, :, 2].set(g_chunk)
    return cos_t, sin_t, dm, aux


def _s5_kernel(u_ref, toep_ref, win_ref, woutt_ref, aq_ref, d_ref, y_ref, ucat_ref, inc_ref, x_ref, *, q):
    m = u_ref.shape[0] // q
    sl = STATE_LANES
    for s in range(q):
        ucat_ref[:, s * LANES:(s + 1) * LANES] = u_ref[pl.ds(s, m, stride=q), :].astype(BF16)
    ucat = ucat_ref[...]
    inc_ref[...] = jnp.dot(ucat, win_ref[0], preferred_element_type=F32)
    a_re = aq_ref[0, 0:1, :]
    a_im = aq_ref[0, 1:2, :]

    def body(n, carry):
        x_re, x_im = carry
        x_ref[pl.ds(n, 1), 0:sl] = x_re
        x_ref[pl.ds(n, 1), sl:2 * sl] = x_im
        i_re = inc_ref[pl.ds(n, 1), 0:sl]
        i_im = inc_ref[pl.ds(n, 1), sl:2 * sl]
        return (a_re * x_re - a_im * x_im + i_re, a_re * x_im + a_im * x_re + i_im)

    zero = jnp.zeros((1, sl), F32)
    lax.fori_loop(0, m, body, (zero, zero))
    y = jnp.dot(ucat, toep_ref[0], preferred_element_type=F32)
    y = y + lax.dot_general(x_ref[...].astype(BF16), woutt_ref[0], (((1,), (1,)), ((), ())),
                            preferred_element_type=F32)
    for t in range(q):
        rows = pl.ds(t, m, stride=q)
        y_ref[rows, :] = y[:, t * LANES:(t + 1) * LANES] + d_ref[...] * u_ref[rows, :]


def s5_core(proj, toep, win, woutt, aq, d_skip, batch, seq, u_col):
    q = S5_CHUNK
    m = seq // q
    nblk = toep.shape[0]
    return pl.pallas_call(
        functools.partial(_s5_kernel, q=q),
        grid=(nblk, batch),
        in_specs=[
            pl.BlockSpec((seq, LANES), lambda j, b: (b, u_col + j)),
            pl.BlockSpec((1, q * LANES, q * LANES), lambda j, b: (j, 0, 0)),
            pl.BlockSpec((1, q * LANES, 2 * STATE_LANES), lambda j, b: (j, 0, 0)),
            pl.BlockSpec((1, q * LANES, 2 * STATE_LANES), lambda j, b: (j, 0, 0)),
            pl.BlockSpec((1, 2, STATE_LANES), lambda j, b: (j, 0, 0)),
            pl.BlockSpec((1, LANES), lambda j, b: (0, j)),
        ],
        out_specs=pl.BlockSpec((seq, LANES), lambda j, b: (b, j)),
        out_shape=jax.ShapeDtypeStruct((batch * seq, nblk * LANES), F32),
        scratch_shapes=[
            pltpu.VMEM((m, q * LANES), BF16),
            pltpu.VMEM((m, 2 * STATE_LANES), F32),
            pltpu.VMEM((m, 2 * STATE_LANES), F32),
        ],
        compiler_params=_params("parallel", "arbitrary"),
        name="s5_core",
    )(proj, toep, win, woutt, aq, d_skip)


def _dot_nt_split(x, y):
    nt = (((1,), (1,)), ((), ()))
    xh = x.astype(BF16)
    xl = (x - xh.astype(F32)).astype(BF16)
    yh = y.astype(BF16)
    yl = (y - yh.astype(F32)).astype(BF16)
    out = lax.dot_general(xh, yh, nt, preferred_element_type=F32)
    out = out + lax.dot_general(xh, yl, nt, preferred_element_type=F32)
    return out + lax.dot_general(xl, yh, nt, preferred_element_type=F32)


def _s5_tables_kernel(are_ref, aim_ref, ls_ref, bre_ref, bim_ref, cre_ref, cim_ref,
                      toep_ref, win_ref, woutt_ref, aq_ref, wf_ref, *, q):
    sl = STATE_LANES
    lam_re = jnp.minimum(are_ref[0], -1e-4)
    lam_im = aim_ref[0]
    step = jnp.exp(ls_ref[0])
    mag = jnp.exp(lam_re * step)
    a_re = mag * jnp.cos(lam_im * step)
    a_im = mag * jnp.sin(lam_im * step)
    den = lam_re * lam_re + lam_im * lam_im
    f_re = ((a_re - 1.0) * lam_re + a_im * lam_im) / den
    f_im = (a_im * lam_re - (a_re - 1.0) * lam_im) / den
    b_re = bre_ref[0]
    b_im = bim_ref[0]
    bb_re = f_re * b_re - f_im * b_im
    bb_im = f_re * b_im + f_im * b_re
    c_re = cre_ref[0]
    c_im = cim_ref[0]

    pows = [(jnp.ones((1, sl), F32), jnp.zeros((1, sl), F32))]
    for _ in range(q):
        p_re, p_im = pows[-1]
        pows.append((p_re * a_re - p_im * a_im, p_re * a_im + p_im * a_re))

    row_group = lax.shift_right_logical(lax.broadcasted_iota(jnp.int32, (LANES, sl), 0), S5_GROUP_SHIFT)
    col_group = lax.shift_right_logical(lax.broadcasted_iota(jnp.int32, (LANES, sl), 1), S5_STATE_SHIFT)
    same_group = row_group == col_group

    def tile(k, m_re, m_im, im_sign):
        p_re, p_im = pows[k]
        v_re = p_re * m_re - p_im * m_im
        v_im = (p_re * m_im + p_im * m_re) * im_sign
        e_re = jnp.where(same_group, jnp.concatenate([v_re] * GROUPS_PER_BLOCK, axis=0), 0.0)
        e_im = jnp.where(same_group, jnp.concatenate([v_im] * GROUPS_PER_BLOCK, axis=0), 0.0)
        return jnp.concatenate([e_re, e_im], axis=1)

    for s in range(q):
        rows = slice(s * LANES, (s + 1) * LANES)
        wf_ref[rows, :] = tile(q - 1 - s, bb_re, bb_im, 1.0)
        woutt_ref[0, rows, :] = tile(s + 1, c_re, c_im, -1.0).astype(BF16)
    win_ref[0] = wf_ref[...].astype(BF16)
    taps = _dot_nt_split(wf_ref[...], tile(0, c_re, c_im, -1.0))
    toep_ref[0] = jnp.zeros(toep_ref.shape[1:], BF16)
    for s in range(q):
        for t in range(s, q):
            lag_rows = slice((q - 1 - (t - s)) * LANES, (q - (t - s)) * LANES)
            toep_ref[0, s * LANES:(s + 1) * LANES, t * LANES:(t + 1) * LANES] = taps[lag_rows, :].astype(BF16)
    aq_ref[0, 0:1, :] = pows[q][0]
    aq_ref[0, 1:2, :] = pows[q][1]


def s5_tables(a_re, a_im, b_re, b_im, c_re, c_im, log_step):
    q = S5_CHUNK
    g, p = a_re.shape
    gb = GROUPS_PER_BLOCK
    nblk = g // gb
    sl = gb * p

    def rows(t):
        return t.astype(F32).reshape(nblk, 1, sl)

    def b_mat(t):
        return t.astype(F32).reshape(nblk, gb, p, S5_GROUP).transpose(0, 3, 1, 2).reshape(nblk, S5_GROUP, sl)

    def c_mat(t):
        return t.astype(F32).reshape(nblk, gb, S5_GROUP, p).transpose(0, 2, 1, 3).reshape(nblk, S5_GROUP, sl)

    row_spec = pl.BlockSpec((1, 1, sl), lambda j: (j, 0, 0))
    mat_spec = pl.BlockSpec((1, S5_GROUP, sl), lambda j: (j, 0, 0))
    return pl.pallas_call(
        functools.partial(_s5_tables_kernel, q=q),
        grid=(nblk,),
        in_specs=[row_spec, row_spec, row_spec, mat_spec, mat_spec, mat_spec, mat_spec],
        out_specs=[
            pl.BlockSpec((1, q * LANES, q * LANES), lambda j: (j, 0, 0)),
            pl.BlockSpec((1, q * LANES, 2 * sl), lambda j: (j, 0, 0)),
            pl.BlockSpec((1, q * LANES, 2 * sl), lambda j: (j, 0, 0)),
            pl.BlockSpec((1, 2, sl), lambda j: (j, 0, 0)),
        ],
        out_shape=[
            jax.ShapeDtypeStruct((nblk, q * LANES, q * LANES), BF16),
            jax.ShapeDtypeStruct((nblk, q * LANES, 2 * sl), BF16),
            jax.ShapeDtypeStruct((nblk, q * LANES, 2 * sl), BF16),
            jax.ShapeDtypeStruct((nblk, 2, sl), F32),
        ],
        scratch_shapes=[pltpu.VMEM((q * LANES, 2 * sl), F32)],
        compiler_params=_params("parallel"),
        name="s5_tables",
    )(rows(a_re), rows(a_im), rows(log_step), b_mat(b_re), b_mat(b_im), c_mat(c_re), c_mat(c_im))


def _s5_glu_kernel(y_ref, w_ref, o_ref):
    y = y_ref[...]
    g = 0.5 * y * (1.0 + jnp.tanh(math.sqrt(2.0 / math.pi) * (y + 0.044715 * (y * y * y))))
    z = jnp.dot(g.astype(BF16), w_ref[...], preferred_element_type=F32)
    o_ref[...] = (g * jax.nn.sigmoid(z)).astype(o_ref.dtype)


def s5_glu(y, w, tm):
    t, n = y.shape
    return pl.pallas_call(
        _s5_glu_kernel,
        grid=(t // tm,),
        in_specs=[
            pl.BlockSpec((tm, n), lambda i: (i, 0)),
            pl.BlockSpec((n, n), lambda i: (0, 0)),
        ],
        out_specs=pl.BlockSpec((tm, n), lambda i: (i, 0)),
        out_shape=jax.ShapeDtypeStruct((t, n), BF16),
        compiler_params=_params("parallel"),
        name="s5_glu",
    )(y, w)


def _short_conv_kernel(h_ref, gb_ref, gc_ref, w_ref, o_ref):
    z = gc_ref[...] * h_ref[...]
    row = lax.broadcasted_iota(jnp.int32, z.shape, 0)
    y = w_ref[CONV_K - 1:CONV_K, :] * z
    for lag in range(1, CONV_K):
        zl = jnp.where(row >= lag, pltpu.roll(z, lag, 0), 0.0)
        y = y + w_ref[CONV_K - 1 - lag:CONV_K - lag, :] * zl
    o_ref[...] = (gb_ref[...] * y).astype(o_ref.dtype)


def short_conv(proj, conv_w, batch, seq, h_col, gb_col, gc_col):
    nblk = conv_w.shape[1] // LANES

    def col(c0):
        return pl.BlockSpec((seq, LANES), lambda b, j: (b, c0 + j))

    return pl.pallas_call(
        _short_conv_kernel,
        grid=(batch, nblk),
        in_specs=[col(h_col), col(gb_col), col(gc_col),
                  pl.BlockSpec((CONV_K, LANES), lambda b, j: (0, j))],
        out_specs=pl.BlockSpec((seq, LANES), lambda b, j: (b, j)),
        out_shape=jax.ShapeDtypeStruct((batch * seq, nblk * LANES), BF16),
        compiler_params=_params("parallel", "parallel"),
        name="short_conv",
    )(proj, proj, proj, conv_w)


def _even_mixer(h, g, w_in, b_forget, w_out, tables, batch, seq):
    fw = FOX_HEADS * HEAD_DIM
    rw = RET_HEADS * HEAD_DIM
    scale = HEAD_DIM ** -0.5
    c0 = 3 * fw
    c1 = c0 + FOX_HEADS
    w_main = jnp.concatenate([
        w_in[:, :fw] * scale, w_in[:, fw:c0],
        w_in[:, c1:c1 + rw], w_in[:, c1 + rw:c1 + 2 * rw] * scale, w_in[:, c1 + 2 * rw:],
    ], axis=1).astype(BF16)
    w_forget = jnp.pad(w_in[:, c0:c1], ((0, 0), (0, LANES - FOX_HEADS))).astype(BF16)
    bias = jnp.pad(b_forget.astype(F32), (0, LANES - FOX_HEADS)).reshape(1, LANES)

    t = h.shape[0]
    proj = norm_matmul(h, g, w_main, BF16, tm=_row_tile(t, PROJ_ROWS), tn=PROJ_COLS)
    fl = norm_matmul(h, g, w_forget, F32, tm=_row_tile(t, PROJ_ROWS), tn=LANES)
    call = fox_cumsum(fl, bias, batch, seq)
    nh = FOX_HEADS
    fox = fox_attention(proj, call, batch, seq, 0, nh, 2 * nh, blk=_row_tile(seq, FOX_BLOCK))
    cos_t, sin_t, dm, aux = tables
    ret = retention(proj, cos_t, sin_t, dm, aux, batch, seq, 3 * nh, 4 * nh, 5 * nh, 6 * nh)
    w_o = w_out.astype(BF16)
    return mix_out(h, fox, ret, w_o[:fw], w_o[fw:], tm=_row_tile(t, MIX_ROWS), tn=w_o.shape[1])


def _odd_mixer(h, g, w_in, a_re, a_im, b_re, b_im, c_re, c_im, d_skip, log_step, w_glu, conv_w, w_out,
               batch, seq):
    sw = a_re.shape[0] * S5_GROUP
    nblk = sw // LANES
    t = h.shape[0]
    proj = norm_matmul(h, g, w_in.astype(BF16), F32, tm=_row_tile(t, PROJ_ROWS), tn=PROJ_COLS)
    toep, win, woutt, aq = s5_tables(a_re, a_im, b_re, b_im, c_re, c_im, log_step)
    y = s5_core(proj, toep, win, woutt, aq, d_skip.reshape(1, sw).astype(F32), batch, seq, 0)
    ssm = s5_glu(y, w_glu.astype(BF16), tm=_row_tile(t, MIX_ROWS))
    conv = short_conv(proj, conv_w.astype(F32), batch, seq, nblk, 2 * nblk, 3 * nblk)
    w_o = w_out.astype(BF16)
    return mix_out(h, ssm, conv, w_o[:sw], w_o[sw:], tm=_row_tile(t, MIX_ROWS), tn=w_o.shape[1])


def kernel(x, mem, norm_mix, norm_xattn, norm_mlp, norm_mem, norm_final, ab_w_in, ab_b_forget, ab_w_out,
           cd_w_in, s5_a_re, s5_a_im, s5_b_re, s5_b_im, s5_c_re, s5_c_im, s5_d, s5_log_step, s5_w_glu,
           conv_w, cd_w_out, xa_wq, xa_wkv, xa_wo, mlp_w1, mlp_w2):
    batch, seq, d = x.shape
    depth = norm_mix.shape[0]
    mlen = mem.shape[1]
    xw = xa_wq.shape[2]
    h = x.reshape(batch * seq, d)
    memf = mem.reshape(batch * mlen, d)
    tables = _retention_tables(seq)
    xa_scale = HEAD_DIM ** -0.5
    for layer in range(depth):
        if layer % 2 == 0:
            e = layer // 2
            h = _even_mixer(h, norm_mix[layer], ab_w_in[e], ab_b_forget[e], ab_w_out[e], tables, batch, seq)
        else:
            o = layer // 2
            h = _odd_mixer(h, norm_mix[layer], cd_w_in[o], s5_a_re[o], s5_a_im[o], s5_b_re[o], s5_b_im[o],
                           s5_c_re[o], s5_c_im[o], s5_d[o], s5_log_step[o], s5_w_glu[o], conv_w[o],
                           cd_w_out[o], batch, seq)
        kv = norm_matmul(memf, norm_mem, xa_wkv[layer].astype(BF16), BF16,
                         tm=_row_tile(batch * mlen, PROJ_ROWS), tn=PROJ_COLS)
        kv = kv.reshape(batch, mlen, 2 * xw)
        h = xattn_block(h, norm_xattn[layer], (xa_wq[layer] * xa_scale).astype(BF16), kv,
                        xa_wo[layer].astype(BF16), seq, tm=_row_tile(seq, MIX_ROWS))
        h = mlp_block(h, norm_mlp[layer], mlp_w1[layer].astype(BF16), mlp_w2[layer].astype(BF16),
                      norm_final, layer == depth - 1, tm=_row_tile(batch * seq, MLP_ROWS), tf=MLP_COLS)
    return h.reshape(batch, seq, d)
```

```python
import functools
import math

import jax
import jax.numpy as jnp
from jax import lax
from jax.experimental import pallas as pl
from jax.experimental.pallas import tpu as pltpu

F32 = jnp.float32
BF16 = jnp.bfloat16

EPS = 1e-6
ROPE_BASE = 10000.0
LANES = 128
HEAD_DIM = 128
FOX_HEADS = 8
RET_HEADS = 8
RET_CHUNK = 128
XA_HEADS = 4
S5_GROUP = 16
S5_STATE = 64
S5_CHUNK = 16
CONV_K = 3
S5_GROUP_SHIFT = S5_GROUP.bit_length() - 1
S5_STATE_SHIFT = S5_STATE.bit_length() - 1
GROUPS_PER_BLOCK = LANES // S5_GROUP
STATE_LANES = GROUPS_PER_BLOCK * S5_STATE
VMEM_LIMIT = 56 * 1024 * 1024
NEG_BIG = -1e30

PROJ_ROWS, PROJ_COLS = 1024, 512
MIX_ROWS = 512
MLP_ROWS, MLP_COLS = 1024, 512
FOX_BLOCK = 512
FOX_HEADS_PER_STEP = 4
RET_ROWS = 1024
RET_HEADS_PER_STEP = 4


def _row_tile(rows, want):
    tile = min(rows, want)
    while rows % tile:
        tile //= 2
    return tile


def _params(*sem):
    return pltpu.CompilerParams(dimension_semantics=sem, vmem_limit_bytes=VMEM_LIMIT)


def _rms(x, g):
    ms = jnp.mean(x * x, axis=-1, keepdims=True)
    return x * lax.rsqrt(ms + EPS) * g


def _norm_matmul_kernel(x_ref, g_ref, w_ref, o_ref, xn_ref):
    @pl.when(pl.program_id(1) == 0)
    def _():
        xn_ref[...] = _rms(x_ref[...], g_ref[...]).astype(BF16)

    o_ref[...] = jnp.dot(xn_ref[...], w_ref[...], preferred_element_type=F32).astype(o_ref.dtype)


def norm_matmul(x, g, w, out_dtype, tm, tn):
    t, d = x.shape
    n = w.shape[1]
    return pl.pallas_call(
        _norm_matmul_kernel,
        grid=(t // tm, n // tn),
        in_specs=[
            pl.BlockSpec((tm, d), lambda i, j: (i, 0)),
            pl.BlockSpec((1, d), lambda i, j: (0, 0)),
            pl.BlockSpec((d, tn), lambda i, j: (0, j)),
        ],
        out_specs=pl.BlockSpec((tm, tn), lambda i, j: (i, j)),
        out_shape=jax.ShapeDtypeStruct((t, n), out_dtype),
        scratch_shapes=[pltpu.VMEM((tm, d), BF16)],
        compiler_params=_params("parallel", "arbitrary"),
        name="norm_matmul",
    )(x, g.reshape(1, d), w)


def _mix_out_kernel(h_ref, a1_ref, a2_ref, w1_ref, w2_ref, o_ref):
    acc = jnp.dot(a1_ref[...], w1_ref[...], preferred_element_type=F32)
    acc = acc + jnp.dot(a2_ref[...], w2_ref[...], preferred_element_type=F32)
    o_ref[...] = h_ref[...] + acc


def mix_out(h, a1, a2, w1, w2, tm, tn):
    t, d = h.shape
    k1 = a1.shape[1]
    k2 = a2.shape[1]
    return pl.pallas_call(
        _mix_out_kernel,
        grid=(t // tm, d // tn),
        in_specs=[
            pl.BlockSpec((tm, tn), lambda i, j: (i, j)),
            pl.BlockSpec((tm, k1), lambda i, j: (i, 0)),
            pl.BlockSpec((tm, k2), lambda i, j: (i, 0)),
            pl.BlockSpec((k1, tn), lambda i, j: (0, j)),
            pl.BlockSpec((k2, tn), lambda i, j: (0, j)),
        ],
        out_specs=pl.BlockSpec((tm, tn), lambda i, j: (i, j)),
        out_shape=jax.ShapeDtypeStruct((t, d), F32),
        compiler_params=_params("parallel", "arbitrary"),
        name="mix_out",
    )(h, a1, a2, w1, w2)


def _mlp_kernel(h_ref, g_ref, w1_ref, w2_ref, gf_ref, o_ref, xn_ref, *, final_norm):
    f = pl.program_id(1)

    @pl.when(f == 0)
    def _():
        x = h_ref[...]
        xn_ref[...] = _rms(x, g_ref[...]).astype(BF16)
        o_ref[...] = x

    a = jnp.dot(xn_ref[...], w1_ref[...], preferred_element_type=F32)
    a = jnp.square(jnp.maximum(a, 0.0)).astype(BF16)
    o_ref[...] += jnp.dot(a, w2_ref[...], preferred_element_type=F32)

    if final_norm:
        @pl.when(f == pl.num_programs(1) - 1)
        def _():
            o_ref[...] = _rms(o_ref[...], gf_ref[...])


def mlp_block(h, g, w1, w2, g_final, final_norm, tm, tf):
    t, d = h.shape
    dff = w1.shape[1]
    return pl.pallas_call(
        functools.partial(_mlp_kernel, final_norm=final_norm),
        grid=(t // tm, dff // tf),
        in_specs=[
            pl.BlockSpec((tm, d), lambda i, f: (i, 0)),
            pl.BlockSpec((1, d), lambda i, f: (0, 0)),
            pl.BlockSpec((d, tf), lambda i, f: (0, f)),
            pl.BlockSpec((tf, d), lambda i, f: (f, 0)),
            pl.BlockSpec((1, d), lambda i, f: (0, 0)),
        ],
        out_specs=pl.BlockSpec((tm, d), lambda i, f: (i, 0)),
        out_shape=jax.ShapeDtypeStruct((t, d), F32),
        scratch_shapes=[pltpu.VMEM((tm, d), BF16)],
        compiler_params=_params("parallel", "arbitrary"),
        name="mlp_block",
    )(h, g.reshape(1, d), w1, w2, g_final.reshape(1, d))


def _xattn_kernel(h_ref, g_ref, wq_ref, k_ref, v_ref, wo_ref, o_ref):
    x = h_ref[...]
    xn = _rms(x, g_ref[...]).astype(BF16)
    q = jnp.dot(xn, wq_ref[...], preferred_element_type=F32).astype(BF16)
    heads = []
    for hd in range(XA_HEADS):
        sl = slice(hd * HEAD_DIM, (hd + 1) * HEAD_DIM)
        s = lax.dot_general(q[:, sl], k_ref[0, :, sl], (((1,), (1,)), ((), ())),
                            preferred_element_type=F32)
        m = jnp.max(s, axis=-1, keepdims=True)
        p = jnp.exp(s - m)
        l = jnp.sum(p, axis=-1, keepdims=True)
        oh = jnp.dot(p.astype(BF16), v_ref[0, :, sl], preferred_element_type=F32)
        heads.append((oh * (1.0 / l)).astype(BF16))
    o = jnp.concatenate(heads, axis=-1)
    o_ref[...] = x + jnp.dot(o, wo_ref[...], preferred_element_type=F32)


def xattn_block(h, g, wq, kv, wo, seq, tm):
    t, d = h.shape
    mlen = kv.shape[1]
    xw = wq.shape[1]
    per_batch = seq // tm
    return pl.pallas_call(
        _xattn_kernel,
        grid=(t // tm,),
        in_specs=[
            pl.BlockSpec((tm, d), lambda i: (i, 0)),
            pl.BlockSpec((1, d), lambda i: (0, 0)),
            pl.BlockSpec((d, xw), lambda i: (0, 0)),
            pl.BlockSpec((1, mlen, xw), lambda i: (i // per_batch, 0, 0)),
            pl.BlockSpec((1, mlen, xw), lambda i: (i // per_batch, 0, 1)),
            pl.BlockSpec((xw, d), lambda i: (0, 0)),
        ],
        out_specs=pl.BlockSpec((tm, d), lambda i: (i, 0)),
        out_shape=jax.ShapeDtypeStruct((t, d), F32),
        compiler_params=_params("parallel"),
        name="xattn_block",
    )(h, g.reshape(1, d), wq, kv, kv, wo)


def _split3(x):
    hi = x.astype(BF16)
    r1 = x - hi.astype(F32)
    mid = r1.astype(BF16)
    lo = (r1 - mid.astype(F32)).astype(BF16)
    return hi, mid, lo


def _fox_cumsum_kernel(fl_ref, b_ref, call_ref, *, blk):
    seq = fl_ref.shape[0]
    row = lax.broadcasted_iota(jnp.int32, (blk, blk), 0)
    col = lax.broadcasted_iota(jnp.int32, (blk, blk), 1)
    tri = jnp.where(row >= col, 1.0, 0.0).astype(BF16)
    carry = jnp.zeros((1, LANES), F32)
    for i in range(seq // blk):
        rows = slice(i * blk, (i + 1) * blk)
        x = fl_ref[rows, :] + b_ref[...]
        ls = jnp.minimum(x, 0.0) - jnp.log(1.0 + jnp.exp(-jnp.abs(x)))
        c = carry
        for piece in _split3(ls):
            c = c + jnp.dot(tri, piece, preferred_element_type=F32)
        call_ref[rows, :] = c
        carry = c[blk - 1:blk, :]


def fox_cumsum(fl, bias, batch, seq):
    blk = min(256, seq)
    return pl.pallas_call(
        functools.partial(_fox_cumsum_kernel, blk=blk),
        grid=(batch,),
        in_specs=[
            pl.BlockSpec((seq, LANES), lambda b: (b, 0)),
            pl.BlockSpec((1, LANES), lambda b: (0, 0)),
        ],
        out_specs=pl.BlockSpec((seq, LANES), lambda b: (b, 0)),
        out_shape=jax.ShapeDtypeStruct((batch * seq, LANES), F32),
        compiler_params=_params("parallel"),
        name="fox_cumsum",
    )(fl, bias)


def _gate_lanes(c_col, lane, own_first):
    hi, mid, lo = [p.astype(F32) for p in _split3(c_col)]
    base = 0 if own_first else 3
    pieces = jnp.where(lane == base, hi, jnp.where(lane == base + 1, mid, jnp.where(lane == base + 2, lo, 0.0)))
    ones = jnp.where((lane >= 3 - base) & (lane < 6 - base), 1.0, 0.0)
    return (pieces + ones).astype(BF16)


def _fox_attn_kernel(q_ref, k_ref, v_ref, c_ref, o_ref, qaug_ref, kaug_ref, vaug_ref, m_ref, acc_ref,
                     *, blk, hp):
    g = pl.program_id(1)
    i = pl.program_id(2)
    seq = k_ref.shape[0]
    nt = (((1,), (1,)), ((), ()))

    def head_cols(t):
        return slice(t * HEAD_DIM, (t + 1) * HEAD_DIM)

    @pl.when(i == 0)
    def _():
        lane = lax.broadcasted_iota(jnp.int32, (seq, LANES), 1)
        for t in range(hp):
            c_key = jnp.sum(jnp.where(lane == g * hp + t, c_ref[...], 0.0), axis=1, keepdims=True)
            kaug_ref[t, :, :HEAD_DIM] = k_ref[:, head_cols(t)]
            kaug_ref[t, :, HEAD_DIM:] = _gate_lanes(-c_key, lane, own_first=False)
            vaug_ref[t, :, :HEAD_DIM] = v_ref[:, head_cols(t)]
            vaug_ref[t, :, HEAD_DIM:] = jnp.ones((seq, LANES), BF16)

    lane = lax.broadcasted_iota(jnp.int32, (blk, LANES), 1)
    c_rows = c_ref[pl.ds(pl.multiple_of(i * blk, blk), blk), :]
    for t in range(hp):
        c_query = jnp.sum(jnp.where(lane == g * hp + t, c_rows, 0.0), axis=1, keepdims=True)
        qaug_ref[t, :, :HEAD_DIM] = q_ref[:, head_cols(t)]
        qaug_ref[t, :, HEAD_DIM:] = _gate_lanes(c_query, lane, own_first=True)
    m_ref[...] = jnp.full(m_ref.shape, NEG_BIG, F32)
    acc_ref[...] = jnp.zeros(acc_ref.shape, F32)

    def step(j, masked):
        start = pl.multiple_of(j * blk, blk)
        scores = [lax.dot_general(qaug_ref[t], kaug_ref[t, pl.ds(start, blk), :], nt,
                                  preferred_element_type=F32) for t in range(hp)]
        for t in range(hp):
            s = scores[t]
            if masked:
                r = lax.broadcasted_iota(jnp.int32, (blk, blk), 0)
                c = lax.broadcasted_iota(jnp.int32, (blk, blk), 1)
                s = jnp.where(r >= c, s, -jnp.inf)
            m_prev = m_ref[t]
            m_new = jnp.maximum(m_prev, jnp.max(s, axis=-1, keepdims=True))
            alpha = jnp.exp(m_prev - m_new)
            p = jnp.exp(s - pltpu.repeat(m_new, blk // LANES, axis=1)).astype(BF16)
            acc_ref[t] = (pltpu.repeat(alpha, acc_ref.shape[2] // LANES, axis=1) * acc_ref[t]
                          + jnp.dot(p, vaug_ref[t, pl.ds(start, blk), :], preferred_element_type=F32))
            m_ref[t] = m_new

    def body(j, carry):
        step(j, False)
        return carry

    lax.fori_loop(0, i, body, 0)
    step(i, True)
    for t in range(hp):
        acc = acc_ref[t]
        o_ref[:, head_cols(t)] = (acc[:, :HEAD_DIM] * (1.0 / acc[:, HEAD_DIM:])).astype(o_ref.dtype)


def fox_attention(proj, call, batch, seq, q_col, k_col, v_col, blk, hp):
    nq = seq // blk
    width = hp * HEAD_DIM
    aug = HEAD_DIM + LANES
    return pl.pallas_call(
        functools.partial(_fox_attn_kernel, blk=blk, hp=hp),
        grid=(batch, FOX_HEADS // hp, nq),
        in_specs=[
            pl.BlockSpec((blk, width), lambda b, g, i: (b * nq + i, q_col // hp + g)),
            pl.BlockSpec((seq, width), lambda b, g, i: (b, k_col // hp + g)),
            pl.BlockSpec((seq, width), lambda b, g, i: (b, v_col // hp + g)),
            pl.BlockSpec((seq, LANES), lambda b, g, i: (b, 0)),
        ],
        out_specs=pl.BlockSpec((blk, width), lambda b, g, i: (b * nq + i, g)),
        out_shape=jax.ShapeDtypeStruct((batch * seq, FOX_HEADS * HEAD_DIM), BF16),
        scratch_shapes=[
            pltpu.VMEM((hp, blk, aug), BF16),
            pltpu.VMEM((hp, seq, aug), BF16),
            pltpu.VMEM((hp, seq, aug), BF16),
            pltpu.VMEM((hp, blk, LANES), F32),
            pltpu.VMEM((hp, blk, aug), F32),
        ],
        compiler_params=_params("parallel", "parallel", "arbitrary"),
        name="fox_attention",
    )(proj, proj, proj, call)


def _retention_kernel(q_ref, k_ref, v_ref, g_ref, cos_ref, sin_ref, dm_ref, aux_ref, o_ref, r_ref,
                      *, cs, hp):
    rows_per_step = q_ref.shape[0]
    half = HEAD_DIM // 2
    nt = (((1,), (1,)), ((), ()))

    @pl.when(pl.program_id(2) == 0)
    def _():
        r_ref[...] = jnp.zeros(r_ref.shape, F32)

    def body(c, carry):
        rows = pl.ds(pl.multiple_of(c * cs, cs), cs)
        cos = cos_ref[rows, :]
        sin = sin_ref[rows, :]
        for t in range(hp):
            cols = slice(t * HEAD_DIM, (t + 1) * HEAD_DIM)
            q = q_ref[rows, cols].astype(F32)
            k = k_ref[rows, cols].astype(F32)
            v = v_ref[rows, cols]
            qr = q * cos + pltpu.roll(q, half, 1) * sin
            kr = k * cos + pltpu.roll(k, half, 1) * sin
            qb = qr.astype(BF16)
            s = lax.dot_general(qb, kr.astype(BF16), nt, preferred_element_type=F32) * dm_ref[t]
            o = jnp.dot(s.astype(BF16), v, preferred_element_type=F32)
            r = r_ref[t]
            o = o + jnp.dot(qb, r.astype(BF16), preferred_element_type=F32) * aux_ref[t, 0]
            kz = (kr * aux_ref[t, 1]).astype(BF16)
            r_ref[t] = aux_ref[t, 2] * r + lax.dot_general(kz, v, (((0,), (0,)), ((), ())),
                                                          preferred_element_type=F32)
            mu = jnp.mean(o, axis=-1, keepdims=True)
            oc = o - mu
            var = jnp.mean(oc * oc, axis=-1, keepdims=True)
            on = oc * lax.rsqrt(var + EPS)
            g = g_ref[rows, cols].astype(F32)
            o_ref[rows, cols] = (g * jax.nn.sigmoid(g) * on).astype(o_ref.dtype)
        return carry

    lax.fori_loop(0, rows_per_step // cs, body, 0)


def retention(proj, cos, sin, dm, aux, batch, seq, q_col, k_col, v_col, g_col, rows, hp):
    width = hp * HEAD_DIM
    nr = seq // rows

    def col(c0):
        return pl.BlockSpec((rows, width), lambda b, g, i: (b * nr + i, c0 // hp + g))

    return pl.pallas_call(
        functools.partial(_retention_kernel, cs=RET_CHUNK, hp=hp),
        grid=(batch, RET_HEADS // hp, nr),
        in_specs=[
            col(q_col), col(k_col), col(v_col), col(g_col),
            pl.BlockSpec((rows, HEAD_DIM), lambda b, g, i: (i, 0)),
            pl.BlockSpec((rows, HEAD_DIM), lambda b, g, i: (i, 0)),
            pl.BlockSpec((hp, RET_CHUNK, RET_CHUNK), lambda b, g, i: (g, 0, 0)),
            pl.BlockSpec((hp, 3, RET_CHUNK, LANES), lambda b, g, i: (g, 0, 0, 0)),
        ],
        out_specs=pl.BlockSpec((rows, width), lambda b, g, i: (b * nr + i, g)),
        out_shape=jax.ShapeDtypeStruct((batch * seq, RET_HEADS * HEAD_DIM), BF16),
        scratch_shapes=[pltpu.VMEM((hp, HEAD_DIM, HEAD_DIM), F32)],
        compiler_params=_params("parallel", "parallel", "arbitrary"),
        name="retention",
    )(proj, proj, proj, proj, cos, sin, dm, aux)


def _retention_tables(seq):
    half = HEAD_DIM // 2
    inv = ROPE_BASE ** (-jnp.arange(half, dtype=F32) / half)
    ang = jnp.arange(seq, dtype=F32)[:, None] * inv[None, :]
    cos, sin = jnp.cos(ang), jnp.sin(ang)
    cos_t = jnp.concatenate([cos, cos], axis=-1)
    sin_t = jnp.concatenate([-sin, sin], axis=-1)
    cs = RET_CHUNK
    log_g = jnp.log1p(-jnp.exp2(-5.0 - jnp.arange(RET_HEADS, dtype=F32)))
    pos = jnp.arange(cs, dtype=F32)
    diff = pos[:, None] - pos[None, :]
    dm = jnp.where(diff >= 0, jnp.exp(log_g[:, None, None] * jnp.maximum(diff, 0.0)), 0.0)
    zeta = jnp.exp(log_g[:, None] * (cs - 1 - pos)[None, :])
    xi = jnp.exp(log_g[:, None] * (pos + 1)[None, :])
    g_chunk = jnp.broadcast_to(jnp.exp(log_g * cs)[:, None], (RET_HEADS, cs))
    aux = jnp.broadcast_to(jnp.stack([xi, zeta, g_chunk], axis=1)[..., None], (RET_HEADS, 3, cs, LANES))
    return cos_t, sin_t, dm, aux


def _s5_kernel(u_ref, toep_ref, win_ref, woutt_ref, aq_ref, d_ref, y_ref, ucat_ref, inc_ref, x_ref, *, q):
    m = u_ref.shape[0] // q
    sl = STATE_LANES
    for s in range(q):
        ucat_ref[:, s * LANES:(s + 1) * LANES] = u_ref[pl.ds(s, m, stride=q), :].astype(BF16)
    ucat = ucat_ref[...]
    inc_ref[...] = jnp.dot(ucat, win_ref[0], preferred_element_type=F32)
    a_re = aq_ref[0, 0:1, :]
    a_im = aq_ref[0, 1:2, :]

    def body(n, carry):
        x_re, x_im = carry
        x_ref[pl.ds(n, 1), 0:sl] = x_re
        x_ref[pl.ds(n, 1), sl:2 * sl] = x_im
        i_re = inc_ref[pl.ds(n, 1), 0:sl]
        i_im = inc_ref[pl.ds(n, 1), sl:2 * sl]
        return (a_re * x_re - a_im * x_im + i_re, a_re * x_im + a_im * x_re + i_im)

    zero = jnp.zeros((1, sl), F32)
    lax.fori_loop(0, m, body, (zero, zero))
    y = jnp.dot(ucat, toep_ref[0], preferred_element_type=F32)
    y = y + lax.dot_general(x_ref[...].astype(BF16), woutt_ref[0], (((1,), (1,)), ((), ())),
                            preferred_element_type=F32)
    for t in range(q):
        rows = pl.ds(t, m, stride=q)
        y_ref[rows, :] = y[:, t * LANES:(t + 1) * LANES] + d_ref[...] * u_ref[rows, :]


def s5_core(proj, toep, win, woutt, aq, d_skip, batch, seq, u_col):
    q = S5_CHUNK
    m = seq // q
    nblk = toep.shape[0]
    return pl.pallas_call(
        functools.partial(_s5_kernel, q=q),
        grid=(nblk, batch),
        in_specs=[
            pl.BlockSpec((seq, LANES), lambda j, b: (b, u_col + j)),
            pl.BlockSpec((1, q * LANES, q * LANES), lambda j, b: (j, 0, 0)),
            pl.BlockSpec((1, q * LANES, 2 * STATE_LANES), lambda j, b: (j, 0, 0)),
            pl.BlockSpec((1, q * LANES, 2 * STATE_LANES), lambda j, b: (j, 0, 0)),
            pl.BlockSpec((1, 2, STATE_LANES), lambda j, b: (j, 0, 0)),
            pl.BlockSpec((1, LANES), lambda j, b: (0, j)),
        ],
        out_specs=pl.BlockSpec((seq, LANES), lambda j, b: (b, j)),
        out_shape=jax.ShapeDtypeStruct((batch * seq, nblk * LANES), F32),
        scratch_shapes=[
            pltpu.VMEM((m, q * LANES), BF16),
            pltpu.VMEM((m, 2 * STATE_LANES), F32),
            pltpu.VMEM((m, 2 * STATE_LANES), F32),
        ],
        compiler_params=_params("parallel", "arbitrary"),
        name="s5_core",
    )(proj, toep, win, woutt, aq, d_skip)


def _dot_nt_split(x, y):
    nt = (((1,), (1,)), ((), ()))
    xh = x.astype(BF16)
    xl = (x - xh.astype(F32)).astype(BF16)
    yh = y.astype(BF16)
    yl = (y - yh.astype(F32)).astype(BF16)
    out = lax.dot_general(xh, yh, nt, preferred_element_type=F32)
    out = out + lax.dot_general(xh, yl, nt, preferred_element_type=F32)
    return out + lax.dot_general(xl, yh, nt, preferred_element_type=F32)


def _s5_tables_kernel(are_ref, aim_ref, ls_ref, bre_ref, bim_ref, cre_ref, cim_ref,
                      toep_ref, win_ref, woutt_ref, aq_ref, wf_ref, *, q):
    sl = STATE_LANES
    lam_re = jnp.minimum(are_ref[0], -1e-4)
    lam_im = aim_ref[0]
    step = jnp.exp(ls_ref[0])
    mag = jnp.exp(lam_re * step)
    a_re = mag * jnp.cos(lam_im * step)
    a_im = mag * jnp.sin(lam_im * step)
    den = lam_re * lam_re + lam_im * lam_im
    f_re = ((a_re - 1.0) * lam_re + a_im * lam_im) / den
    f_im = (a_im * lam_re - (a_re - 1.0) * lam_im) / den
    b_re = bre_ref[0]
    b_im = bim_ref[0]
    bb_re = f_re * b_re - f_im * b_im
    bb_im = f_re * b_im + f_im * b_re
    c_re = cre_ref[0]
    c_im = cim_ref[0]

    pows = [(jnp.ones((1, sl), F32), jnp.zeros((1, sl), F32))]
    for _ in range(q):
        p_re, p_im = pows[-1]
        pows.append((p_re * a_re - p_im * a_im, p_re * a_im + p_im * a_re))

    row_group = lax.shift_right_logical(lax.broadcasted_iota(jnp.int32, (LANES, sl), 0), S5_GROUP_SHIFT)
    col_group = lax.shift_right_logical(lax.broadcasted_iota(jnp.int32, (LANES, sl), 1), S5_STATE_SHIFT)
    same_group = row_group == col_group

    def tile(k, m_re, m_im, im_sign):
        p_re, p_im = pows[k]
        v_re = p_re * m_re - p_im * m_im
        v_im = (p_re * m_im + p_im * m_re) * im_sign
        e_re = jnp.where(same_group, jnp.concatenate([v_re] * GROUPS_PER_BLOCK, axis=0), 0.0)
        e_im = jnp.where(same_group, jnp.concatenate([v_im] * GROUPS_PER_BLOCK, axis=0), 0.0)
        return jnp.concatenate([e_re, e_im], axis=1)

    for s in range(q):
        rows = slice(s * LANES, (s + 1) * LANES)
        wf_ref[rows, :] = tile(q - 1 - s, bb_re, bb_im, 1.0)
        woutt_ref[0, rows, :] = tile(s + 1, c_re, c_im, -1.0).astype(BF16)
    win_ref[0] = wf_ref[...].astype(BF16)
    taps = _dot_nt_split(wf_ref[...], tile(0, c_re, c_im, -1.0))
    toep_ref[0] = jnp.zeros(toep_ref.shape[1:], BF16)
    for s in range(q):
        for t in range(s, q):
            lag_rows = slice((q - 1 - (t - s)) * LANES, (q - (t - s)) * LANES)
            toep_ref[0, s * LANES:(s + 1) * LANES, t * LANES:(t + 1) * LANES] = taps[lag_rows, :].astype(BF16)
    aq_ref[0, 0:1, :] = pows[q][0]
    aq_ref[0, 1:2, :] = pows[q][1]


def s5_tables(a_re, a_im, b_re, b_im, c_re, c_im, log_step):
    q = S5_CHUNK
    g, p = a_re.shape
    gb = GROUPS_PER_BLOCK
    nblk = g // gb
    sl = gb * p

    def rows(t):
        return t.astype(F32).reshape(nblk, 1, sl)

    def b_mat(t):
        return t.astype(F32).reshape(nblk, gb, p, S5_GROUP).transpose(0, 3, 1, 2).reshape(nblk, S5_GROUP, sl)

    def c_mat(t):
        return t.astype(F32).reshape(nblk, gb, S5_GROUP, p).transpose(0, 2, 1, 3).reshape(nblk, S5_GROUP, sl)

    row_spec = pl.BlockSpec((1, 1, sl), lambda j: (j, 0, 0))
    mat_spec = pl.BlockSpec((1, S5_GROUP, sl), lambda j: (j, 0, 0))
    return pl.pallas_call(
        functools.partial(_s5_tables_kernel, q=q),
        grid=(nblk,),
        in_specs=[row_spec, row_spec, row_spec, mat_spec, mat_spec, mat_spec, mat_spec],
        out_specs=[
            pl.BlockSpec((1, q * LANES, q * LANES), lambda j: (j, 0, 0)),
            pl.BlockSpec((1, q * LANES, 2 * sl), lambda j: (j, 0, 0)),
            pl.BlockSpec((1, q * LANES, 2 * sl), lambda j: (j, 0, 0)),
            pl.BlockSpec((1, 2, sl), lambda j: (j, 0, 0)),
        ],
        out_shape=[
            jax.ShapeDtypeStruct((nblk, q * LANES, q * LANES), BF16),
            jax.ShapeDtypeStruct((nblk, q * LANES, 2 * sl), BF16),
            jax.ShapeDtypeStruct((nblk, q * LANES, 2 * sl), BF16),
            jax.ShapeDtypeStruct((nblk, 2, sl), F32),
        ],
        scratch_shapes=[pltpu.VMEM((q * LANES, 2 * sl), F32)],
        compiler_params=_params("parallel"),
        name="s5_tables",
    )(rows(a_re), rows(a_im), rows(log_step), b_mat(b_re), b_mat(b_im), c_mat(c_re), c_mat(c_im))


def _s5_glu_kernel(y_ref, w_ref, o_ref):
    y = y_ref[...]
    g = 0.5 * y * (1.0 + jnp.tanh(math.sqrt(2.0 / math.pi) * (y + 0.044715 * (y * y * y))))
    z = jnp.dot(g.astype(BF16), w_ref[...], preferred_element_type=F32)
    o_ref[...] = (g * jax.nn.sigmoid(z)).astype(o_ref.dtype)


def s5_glu(y, w, tm):
    t, n = y.shape
    return pl.pallas_call(
        _s5_glu_kernel,
        grid=(t // tm,),
        in_specs=[
            pl.BlockSpec((tm, n), lambda i: (i, 0)),
            pl.BlockSpec((n, n), lambda i: (0, 0)),
        ],
        out_specs=pl.BlockSpec((tm, n), lambda i: (i, 0)),
        out_shape=jax.ShapeDtypeStruct((t, n), BF16),
        compiler_params=_params("parallel"),
        name="s5_glu",
    )(y, w)


def _short_conv_kernel(h_ref, gb_ref, gc_ref, w_ref, o_ref):
    z = gc_ref[...] * h_ref[...]
    row = lax.broadcasted_iota(jnp.int32, z.shape, 0)
    y = w_ref[CONV_K - 1:CONV_K, :] * z
    for lag in range(1, CONV_K):
        zl = jnp.where(row >= lag, pltpu.roll(z, lag, 0), 0.0)
        y = y + w_ref[CONV_K - 1 - lag:CONV_K - lag, :] * zl
    o_ref[...] = (gb_ref[...] * y).astype(o_ref.dtype)


def short_conv(proj, conv_w, batch, seq, h_col, gb_col, gc_col):
    nblk = conv_w.shape[1] // LANES

    def col(c0):
        return pl.BlockSpec((seq, LANES), lambda b, j: (b, c0 + j))

    return pl.pallas_call(
        _short_conv_kernel,
        grid=(batch, nblk),
        in_specs=[col(h_col), col(gb_col), col(gc_col),
                  pl.BlockSpec((CONV_K, LANES), lambda b, j: (0, j))],
        out_specs=pl.BlockSpec((seq, LANES), lambda b, j: (b, j)),
        out_shape=jax.ShapeDtypeStruct((batch * seq, nblk * LANES), BF16),
        compiler_params=_params("parallel", "parallel"),
        name="short_conv",
    )(proj, proj, proj, conv_w)


def _even_mixer(h, g, w_in, b_forget, w_out, tables, batch, seq):
    fw = FOX_HEADS * HEAD_DIM
    rw = RET_HEADS * HEAD_DIM
    scale = HEAD_DIM ** -0.5
    c0 = 3 * fw
    c1 = c0 + FOX_HEADS
    w_main = jnp.concatenate([
        w_in[:, :fw] * scale, w_in[:, fw:c0],
        w_in[:, c1:c1 + rw], w_in[:, c1 + rw:c1 + 2 * rw] * scale, w_in[:, c1 + 2 * rw:],
    ], axis=1).astype(BF16)
    w_forget = jnp.pad(w_in[:, c0:c1], ((0, 0), (0, LANES - FOX_HEADS))).astype(BF16)
    bias = jnp.pad(b_forget.astype(F32), (0, LANES - FOX_HEADS)).reshape(1, LANES)

    t = h.shape[0]
    proj = norm_matmul(h, g, w_main, BF16, tm=_row_tile(t, PROJ_ROWS), tn=PROJ_COLS)
    fl = norm_matmul(h, g, w_forget, F32, tm=_row_tile(t, PROJ_ROWS), tn=LANES)
    call = fox_cumsum(fl, bias, batch, seq)
    nh = FOX_HEADS
    fox = fox_attention(proj, call, batch, seq, 0, nh, 2 * nh, blk=_row_tile(seq, FOX_BLOCK),
                        hp=FOX_HEADS_PER_STEP)
    cos_t, sin_t, dm, aux = tables
    ret = retention(proj, cos_t, sin_t, dm, aux, batch, seq, 3 * nh, 4 * nh, 5 * nh, 6 * nh,
                    rows=_row_tile(seq, RET_ROWS), hp=RET_HEADS_PER_STEP)
    w_o = w_out.astype(BF16)
    return mix_out(h, fox, ret, w_o[:fw], w_o[fw:], tm=_row_tile(t, MIX_ROWS), tn=w_o.shape[1])


def _odd_mixer(h, g, w_in, a_re, a_im, b_re, b_im, c_re, c_im, d_skip, log_step, w_glu, conv_w, w_out,
               batch, seq):
    sw = a_re.shape[0] * S5_GROUP
    nblk = sw // LANES
    t = h.shape[0]
    proj = norm_matmul(h, g, w_in.astype(BF16), F32, tm=_row_tile(t, PROJ_ROWS), tn=PROJ_COLS)
    toep, win, woutt, aq = s5_tables(a_re, a_im, b_re, b_im, c_re, c_im, log_step)
    y = s5_core(proj, toep, win, woutt, aq, d_skip.reshape(1, sw).astype(F32), batch, seq, 0)
    ssm = s5_glu(y, w_glu.astype(BF16), tm=_row_tile(t, MIX_ROWS))
    conv = short_conv(proj, conv_w.astype(F32), batch, seq, nblk, 2 * nblk, 3 * nblk)
    w_o = w_out.astype(BF16)
    return mix_out(h, ssm, conv, w_o[:sw], w_o[sw:], tm=_row_tile(t, MIX_ROWS), tn=w_o.shape[1])


def kernel(x, mem, norm_mix, norm_xattn, norm_mlp, norm_mem, norm_final, ab_w_in, ab_b_forget, ab_w_out,
           cd_w_in, s5_a_re, s5_a_im, s5_b_re, s5_b_im, s5_c_re, s5_c_im, s5_d, s5_log_step, s5_w_glu,
           conv_w, cd_w_out, xa_wq, xa_wkv, xa_wo, mlp_w1, mlp_w2):
    batch, seq, d = x.shape
    depth = norm_mix.shape[0]
    mlen = mem.shape[1]
    xw = xa_wq.shape[2]
    h = x.reshape(batch * seq, d)
    memf = mem.reshape(batch * mlen, d)
    tables = _retention_tables(seq)
    xa_scale = HEAD_DIM ** -0.5
    for layer in range(depth):
        if layer % 2 == 0:
            e = layer // 2
            h = _even_mixer(h, norm_mix[layer], ab_w_in[e], ab_b_forget[e], ab_w_out[e], tables, batch, seq)
        else:
            o = layer // 2
            h = _odd_mixer(h, norm_mix[layer], cd_w_in[o], s5_a_re[o], s5_a_im[o], s5_b_re[o], s5_b_im[o],
                           s5_c_re[o], s5_c_im[o], s5_d[o], s5_log_step[o], s5_w_glu[o], conv_w[o],
                           cd_w_out[o], batch, seq)
        kv = norm_matmul(memf, norm_mem, xa_wkv[layer].astype(BF16), BF16,
                         tm=_row_tile(batch * mlen, PROJ_ROWS), tn=PROJ_COLS)
        kv = kv.reshape(batch, mlen, 2 * xw)
        h = xattn_block(h, norm_xattn[layer], (xa_wq[layer] * xa_scale).astype(BF16), kv,
                        xa_wo[layer].astype(BF16), seq, tm=_row_tile(seq, MIX_ROWS))
        h = mlp_block(h, norm_mlp[layer], mlp_w1[layer].astype(BF16), mlp_w2[layer].astype(BF16),
                      norm_final, layer == depth - 1, tm=_row_tile(batch * seq, MLP_ROWS), tf=MLP_COLS)
    return h.reshape(batch, seq, d)
```

```python
import functools
import math

import jax
import jax.numpy as jnp
from jax import lax
from jax.experimental import pallas as pl
from jax.experimental.pallas import tpu as pltpu

F32 = jnp.float32
BF16 = jnp.bfloat16

EPS = 1e-6
ROPE_BASE = 10000.0
LANES = 128
HEAD_DIM = 128
FOX_HEADS = 8
RET_HEADS = 8
RET_CHUNK = 128
XA_HEADS = 4
S5_GROUP = 16
S5_STATE = 64
S5_CHUNK = 16
CONV_K = 3
S5_GROUP_SHIFT = S5_GROUP.bit_length() - 1
S5_STATE_SHIFT = S5_STATE.bit_length() - 1
GROUPS_PER_BLOCK = LANES // S5_GROUP
STATE_LANES = GROUPS_PER_BLOCK * S5_STATE
VMEM_LIMIT = 56 * 1024 * 1024
NEG_BIG = -1e30

PROJ_ROWS, PROJ_COLS = 1024, 1024
MIX_ROWS = 512
MLP_ROWS, MLP_COLS = 1024, 512
FOX_BLOCK = 512
FOX_HEADS_PER_STEP = 4
RET_ROWS = 1024
RET_HEADS_PER_STEP = 8


def _row_tile(rows, want):
    tile = min(rows, want)
    while rows % tile:
        tile //= 2
    return tile


def _params(*sem):
    return pltpu.CompilerParams(dimension_semantics=sem, vmem_limit_bytes=VMEM_LIMIT)


def _rms(x, g):
    ms = jnp.mean(x * x, axis=-1, keepdims=True)
    return x * lax.rsqrt(ms + EPS) * g


def _norm_matmul_kernel(x_ref, g_ref, w_ref, o_ref, xn_ref):
    @pl.when(pl.program_id(1) == 0)
    def _():
        xn_ref[...] = _rms(x_ref[...], g_ref[...]).astype(BF16)

    o_ref[...] = jnp.dot(xn_ref[...], w_ref[...], preferred_element_type=F32).astype(o_ref.dtype)


def norm_matmul(x, g, w, out_dtype, tm, tn):
    t, d = x.shape
    n = w.shape[1]
    return pl.pallas_call(
        _norm_matmul_kernel,
        grid=(t // tm, n // tn),
        in_specs=[
            pl.BlockSpec((tm, d), lambda i, j: (i, 0)),
            pl.BlockSpec((1, d), lambda i, j: (0, 0)),
            pl.BlockSpec((d, tn), lambda i, j: (0, j)),
        ],
        out_specs=pl.BlockSpec((tm, tn), lambda i, j: (i, j)),
        out_shape=jax.ShapeDtypeStruct((t, n), out_dtype),
        scratch_shapes=[pltpu.VMEM((tm, d), BF16)],
        compiler_params=_params("parallel", "arbitrary"),
        name="norm_matmul",
    )(x, g.reshape(1, d), w)


def _mlp_kernel(h_ref, g_ref, w1_ref, w2_ref, gf_ref, o_ref, xn_ref, *, final_norm):
    f = pl.program_id(1)

    @pl.when(f == 0)
    def _():
        x = h_ref[...]
        xn_ref[...] = _rms(x, g_ref[...]).astype(BF16)
        o_ref[...] = x

    a = jnp.dot(xn_ref[...], w1_ref[...], preferred_element_type=F32)
    a = jnp.square(jnp.maximum(a, 0.0)).astype(BF16)
    o_ref[...] += jnp.dot(a, w2_ref[...], preferred_element_type=F32)

    if final_norm:
        @pl.when(f == pl.num_programs(1) - 1)
        def _():
            o_ref[...] = _rms(o_ref[...], gf_ref[...])


def mlp_block(h, g, w1, w2, g_final, final_norm, tm, tf):
    t, d = h.shape
    dff = w1.shape[1]
    return pl.pallas_call(
        functools.partial(_mlp_kernel, final_norm=final_norm),
        grid=(t // tm, dff // tf),
        in_specs=[
            pl.BlockSpec((tm, d), lambda i, f: (i, 0)),
            pl.BlockSpec((1, d), lambda i, f: (0, 0)),
            pl.BlockSpec((d, tf), lambda i, f: (0, f)),
            pl.BlockSpec((tf, d), lambda i, f: (f, 0)),
            pl.BlockSpec((1, d), lambda i, f: (0, 0)),
        ],
        out_specs=pl.BlockSpec((tm, d), lambda i, f: (i, 0)),
        out_shape=jax.ShapeDtypeStruct((t, d), F32),
        scratch_shapes=[pltpu.VMEM((tm, d), BF16)],
        compiler_params=_params("parallel", "arbitrary"),
        name="mlp_block",
    )(h, g.reshape(1, d), w1, w2, g_final.reshape(1, d))


def _mix_xattn_kernel(h_ref, a1_ref, a2_ref, w1_ref, w2_ref, g_ref, wq_ref, k_ref, v_ref, wo_ref, o_ref):
    x = h_ref[...] + jnp.dot(a1_ref[...], w1_ref[...], preferred_element_type=F32)
    x = x + jnp.dot(a2_ref[...], w2_ref[...], preferred_element_type=F32)
    xn = _rms(x, g_ref[...]).astype(BF16)
    q = jnp.dot(xn, wq_ref[...], preferred_element_type=F32).astype(BF16)
    heads = []
    for hd in range(XA_HEADS):
        sl = slice(hd * HEAD_DIM, (hd + 1) * HEAD_DIM)
        s = lax.dot_general(q[:, sl], k_ref[0, :, sl], (((1,), (1,)), ((), ())),
                            preferred_element_type=F32)
        m = jnp.max(s, axis=-1, keepdims=True)
        p = jnp.exp(s - m)
        l = jnp.sum(p, axis=-1, keepdims=True)
        oh = jnp.dot(p.astype(BF16), v_ref[0, :, sl], preferred_element_type=F32)
        heads.append((oh * (1.0 / l)).astype(BF16))
    o = jnp.concatenate(heads, axis=-1)
    o_ref[...] = x + jnp.dot(o, wo_ref[...], preferred_element_type=F32)


def mix_xattn_block(h, a1, a2, w1, w2, g, wq, kv, wo, seq, tm):
    t, d = h.shape
    k1 = a1.shape[1]
    k2 = a2.shape[1]
    mlen = kv.shape[1]
    xw = wq.shape[1]
    per_batch = seq // tm

    def const(shape):
        return pl.BlockSpec(shape, lambda i: (0,) * len(shape))

    return pl.pallas_call(
        _mix_xattn_kernel,
        grid=(t // tm,),
        in_specs=[
            pl.BlockSpec((tm, d), lambda i: (i, 0)),
            pl.BlockSpec((tm, k1), lambda i: (i, 0)),
            pl.BlockSpec((tm, k2), lambda i: (i, 0)),
            const((k1, d)), const((k2, d)), const((1, d)), const((d, xw)),
            pl.BlockSpec((1, mlen, xw), lambda i: (i // per_batch, 0, 0)),
            pl.BlockSpec((1, mlen, xw), lambda i: (i // per_batch, 0, 1)),
            const((xw, d)),
        ],
        out_specs=pl.BlockSpec((tm, d), lambda i: (i, 0)),
        out_shape=jax.ShapeDtypeStruct((t, d), F32),
        compiler_params=_params("parallel"),
        name="mix_xattn_block",
    )(h, a1, a2, w1, w2, g.reshape(1, d), wq, kv, kv, wo)


def _split3(x):
    hi = x.astype(BF16)
    r1 = x - hi.astype(F32)
    mid = r1.astype(BF16)
    lo = (r1 - mid.astype(F32)).astype(BF16)
    return hi, mid, lo


def _fox_cumsum_kernel(fl_ref, b_ref, call_ref, *, blk):
    seq = fl_ref.shape[0]
    row = lax.broadcasted_iota(jnp.int32, (blk, blk), 0)
    col = lax.broadcasted_iota(jnp.int32, (blk, blk), 1)
    tri = jnp.where(row >= col, 1.0, 0.0).astype(BF16)
    carry = jnp.zeros((1, LANES), F32)
    for i in range(seq // blk):
        rows = slice(i * blk, (i + 1) * blk)
        x = fl_ref[rows, :] + b_ref[...]
        ls = jnp.minimum(x, 0.0) - jnp.log(1.0 + jnp.exp(-jnp.abs(x)))
        c = carry
        for piece in _split3(ls):
            c = c + jnp.dot(tri, piece, preferred_element_type=F32)
        call_ref[rows, :] = c
        carry = c[blk - 1:blk, :]


def fox_cumsum(fl, bias, batch, seq):
    blk = min(256, seq)
    return pl.pallas_call(
        functools.partial(_fox_cumsum_kernel, blk=blk),
        grid=(batch,),
        in_specs=[
            pl.BlockSpec((seq, LANES), lambda b: (b, 0)),
            pl.BlockSpec((1, LANES), lambda b: (0, 0)),
        ],
        out_specs=pl.BlockSpec((seq, LANES), lambda b: (b, 0)),
        out_shape=jax.ShapeDtypeStruct((batch * seq, LANES), F32),
        compiler_params=_params("parallel"),
        name="fox_cumsum",
    )(fl, bias)


def _lane_tile(x, reps):
    return jnp.concatenate([x] * reps, axis=1)


def _gate_lanes(c_col, lane, own_first):
    hi, mid, lo = [p.astype(F32) for p in _split3(c_col)]
    base = 0 if own_first else 3
    pieces = jnp.where(lane == base, hi, jnp.where(lane == base + 1, mid, jnp.where(lane == base + 2, lo, 0.0)))
    ones = jnp.where((lane >= 3 - base) & (lane < 6 - base), 1.0, 0.0)
    return (pieces + ones).astype(BF16)


def _fox_attn_kernel(q_ref, k_ref, v_ref, c_ref, o_ref, qaug_ref, kaug_ref, vaug_ref, m_ref, acc_ref,
                     *, blk, hp):
    g = pl.program_id(1)
    i = pl.program_id(2)
    seq = k_ref.shape[0]
    nt = (((1,), (1,)), ((), ()))

    def head_cols(t):
        return slice(t * HEAD_DIM, (t + 1) * HEAD_DIM)

    @pl.when(i == 0)
    def _():
        lane = lax.broadcasted_iota(jnp.int32, (seq, LANES), 1)
        for t in range(hp):
            c_key = jnp.sum(jnp.where(lane == g * hp + t, c_ref[...], 0.0), axis=1, keepdims=True)
            kaug_ref[t, :, :HEAD_DIM] = k_ref[:, head_cols(t)]
            kaug_ref[t, :, HEAD_DIM:] = _gate_lanes(-c_key, lane, own_first=False)
            vaug_ref[t, :, :HEAD_DIM] = v_ref[:, head_cols(t)]
            vaug_ref[t, :, HEAD_DIM:] = jnp.ones((seq, LANES), BF16)

    lane = lax.broadcasted_iota(jnp.int32, (blk, LANES), 1)
    c_rows = c_ref[pl.ds(pl.multiple_of(i * blk, blk), blk), :]
    for t in range(hp):
        c_query = jnp.sum(jnp.where(lane == g * hp + t, c_rows, 0.0), axis=1, keepdims=True)
        qaug_ref[t, :, :HEAD_DIM] = q_ref[:, head_cols(t)]
        qaug_ref[t, :, HEAD_DIM:] = _gate_lanes(c_query, lane, own_first=True)
    m_ref[...] = jnp.full(m_ref.shape, NEG_BIG, F32)
    acc_ref[...] = jnp.zeros(acc_ref.shape, F32)

    def step(j, masked):
        start = pl.multiple_of(j * blk, blk)
        scores = [lax.dot_general(qaug_ref[t], kaug_ref[t, pl.ds(start, blk), :], nt,
                                  preferred_element_type=F32) for t in range(hp)]
        for t in range(hp):
            s = scores[t]
            if masked:
                r = lax.broadcasted_iota(jnp.int32, (blk, blk), 0)
                c = lax.broadcasted_iota(jnp.int32, (blk, blk), 1)
                s = jnp.where(r >= c, s, -jnp.inf)
            m_prev = m_ref[t]
            m_new = jnp.maximum(m_prev, jnp.max(s, axis=-1, keepdims=True))
            alpha = jnp.exp(m_prev - m_new)
            p = jnp.exp(s - _lane_tile(m_new, blk // LANES)).astype(BF16)
            acc_ref[t] = (_lane_tile(alpha, acc_ref.shape[2] // LANES) * acc_ref[t]
                          + jnp.dot(p, vaug_ref[t, pl.ds(start, blk), :], preferred_element_type=F32))
            m_ref[t] = m_new

    def body(j, carry):
        step(j, False)
        return carry

    lax.fori_loop(0, i, body, 0)
    step(i, True)
    for t in range(hp):
        acc = acc_ref[t]
        o_ref[:, head_cols(t)] = (acc[:, :HEAD_DIM] * (1.0 / acc[:, HEAD_DIM:])).astype(o_ref.dtype)


def fox_attention(proj, call, batch, seq, q_col, k_col, v_col, blk, hp):
    nq = seq // blk
    width = hp * HEAD_DIM
    aug = HEAD_DIM + LANES
    return pl.pallas_call(
        functools.partial(_fox_attn_kernel, blk=blk, hp=hp),
        grid=(batch, FOX_HEADS // hp, nq),
        in_specs=[
            pl.BlockSpec((blk, width), lambda b, g, i: (b * nq + i, q_col // hp + g)),
            pl.BlockSpec((seq, width), lambda b, g, i: (b, k_col // hp + g)),
            pl.BlockSpec((seq, width), lambda b, g, i: (b, v_col // hp + g)),
            pl.BlockSpec((seq, LANES), lambda b, g, i: (b, 0)),
        ],
        out_specs=pl.BlockSpec((blk, width), lambda b, g, i: (b * nq + i, g)),
        out_shape=jax.ShapeDtypeStruct((batch * seq, FOX_HEADS * HEAD_DIM), BF16),
        scratch_shapes=[
            pltpu.VMEM((hp, blk, aug), BF16),
            pltpu.VMEM((hp, seq, aug), BF16),
            pltpu.VMEM((hp, seq, aug), BF16),
            pltpu.VMEM((hp, blk, LANES), F32),
            pltpu.VMEM((hp, blk, aug), F32),
        ],
        compiler_params=_params("parallel", "parallel", "arbitrary"),
        name="fox_attention",
    )(proj, proj, proj, call)


def _retention_kernel(q_ref, k_ref, v_ref, g_ref, cos_ref, sin_ref, dm_ref, aux_ref, o_ref, r_ref,
                      *, cs, hp):
    rows_per_step = q_ref.shape[0]
    half = HEAD_DIM // 2
    nt = (((1,), (1,)), ((), ()))

    @pl.when(pl.program_id(2) == 0)
    def _():
        r_ref[...] = jnp.zeros(r_ref.shape, F32)

    def body(c, carry):
        rows = pl.ds(pl.multiple_of(c * cs, cs), cs)
        cos = cos_ref[rows, :]
        sin = sin_ref[rows, :]
        for t in range(hp):
            cols = slice(t * HEAD_DIM, (t + 1) * HEAD_DIM)
            q = q_ref[rows, cols].astype(F32)
            k = k_ref[rows, cols].astype(F32)
            v = v_ref[rows, cols]
            qr = q * cos + pltpu.roll(q, half, 1) * sin
            kr = k * cos + pltpu.roll(k, half, 1) * sin
            qb = qr.astype(BF16)
            s = lax.dot_general(qb, kr.astype(BF16), nt, preferred_element_type=F32) * dm_ref[t]
            o = jnp.dot(s.astype(BF16), v, preferred_element_type=F32)
            r = r_ref[t]
            o = o + jnp.dot(qb, r.astype(BF16), preferred_element_type=F32) * aux_ref[t, 0]
            kz = (kr * aux_ref[t, 1]).astype(BF16)
            r_ref[t] = aux_ref[t, 2] * r + lax.dot_general(kz, v, (((0,), (0,)), ((), ())),
                                                          preferred_element_type=F32)
            mu = jnp.mean(o, axis=-1, keepdims=True)
            oc = o - mu
            var = jnp.mean(oc * oc, axis=-1, keepdims=True)
            on = oc * lax.rsqrt(var + EPS)
            g = g_ref[rows, cols].astype(F32)
            o_ref[rows, cols] = (g * jax.nn.sigmoid(g) * on).astype(o_ref.dtype)
        return carry

    lax.fori_loop(0, rows_per_step // cs, body, 0)


def retention(proj, cos, sin, dm, aux, batch, seq, q_col, k_col, v_col, g_col, rows, hp):
    width = hp * HEAD_DIM
    nr = seq // rows

    def col(c0):
        return pl.BlockSpec((rows, width), lambda b, g, i: (b * nr + i, c0 // hp + g))

    return pl.pallas_call(
        functools.partial(_retention_kernel, cs=RET_CHUNK, hp=hp),
        grid=(batch, RET_HEADS // hp, nr),
        in_specs=[
            col(q_col), col(k_col), col(v_col), col(g_col),
            pl.BlockSpec((rows, HEAD_DIM), lambda b, g, i: (i, 0)),
            pl.BlockSpec((rows, HEAD_DIM), lambda b, g, i: (i, 0)),
            pl.BlockSpec((hp, RET_CHUNK, RET_CHUNK), lambda b, g, i: (g, 0, 0)),
            pl.BlockSpec((hp, 3, RET_CHUNK, LANES), lambda b, g, i: (g, 0, 0, 0)),
        ],
        out_specs=pl.BlockSpec((rows, width), lambda b, g, i: (b * nr + i, g)),
        out_shape=jax.ShapeDtypeStruct((batch * seq, RET_HEADS * HEAD_DIM), BF16),
        scratch_shapes=[pltpu.VMEM((hp, HEAD_DIM, HEAD_DIM), F32)],
        compiler_params=_params("parallel", "parallel", "arbitrary"),
        name="retention",
    )(proj, proj, proj, proj, cos, sin, dm, aux)


def _retention_tables(seq):
    half = HEAD_DIM // 2
    inv = ROPE_BASE ** (-jnp.arange(half, dtype=F32) / half)
    ang = jnp.arange(seq, dtype=F32)[:, None] * inv[None, :]
    cos, sin = jnp.cos(ang), jnp.sin(ang)
    cos_t = jnp.concatenate([cos, cos], axis=-1)
    sin_t = jnp.concatenate([-sin, sin], axis=-1)
    cs = RET_CHUNK
    log_g = jnp.log1p(-jnp.exp2(-5.0 - jnp.arange(RET_HEADS, dtype=F32)))
    pos = jnp.arange(cs, dtype=F32)
    diff = pos[:, None] - pos[None, :]
    dm = jnp.where(diff >= 0, jnp.exp(log_g[:, None, None] * jnp.maximum(diff, 0.0)), 0.0)
    zeta = jnp.exp(log_g[:, None] * (cs - 1 - pos)[None, :])
    xi = jnp.exp(log_g[:, None] * (pos + 1)[None, :])
    g_chunk = jnp.broadcast_to(jnp.exp(log_g * cs)[:, None], (RET_HEADS, cs))
    aux = jnp.broadcast_to(jnp.stack([xi, zeta, g_chunk], axis=1)[..., None], (RET_HEADS, 3, cs, LANES))
    return cos_t, sin_t, dm, aux


def _s5_kernel(u_ref, toep_ref, win_ref, woutt_ref, aq_ref, d_ref, y_ref, ucat_ref, inc_ref, x_ref, *, q):
    m = u_ref.shape[0] // q
    sl = STATE_LANES
    for s in range(q):
        ucat_ref[:, s * LANES:(s + 1) * LANES] = u_ref[pl.ds(s, m, stride=q), :].astype(BF16)
    ucat = ucat_ref[...]
    inc_ref[...] = jnp.dot(ucat, win_ref[0], preferred_element_type=F32)
    a_re = aq_ref[0, 0:1, :]
    a_im = aq_ref[0, 1:2, :]

    def body(n, carry):
        x_re, x_im = carry
        x_ref[pl.ds(n, 1), 0:sl] = x_re
        x_ref[pl.ds(n, 1), sl:2 * sl] = x_im
        i_re = inc_ref[pl.ds(n, 1), 0:sl]
        i_im = inc_ref[pl.ds(n, 1), sl:2 * sl]
        return (a_re * x_re - a_im * x_im + i_re, a_re * x_im + a_im * x_re + i_im)

    zero = jnp.zeros((1, sl), F32)
    lax.fori_loop(0, m, body, (zero, zero))
    y = jnp.dot(ucat, toep_ref[0], preferred_element_type=F32)
    y = y + lax.dot_general(x_ref[...].astype(BF16), woutt_ref[0], (((1,), (1,)), ((), ())),
                            preferred_element_type=F32)
    for t in range(q):
        rows = pl.ds(t, m, stride=q)
        y_ref[rows, :] = y[:, t * LANES:(t + 1) * LANES] + d_ref[...] * u_ref[rows, :]


def s5_core(proj, toep, win, woutt, aq, d_skip, batch, seq, u_col):
    q = S5_CHUNK
    m = seq // q
    nblk = toep.shape[0]
    return pl.pallas_call(
        functools.partial(_s5_kernel, q=q),
        grid=(nblk, batch),
        in_specs=[
            pl.BlockSpec((seq, LANES), lambda j, b: (b, u_col + j)),
            pl.BlockSpec((1, q * LANES, q * LANES), lambda j, b: (j, 0, 0)),
            pl.BlockSpec((1, q * LANES, 2 * STATE_LANES), lambda j, b: (j, 0, 0)),
            pl.BlockSpec((1, q * LANES, 2 * STATE_LANES), lambda j, b: (j, 0, 0)),
            pl.BlockSpec((1, 2, STATE_LANES), lambda j, b: (j, 0, 0)),
            pl.BlockSpec((1, LANES), lambda j, b: (0, j)),
        ],
        out_specs=pl.BlockSpec((seq, LANES), lambda j, b: (b, j)),
        out_shape=jax.ShapeDtypeStruct((batch * seq, nblk * LANES), F32),
        scratch_shapes=[
            pltpu.VMEM((m, q * LANES), BF16),
            pltpu.VMEM((m, 2 * STATE_LANES), F32),
            pltpu.VMEM((m, 2 * STATE_LANES), F32),
        ],
        compiler_params=_params("parallel", "arbitrary"),
        name="s5_core",
    )(proj, toep, win, woutt, aq, d_skip)


def _dot_nt_split(x, y):
    nt = (((1,), (1,)), ((), ()))
    xh = x.astype(BF16)
    xl = (x - xh.astype(F32)).astype(BF16)
    yh = y.astype(BF16)
    yl = (y - yh.astype(F32)).astype(BF16)
    out = lax.dot_general(xh, yh, nt, preferred_element_type=F32)
    out = out + lax.dot_general(xh, yl, nt, preferred_element_type=F32)
    return out + lax.dot_general(xl, yh, nt, preferred_element_type=F32)


def _s5_tables_kernel(are_ref, aim_ref, ls_ref, bre_ref, bim_ref, cre_ref, cim_ref,
                      toep_ref, win_ref, woutt_ref, aq_ref, wf_ref, *, q):
    sl = STATE_LANES
    lam_re = jnp.minimum(are_ref[0], -1e-4)
    lam_im = aim_ref[0]
    step = jnp.exp(ls_ref[0])
    mag = jnp.exp(lam_re * step)
    a_re = mag * jnp.cos(lam_im * step)
    a_im = mag * jnp.sin(lam_im * step)
    den = lam_re * lam_re + lam_im * lam_im
    f_re = ((a_re - 1.0) * lam_re + a_im * lam_im) / den
    f_im = (a_im * lam_re - (a_re - 1.0) * lam_im) / den
    b_re = bre_ref[0]
    b_im = bim_ref[0]
    bb_re = f_re * b_re - f_im * b_im
    bb_im = f_re * b_im + f_im * b_re
    c_re = cre_ref[0]
    c_im = cim_ref[0]

    pows = [(jnp.ones((1, sl), F32), jnp.zeros((1, sl), F32))]
    for _ in range(q):
        p_re, p_im = pows[-1]
        pows.append((p_re * a_re - p_im * a_im, p_re * a_im + p_im * a_re))

    row_group = lax.shift_right_logical(lax.broadcasted_iota(jnp.int32, (LANES, sl), 0), S5_GROUP_SHIFT)
    col_group = lax.shift_right_logical(lax.broadcasted_iota(jnp.int32, (LANES, sl), 1), S5_STATE_SHIFT)
    same_group = row_group == col_group

    def tile(k, m_re, m_im, im_sign):
        p_re, p_im = pows[k]
        v_re = p_re * m_re - p_im * m_im
        v_im = (p_re * m_im + p_im * m_re) * im_sign
        e_re = jnp.where(same_group, jnp.concatenate([v_re] * GROUPS_PER_BLOCK, axis=0), 0.0)
        e_im = jnp.where(same_group, jnp.concatenate([v_im] * GROUPS_PER_BLOCK, axis=0), 0.0)
        return jnp.concatenate([e_re, e_im], axis=1)

    for s in range(q):
        rows = slice(s * LANES, (s + 1) * LANES)
        wf_ref[rows, :] = tile(q - 1 - s, bb_re, bb_im, 1.0)
        woutt_ref[0, rows, :] = tile(s + 1, c_re, c_im, -1.0).astype(BF16)
    win_ref[0] = wf_ref[...].astype(BF16)
    taps = _dot_nt_split(wf_ref[...], tile(0, c_re, c_im, -1.0))
    toep_ref[0] = jnp.zeros(toep_ref.shape[1:], BF16)
    for s in range(q):
        for t in range(s, q):
            lag_rows = slice((q - 1 - (t - s)) * LANES, (q - (t - s)) * LANES)
            toep_ref[0, s * LANES:(s + 1) * LANES, t * LANES:(t + 1) * LANES] = taps[lag_rows, :].astype(BF16)
    aq_ref[0, 0:1, :] = pows[q][0]
    aq_ref[0, 1:2, :] = pows[q][1]


def s5_tables(a_re, a_im, b_re, b_im, c_re, c_im, log_step):
    q = S5_CHUNK
    g, p = a_re.shape
    gb = GROUPS_PER_BLOCK
    nblk = g // gb
    sl = gb * p

    def rows(t):
        return t.astype(F32).reshape(nblk, 1, sl)

    def b_mat(t):
        return t.astype(F32).reshape(nblk, gb, p, S5_GROUP).transpose(0, 3, 1, 2).reshape(nblk, S5_GROUP, sl)

    def c_mat(t):
        return t.astype(F32).reshape(nblk, gb, S5_GROUP, p).transpose(0, 2, 1, 3).reshape(nblk, S5_GROUP, sl)

    row_spec = pl.BlockSpec((1, 1, sl), lambda j: (j, 0, 0))
    mat_spec = pl.BlockSpec((1, S5_GROUP, sl), lambda j: (j, 0, 0))
    return pl.pallas_call(
        functools.partial(_s5_tables_kernel, q=q),
        grid=(nblk,),
        in_specs=[row_spec, row_spec, row_spec, mat_spec, mat_spec, mat_spec, mat_spec],
        out_specs=[
            pl.BlockSpec((1, q * LANES, q * LANES), lambda j: (j, 0, 0)),
            pl.BlockSpec((1, q * LANES, 2 * sl), lambda j: (j, 0, 0)),
            pl.BlockSpec((1, q * LANES, 2 * sl), lambda j: (j, 0, 0)),
            pl.BlockSpec((1, 2, sl), lambda j: (j, 0, 0)),
        ],
        out_shape=[
            jax.ShapeDtypeStruct((nblk, q * LANES, q * LANES), BF16),
            jax.ShapeDtypeStruct((nblk, q * LANES, 2 * sl), BF16),
            jax.ShapeDtypeStruct((nblk, q * LANES, 2 * sl), BF16),
            jax.ShapeDtypeStruct((nblk, 2, sl), F32),
        ],
        scratch_shapes=[pltpu.VMEM((q * LANES, 2 * sl), F32)],
        compiler_params=_params("parallel"),
        name="s5_tables",
    )(rows(a_re), rows(a_im), rows(log_step), b_mat(b_re), b_mat(b_im), c_mat(c_re), c_mat(c_im))


def _s5_glu_kernel(y_ref, w_ref, o_ref):
    y = y_ref[...]
    g = 0.5 * y * (1.0 + jnp.tanh(math.sqrt(2.0 / math.pi) * (y + 0.044715 * (y * y * y))))
    z = jnp.dot(g.astype(BF16), w_ref[...], preferred_element_type=F32)
    o_ref[...] = (g * jax.nn.sigmoid(z)).astype(o_ref.dtype)


def s5_glu(y, w, tm):
    t, n = y.shape
    return pl.pallas_call(
        _s5_glu_kernel,
        grid=(t // tm,),
        in_specs=[
            pl.BlockSpec((tm, n), lambda i: (i, 0)),
            pl.BlockSpec((n, n), lambda i: (0, 0)),
        ],
        out_specs=pl.BlockSpec((tm, n), lambda i: (i, 0)),
        out_shape=jax.ShapeDtypeStruct((t, n), BF16),
        compiler_params=_params("parallel"),
        name="s5_glu",
    )(y, w)


def _short_conv_kernel(h_ref, gb_ref, gc_ref, w_ref, o_ref):
    z = gc_ref[...] * h_ref[...]
    row = lax.broadcasted_iota(jnp.int32, z.shape, 0)
    y = w_ref[CONV_K - 1:CONV_K, :] * z
    for lag in range(1, CONV_K):
        zl = jnp.where(row >= lag, pltpu.roll(z, lag, 0), 0.0)
        y = y + w_ref[CONV_K - 1 - lag:CONV_K - lag, :] * zl
    o_ref[...] = (gb_ref[...] * y).astype(o_ref.dtype)


def short_conv(proj, conv_w, batch, seq, h_col, gb_col, gc_col):
    nblk = conv_w.shape[1] // LANES

    def col(c0):
        return pl.BlockSpec((seq, LANES), lambda b, j: (b, c0 + j))

    return pl.pallas_call(
        _short_conv_kernel,
        grid=(batch, nblk),
        in_specs=[col(h_col), col(gb_col), col(gc_col),
                  pl.BlockSpec((CONV_K, LANES), lambda b, j: (0, j))],
        out_specs=pl.BlockSpec((seq, LANES), lambda b, j: (b, j)),
        out_shape=jax.ShapeDtypeStruct((batch * seq, nblk * LANES), BF16),
        compiler_params=_params("parallel", "parallel"),
        name="short_conv",
    )(proj, proj, proj, conv_w)


def _even_mixer(h, g, w_in, b_forget, tables, batch, seq):
    fw = FOX_HEADS * HEAD_DIM
    rw = RET_HEADS * HEAD_DIM
    scale = HEAD_DIM ** -0.5
    c0 = 3 * fw
    c1 = c0 + FOX_HEADS
    w_main = jnp.concatenate([
        w_in[:, :fw] * scale, w_in[:, fw:c0],
        w_in[:, c1:c1 + rw], w_in[:, c1 + rw:c1 + 2 * rw] * scale, w_in[:, c1 + 2 * rw:],
    ], axis=1).astype(BF16)
    w_forget = jnp.pad(w_in[:, c0:c1], ((0, 0), (0, LANES - FOX_HEADS))).astype(BF16)
    bias = jnp.pad(b_forget.astype(F32), (0, LANES - FOX_HEADS)).reshape(1, LANES)

    t = h.shape[0]
    proj = norm_matmul(h, g, w_main, BF16, tm=_row_tile(t, PROJ_ROWS), tn=PROJ_COLS)
    fl = norm_matmul(h, g, w_forget, F32, tm=_row_tile(t, PROJ_ROWS), tn=LANES)
    call = fox_cumsum(fl, bias, batch, seq)
    nh = FOX_HEADS
    fox = fox_attention(proj, call, batch, seq, 0, nh, 2 * nh, blk=_row_tile(seq, FOX_BLOCK),
                        hp=FOX_HEADS_PER_STEP)
    cos_t, sin_t, dm, aux = tables
    ret = retention(proj, cos_t, sin_t, dm, aux, batch, seq, 3 * nh, 4 * nh, 5 * nh, 6 * nh,
                    rows=_row_tile(seq, RET_ROWS), hp=RET_HEADS_PER_STEP)
    return fox, ret


def _odd_mixer(h, g, w_in, a_re, a_im, b_re, b_im, c_re, c_im, d_skip, log_step, w_glu, conv_w,
               batch, seq):
    sw = a_re.shape[0] * S5_GROUP
    nblk = sw // LANES
    t = h.shape[0]
    proj = norm_matmul(h, g, w_in.astype(BF16), F32, tm=_row_tile(t, PROJ_ROWS), tn=PROJ_COLS)
    toep, win, woutt, aq = s5_tables(a_re, a_im, b_re, b_im, c_re, c_im, log_step)
    y = s5_core(proj, toep, win, woutt, aq, d_skip.reshape(1, sw).astype(F32), batch, seq, 0)
    ssm = s5_glu(y, w_glu.astype(BF16), tm=_row_tile(t, MIX_ROWS))
    conv = short_conv(proj, conv_w.astype(F32), batch, seq, nblk, 2 * nblk, 3 * nblk)
    return ssm, conv


def kernel(x, mem, norm_mix, norm_xattn, norm_mlp, norm_mem, norm_final, ab_w_in, ab_b_forget, ab_w_out,
           cd_w_in, s5_a_re, s5_a_im, s5_b_re, s5_b_im, s5_c_re, s5_c_im, s5_d, s5_log_step, s5_w_glu,
           conv_w, cd_w_out, xa_wq, xa_wkv, xa_wo, mlp_w1, mlp_w2):
    batch, seq, d = x.shape
    depth = norm_mix.shape[0]
    mlen = mem.shape[1]
    xw = xa_wq.shape[2]
    h = x.reshape(batch * seq, d)
    memf = mem.reshape(batch * mlen, d)
    tables = _retention_tables(seq)
    xa_scale = HEAD_DIM ** -0.5
    for layer in range(depth):
        if layer % 2 == 0:
            e = layer // 2
            a1, a2 = _even_mixer(h, norm_mix[layer], ab_w_in[e], ab_b_forget[e], tables, batch, seq)
            w_o = ab_w_out[e].astype(BF16)
        else:
            o = layer // 2
            a1, a2 = _odd_mixer(h, norm_mix[layer], cd_w_in[o], s5_a_re[o], s5_a_im[o], s5_b_re[o],
                                s5_b_im[o], s5_c_re[o], s5_c_im[o], s5_d[o], s5_log_step[o], s5_w_glu[o],
                                conv_w[o], batch, seq)
            w_o = cd_w_out[o].astype(BF16)
        kv = norm_matmul(memf, norm_mem, xa_wkv[layer].astype(BF16), BF16,
                         tm=_row_tile(batch * mlen, PROJ_ROWS), tn=PROJ_COLS)
        kv = kv.reshape(batch, mlen, 2 * xw)
        k1 = a1.shape[1]
        h = mix_xattn_block(h, a1, a2, w_o[:k1], w_o[k1:], norm_xattn[layer],
                            (xa_wq[layer] * xa_scale).astype(BF16), kv, xa_wo[layer].astype(BF16),
                            seq, tm=_row_tile(seq, MIX_ROWS))
        h = mlp_block(h, norm_mlp[layer], mlp_w1[layer].astype(BF16), mlp_w2[layer].astype(BF16),
                      norm_final, layer == depth - 1, tm=_row_tile(batch * seq, MLP_ROWS), tf=MLP_COLS)
    return h.reshape(batch, seq, d)
```

```python
import functools
import math

import jax
import jax.numpy as jnp
from jax import lax
from jax.experimental import pallas as pl
from jax.experimental.pallas import tpu as pltpu

F32 = jnp.float32
BF16 = jnp.bfloat16

EPS = 1e-6
ROPE_BASE = 10000.0
LANES = 128
HEAD_DIM = 128
FOX_HEADS = 8
RET_HEADS = 8
RET_CHUNK = 128
XA_HEADS = 4
S5_GROUP = 16
S5_STATE = 64
S5_CHUNK = 16
CONV_K = 3
S5_GROUP_SHIFT = S5_GROUP.bit_length() - 1
S5_STATE_SHIFT = S5_STATE.bit_length() - 1
GROUPS_PER_BLOCK = LANES // S5_GROUP
STATE_LANES = GROUPS_PER_BLOCK * S5_STATE
VMEM_LIMIT = 56 * 1024 * 1024
NEG_BIG = -1e30

PROJ_ROWS, PROJ_COLS = 1024, 1024
MIX_ROWS = 512
MLP_ROWS, MLP_COLS = 1024, 512
FOX_BLOCK = 512
FOX_HEADS_PER_STEP = 4
RET_ROWS = 1024
RET_HEADS_PER_STEP = 8


def _row_tile(rows, want):
    tile = min(rows, want)
    while rows % tile:
        tile //= 2
    return tile


def _params(*sem):
    return pltpu.CompilerParams(dimension_semantics=sem, vmem_limit_bytes=VMEM_LIMIT)


def _rms(x, g):
    ms = jnp.mean(x * x, axis=-1, keepdims=True)
    return x * lax.rsqrt(ms + EPS) * g


def _norm_matmul_kernel(x_ref, g_ref, w_ref, o_ref, xn_ref):
    @pl.when(pl.program_id(1) == 0)
    def _():
        xn_ref[...] = _rms(x_ref[...], g_ref[...]).astype(BF16)

    o_ref[...] = jnp.dot(xn_ref[...], w_ref[...], preferred_element_type=F32).astype(o_ref.dtype)


def norm_matmul(x, g, w, out_dtype, tm, tn):
    t, d = x.shape
    n = w.shape[1]
    return pl.pallas_call(
        _norm_matmul_kernel,
        grid=(t // tm, n // tn),
        in_specs=[
            pl.BlockSpec((tm, d), lambda i, j: (i, 0)),
            pl.BlockSpec((1, d), lambda i, j: (0, 0)),
            pl.BlockSpec((d, tn), lambda i, j: (0, j)),
        ],
        out_specs=pl.BlockSpec((tm, tn), lambda i, j: (i, j)),
        out_shape=jax.ShapeDtypeStruct((t, n), out_dtype),
        scratch_shapes=[pltpu.VMEM((tm, d), BF16)],
        compiler_params=_params("parallel", "arbitrary"),
        name="norm_matmul",
    )(x, g.reshape(1, d), w)


def _mlp_kernel(h_ref, g_ref, w1_ref, w2_ref, gf_ref, o_ref, xn_ref, *, final_norm):
    f = pl.program_id(1)

    @pl.when(f == 0)
    def _():
        x = h_ref[...]
        xn_ref[...] = _rms(x, g_ref[...]).astype(BF16)
        o_ref[...] = x

    a = jnp.dot(xn_ref[...], w1_ref[...], preferred_element_type=F32)
    a = jnp.square(jnp.maximum(a, 0.0)).astype(BF16)
    o_ref[...] += jnp.dot(a, w2_ref[...], preferred_element_type=F32)

    if final_norm:
        @pl.when(f == pl.num_programs(1) - 1)
        def _():
            o_ref[...] = _rms(o_ref[...], gf_ref[...])


def mlp_block(h, g, w1, w2, g_final, final_norm, tm, tf):
    t, d = h.shape
    dff = w1.shape[1]
    return pl.pallas_call(
        functools.partial(_mlp_kernel, final_norm=final_norm),
        grid=(t // tm, dff // tf),
        in_specs=[
            pl.BlockSpec((tm, d), lambda i, f: (i, 0)),
            pl.BlockSpec((1, d), lambda i, f: (0, 0)),
            pl.BlockSpec((d, tf), lambda i, f: (0, f)),
            pl.BlockSpec((tf, d), lambda i, f: (f, 0)),
            pl.BlockSpec((1, d), lambda i, f: (0, 0)),
        ],
        out_specs=pl.BlockSpec((tm, d), lambda i, f: (i, 0)),
        out_shape=jax.ShapeDtypeStruct((t, d), F32),
        scratch_shapes=[pltpu.VMEM((tm, d), BF16)],
        compiler_params=_params("parallel", "arbitrary"),
        name="mlp_block",
    )(h, g.reshape(1, d), w1, w2, g_final.reshape(1, d))


def _s5_gate(y, w_glu):
    g = 0.5 * y * (1.0 + jnp.tanh(math.sqrt(2.0 / math.pi) * (y + 0.044715 * (y * y * y))))
    z = jnp.dot(g.astype(BF16), w_glu, preferred_element_type=F32)
    return (g * jax.nn.sigmoid(z)).astype(BF16)


def _short_conv(hc, gb, gc, z_prev, w_ref):
    z = gc * hc
    row = lax.broadcasted_iota(jnp.int32, z.shape, 0)
    y = w_ref[CONV_K - 1:CONV_K, :] * z
    for lag in range(1, CONV_K):
        zl = pltpu.roll(z, lag, 0)
        for r in range(lag):
            zl = jnp.where(row == r, z_prev[8 - lag + r:8 - lag + r + 1, :], zl)
        y = y + w_ref[CONV_K - 1 - lag:CONV_K - lag, :] * zl
    return (gb * y).astype(BF16)


def _mix_xattn_kernel(*refs, odd, per_batch):
    if odd:
        (h_ref, y_ref, hc_ref, gb_ref, gc_ref, hcp_ref, gcp_ref, cw_ref, wglu_ref,
         w1_ref, w2_ref, g_ref, wq_ref, k_ref, v_ref, wo_ref, o_ref) = refs
        a1 = _s5_gate(y_ref[...], wglu_ref[...])
        first = pl.program_id(0) % per_batch == 0
        z_prev = jnp.where(first, 0.0, gcp_ref[...] * hcp_ref[...])
        a2 = _short_conv(hc_ref[...], gb_ref[...], gc_ref[...], z_prev, cw_ref)
    else:
        h_ref, a1_ref, a2_ref, w1_ref, w2_ref, g_ref, wq_ref, k_ref, v_ref, wo_ref, o_ref = refs
        a1 = a1_ref[...]
        a2 = a2_ref[...]
    x = h_ref[...] + jnp.dot(a1, w1_ref[...], preferred_element_type=F32)
    x = x + jnp.dot(a2, w2_ref[...], preferred_element_type=F32)
    xn = _rms(x, g_ref[...]).astype(BF16)
    q = jnp.dot(xn, wq_ref[...], preferred_element_type=F32).astype(BF16)
    heads = []
    for hd in range(XA_HEADS):
        sl = slice(hd * HEAD_DIM, (hd + 1) * HEAD_DIM)
        s = lax.dot_general(q[:, sl], k_ref[0, :, sl], (((1,), (1,)), ((), ())),
                            preferred_element_type=F32)
        m = jnp.max(s, axis=-1, keepdims=True)
        p = jnp.exp(s - m)
        l = jnp.sum(p, axis=-1, keepdims=True)
        oh = jnp.dot(p.astype(BF16), v_ref[0, :, sl], preferred_element_type=F32)
        heads.append((oh * (1.0 / l)).astype(BF16))
    o = jnp.concatenate(heads, axis=-1)
    o_ref[...] = x + jnp.dot(o, wo_ref[...], preferred_element_type=F32)


def mix_xattn_block(h, mixer_inputs, w1, w2, g, wq, kv, kv_col, wo, seq, tm, odd):
    t, d = h.shape
    k1, k2 = w1.shape[0], w2.shape[0]
    mlen = kv.shape[1]
    xw = wq.shape[1]
    per_batch = seq // tm

    def const(shape):
        return pl.BlockSpec(shape, lambda i: (0,) * len(shape), pipeline_mode=pl.Buffered(1))

    def rows(width, col=0):
        return pl.BlockSpec((tm, width), lambda i: (i, col))

    if odd:
        y, proj, conv_w, w_glu = mixer_inputs
        halo = tm // 8

        def prev_rows(col):
            return pl.BlockSpec((8, k2), lambda i: (jnp.maximum(i * halo - 1, 0), col))

        mixer_args = (y, proj, proj, proj, proj, proj, conv_w, w_glu)
        mixer_specs = [rows(k1), rows(k2, 1), rows(k2, 2), rows(k2, 3), prev_rows(1), prev_rows(3),
                       const(conv_w.shape), const(w_glu.shape)]
    else:
        mixer_args = mixer_inputs
        mixer_specs = [rows(k1), rows(k2)]

    return pl.pallas_call(
        functools.partial(_mix_xattn_kernel, odd=odd, per_batch=per_batch),
        grid=(t // tm,),
        in_specs=[rows(d)] + mixer_specs + [
            const((k1, d)), const((k2, d)), const((1, d)), const((d, xw)),
            pl.BlockSpec((1, mlen, xw), lambda i: (i // per_batch, 0, kv_col)),
            pl.BlockSpec((1, mlen, xw), lambda i: (i // per_batch, 0, kv_col + 1)),
            const((xw, d)),
        ],
        out_specs=rows(d),
        out_shape=jax.ShapeDtypeStruct((t, d), F32),
        compiler_params=_params("parallel"),
        name="mix_xattn_block",
    )(h, *mixer_args, w1, w2, g.reshape(1, d), wq, kv, kv, wo)


def _split3(x):
    hi = x.astype(BF16)
    r1 = x - hi.astype(F32)
    mid = r1.astype(BF16)
    lo = (r1 - mid.astype(F32)).astype(BF16)
    return hi, mid, lo


def _fox_cumsum_kernel(fl_ref, b_ref, call_ref, *, blk):
    seq = fl_ref.shape[0]
    row = lax.broadcasted_iota(jnp.int32, (blk, blk), 0)
    col = lax.broadcasted_iota(jnp.int32, (blk, blk), 1)
    tri = jnp.where(row >= col, 1.0, 0.0).astype(BF16)
    carry = jnp.zeros((1, LANES), F32)
    for i in range(seq // blk):
        rows = slice(i * blk, (i + 1) * blk)
        x = fl_ref[rows, :] + b_ref[...]
        ls = jnp.minimum(x, 0.0) - jnp.log(1.0 + jnp.exp(-jnp.abs(x)))
        c = carry
        for piece in _split3(ls):
            c = c + jnp.dot(tri, piece, preferred_element_type=F32)
        call_ref[rows, :] = c
        carry = c[blk - 1:blk, :]


def fox_cumsum(fl, bias, batch, seq):
    blk = min(256, seq)
    return pl.pallas_call(
        functools.partial(_fox_cumsum_kernel, blk=blk),
        grid=(batch,),
        in_specs=[
            pl.BlockSpec((seq, LANES), lambda b: (b, 0)),
            pl.BlockSpec((1, LANES), lambda b: (0, 0)),
        ],
        out_specs=pl.BlockSpec((seq, LANES), lambda b: (b, 0)),
        out_shape=jax.ShapeDtypeStruct((batch * seq, LANES), F32),
        compiler_params=_params("parallel"),
        name="fox_cumsum",
    )(fl, bias)


def _lane_tile(x, reps):
    return jnp.concatenate([x] * reps, axis=1)


def _gate_lanes(c_col, lane, own_first):
    hi, mid, lo = [p.astype(F32) for p in _split3(c_col)]
    base = 0 if own_first else 3
    pieces = jnp.where(lane == base, hi, jnp.where(lane == base + 1, mid, jnp.where(lane == base + 2, lo, 0.0)))
    ones = jnp.where((lane >= 3 - base) & (lane < 6 - base), 1.0, 0.0)
    return (pieces + ones).astype(BF16)


def _fox_attn_kernel(q_ref, k_ref, v_ref, c_ref, o_ref, qaug_ref, kaug_ref, vaug_ref, m_ref, acc_ref,
                     *, blk, hp):
    g = pl.program_id(1)
    i = pl.program_id(2)
    seq = k_ref.shape[0]
    nt = (((1,), (1,)), ((), ()))

    def head_cols(t):
        return slice(t * HEAD_DIM, (t + 1) * HEAD_DIM)

    @pl.when(i == 0)
    def _():
        lane = lax.broadcasted_iota(jnp.int32, (seq, LANES), 1)
        for t in range(hp):
            c_key = jnp.sum(jnp.where(lane == g * hp + t, c_ref[...], 0.0), axis=1, keepdims=True)
            kaug_ref[t, :, :HEAD_DIM] = k_ref[:, head_cols(t)]
            kaug_ref[t, :, HEAD_DIM:] = _gate_lanes(-c_key, lane, own_first=False)
            vaug_ref[t, :, :HEAD_DIM] = v_ref[:, head_cols(t)]
            vaug_ref[t, :, HEAD_DIM:] = jnp.ones((seq, LANES), BF16)

    lane = lax.broadcasted_iota(jnp.int32, (blk, LANES), 1)
    c_rows = c_ref[pl.ds(pl.multiple_of(i * blk, blk), blk), :]
    for t in range(hp):
        c_query = jnp.sum(jnp.where(lane == g * hp + t, c_rows, 0.0), axis=1, keepdims=True)
        qaug_ref[t, :, :HEAD_DIM] = q_ref[:, head_cols(t)]
        qaug_ref[t, :, HEAD_DIM:] = _gate_lanes(c_query, lane, own_first=True)
    m_ref[...] = jnp.full(m_ref.shape, NEG_BIG, F32)
    acc_ref[...] = jnp.zeros(acc_ref.shape, F32)

    def step(j, masked):
        start = pl.multiple_of(j * blk, blk)
        scores = [lax.dot_general(qaug_ref[t], kaug_ref[t, pl.ds(start, blk), :], nt,
                                  preferred_element_type=F32) for t in range(hp)]
        for t in range(hp):
            s = scores[t]
            if masked:
                r = lax.broadcasted_iota(jnp.int32, (blk, blk), 0)
                c = lax.broadcasted_iota(jnp.int32, (blk, blk), 1)
                s = jnp.where(r >= c, s, -jnp.inf)
            m_prev = m_ref[t]
            m_new = jnp.maximum(m_prev, jnp.max(s, axis=-1, keepdims=True))
            alpha = jnp.exp(m_prev - m_new)
            p = jnp.exp(s - _lane_tile(m_new, blk // LANES)).astype(BF16)
            acc_ref[t] = (_lane_tile(alpha, acc_ref.shape[2] // LANES) * acc_ref[t]
                          + jnp.dot(p, vaug_ref[t, pl.ds(start, blk), :], preferred_element_type=F32))
            m_ref[t] = m_new

    def body(j, carry):
        step(j, False)
        return carry

    lax.fori_loop(0, i, body, 0)
    step(i, True)
    for t in range(hp):
        acc = acc_ref[t]
        o_ref[:, head_cols(t)] = (acc[:, :HEAD_DIM] * (1.0 / acc[:, HEAD_DIM:])).astype(o_ref.dtype)


def fox_attention(proj, call, batch, seq, q_col, k_col, v_col, blk, hp):
    nq = seq // blk
    width = hp * HEAD_DIM
    aug = HEAD_DIM + LANES
    return pl.pallas_call(
        functools.partial(_fox_attn_kernel, blk=blk, hp=hp),
        grid=(batch, FOX_HEADS // hp, nq),
        in_specs=[
            pl.BlockSpec((blk, width), lambda b, g, i: (b * nq + i, q_col // hp + g)),
            pl.BlockSpec((seq, width), lambda b, g, i: (b, k_col // hp + g)),
            pl.BlockSpec((seq, width), lambda b, g, i: (b, v_col // hp + g)),
            pl.BlockSpec((seq, LANES), lambda b, g, i: (b, 0)),
        ],
        out_specs=pl.BlockSpec((blk, width), lambda b, g, i: (b * nq + i, g)),
        out_shape=jax.ShapeDtypeStruct((batch * seq, FOX_HEADS * HEAD_DIM), BF16),
        scratch_shapes=[
            pltpu.VMEM((hp, blk, aug), BF16),
            pltpu.VMEM((hp, seq, aug), BF16),
            pltpu.VMEM((hp, seq, aug), BF16),
            pltpu.VMEM((hp, blk, LANES), F32),
            pltpu.VMEM((hp, blk, aug), F32),
        ],
        compiler_params=_params("parallel", "parallel", "arbitrary"),
        name="fox_attention",
    )(proj, proj, proj, call)


def _retention_kernel(q_ref, k_ref, v_ref, g_ref, cos_ref, sin_ref, dm_ref, aux_ref, o_ref, r_ref,
                      *, cs, hp):
    rows_per_step = q_ref.shape[0]
    half = HEAD_DIM // 2
    nt = (((1,), (1,)), ((), ()))

    @pl.when(pl.program_id(2) == 0)
    def _():
        r_ref[...] = jnp.zeros(r_ref.shape, F32)

    def body(c, carry):
        rows = pl.ds(pl.multiple_of(c * cs, cs), cs)
        cos = cos_ref[rows, :]
        sin = sin_ref[rows, :]
        for t in range(hp):
            cols = slice(t * HEAD_DIM, (t + 1) * HEAD_DIM)
            q = q_ref[rows, cols].astype(F32)
            k = k_ref[rows, cols].astype(F32)
            v = v_ref[rows, cols]
            qr = q * cos + pltpu.roll(q, half, 1) * sin
            kr = k * cos + pltpu.roll(k, half, 1) * sin
            qb = qr.astype(BF16)
            s = lax.dot_general(qb, kr.astype(BF16), nt, preferred_element_type=F32) * dm_ref[t]
            o = jnp.dot(s.astype(BF16), v, preferred_element_type=F32)
            r = r_ref[t]
            o = o + jnp.dot(qb, r.astype(BF16), preferred_element_type=F32) * aux_ref[t, 0]
            kz = (kr * aux_ref[t, 1]).astype(BF16)
            r_ref[t] = aux_ref[t, 2] * r + lax.dot_general(kz, v, (((0,), (0,)), ((), ())),
                                                          preferred_element_type=F32)
            mu = jnp.mean(o, axis=-1, keepdims=True)
            oc = o - mu
            var = jnp.mean(oc * oc, axis=-1, keepdims=True)
            on = oc * lax.rsqrt(var + EPS)
            g = g_ref[rows, cols].astype(F32)
            o_ref[rows, cols] = (g * jax.nn.sigmoid(g) * on).astype(o_ref.dtype)
        return carry

    lax.fori_loop(0, rows_per_step // cs, body, 0)


def retention(proj, cos, sin, dm, aux, batch, seq, q_col, k_col, v_col, g_col, rows, hp):
    width = hp * HEAD_DIM
    nr = seq // rows

    def col(c0):
        return pl.BlockSpec((rows, width), lambda b, g, i: (b * nr + i, c0 // hp + g))

    return pl.pallas_call(
        functools.partial(_retention_kernel, cs=RET_CHUNK, hp=hp),
        grid=(batch, RET_HEADS // hp, nr),
        in_specs=[
            col(q_col), col(k_col), col(v_col), col(g_col),
            pl.BlockSpec((rows, HEAD_DIM), lambda b, g, i: (i, 0)),
            pl.BlockSpec((rows, HEAD_DIM), lambda b, g, i: (i, 0)),
            pl.BlockSpec((hp, RET_CHUNK, RET_CHUNK), lambda b, g, i: (g, 0, 0)),
            pl.BlockSpec((hp, 3, RET_CHUNK, LANES), lambda b, g, i: (g, 0, 0, 0)),
        ],
        out_specs=pl.BlockSpec((rows, width), lambda b, g, i: (b * nr + i, g)),
        out_shape=jax.ShapeDtypeStruct((batch * seq, RET_HEADS * HEAD_DIM), BF16),
        scratch_shapes=[pltpu.VMEM((hp, HEAD_DIM, HEAD_DIM), F32)],
        compiler_params=_params("parallel", "parallel", "arbitrary"),
        name="retention",
    )(proj, proj, proj, proj, cos, sin, dm, aux)


def _retention_tables(seq):
    half = HEAD_DIM // 2
    inv = ROPE_BASE ** (-jnp.arange(half, dtype=F32) / half)
    ang = jnp.arange(seq, dtype=F32)[:, None] * inv[None, :]
    cos, sin = jnp.cos(ang), jnp.sin(ang)
    cos_t = jnp.concatenate([cos, cos], axis=-1)
    sin_t = jnp.concatenate([-sin, sin], axis=-1)
    cs = RET_CHUNK
    log_g = jnp.log1p(-jnp.exp2(-5.0 - jnp.arange(RET_HEADS, dtype=F32)))
    pos = jnp.arange(cs, dtype=F32)
    diff = pos[:, None] - pos[None, :]
    dm = jnp.where(diff >= 0, jnp.exp(log_g[:, None, None] * jnp.maximum(diff, 0.0)), 0.0)
    zeta = jnp.exp(log_g[:, None] * (cs - 1 - pos)[None, :])
    xi = jnp.exp(log_g[:, None] * (pos + 1)[None, :])
    g_chunk = jnp.broadcast_to(jnp.exp(log_g * cs)[:, None], (RET_HEADS, cs))
    aux = jnp.broadcast_to(jnp.stack([xi, zeta, g_chunk], axis=1)[..., None], (RET_HEADS, 3, cs, LANES))
    return cos_t, sin_t, dm, aux


def _s5_kernel(u_ref, toep_ref, win_ref, woutt_ref, aq_ref, d_ref, y_ref, ucat_ref, inc_ref, x_ref, *, q):
    m = u_ref.shape[0] // q
    sl = STATE_LANES
    for s in range(q):
        ucat_ref[:, s * LANES:(s + 1) * LANES] = u_ref[pl.ds(s, m, stride=q), :].astype(BF16)
    ucat = ucat_ref[...]
    inc_ref[...] = jnp.dot(ucat, win_ref[0], preferred_element_type=F32)
    a_re = aq_ref[0, 0:1, :]
    a_im = aq_ref[0, 1:2, :]

    def body(n, carry):
        x_re, x_im = carry
        x_ref[pl.ds(n, 1), 0:sl] = x_re
        x_ref[pl.ds(n, 1), sl:2 * sl] = x_im
        i_re = inc_ref[pl.ds(n, 1), 0:sl]
        i_im = inc_ref[pl.ds(n, 1), sl:2 * sl]
        return (a_re * x_re - a_im * x_im + i_re, a_re * x_im + a_im * x_re + i_im)

    zero = jnp.zeros((1, sl), F32)
    lax.fori_loop(0, m, body, (zero, zero))
    y = jnp.dot(ucat, toep_ref[0], preferred_element_type=F32)
    y = y + lax.dot_general(x_ref[...].astype(BF16), woutt_ref[0], (((1,), (1,)), ((), ())),
                            preferred_element_type=F32)
    for t in range(q):
        rows = pl.ds(t, m, stride=q)
        y_ref[rows, :] = y[:, t * LANES:(t + 1) * LANES] + d_ref[...] * u_ref[rows, :]


def s5_core(proj, toep, win, woutt, aq, d_skip, batch, seq, u_col):
    q = S5_CHUNK
    m = seq // q
    nblk = toep.shape[0]
    return pl.pallas_call(
        functools.partial(_s5_kernel, q=q),
        grid=(nblk, batch),
        in_specs=[
            pl.BlockSpec((seq, LANES), lambda j, b: (b, u_col + j)),
            pl.BlockSpec((1, q * LANES, q * LANES), lambda j, b: (j, 0, 0)),
            pl.BlockSpec((1, q * LANES, 2 * STATE_LANES), lambda j, b: (j, 0, 0)),
            pl.BlockSpec((1, q * LANES, 2 * STATE_LANES), lambda j, b: (j, 0, 0)),
            pl.BlockSpec((1, 2, STATE_LANES), lambda j, b: (j, 0, 0)),
            pl.BlockSpec((1, LANES), lambda j, b: (0, j)),
        ],
        out_specs=pl.BlockSpec((seq, LANES), lambda j, b: (b, j)),
        out_shape=jax.ShapeDtypeStruct((batch * seq, nblk * LANES), F32),
        scratch_shapes=[
            pltpu.VMEM((m, q * LANES), BF16),
            pltpu.VMEM((m, 2 * STATE_LANES), F32),
            pltpu.VMEM((m, 2 * STATE_LANES), F32),
        ],
        compiler_params=_params("parallel", "arbitrary"),
        name="s5_core",
    )(proj, toep, win, woutt, aq, d_skip)


def _dot_nt_split(x, y):
    nt = (((1,), (1,)), ((), ()))
    xh = x.astype(BF16)
    xl = (x - xh.astype(F32)).astype(BF16)
    yh = y.astype(BF16)
    yl = (y - yh.astype(F32)).astype(BF16)
    out = lax.dot_general(xh, yh, nt, preferred_element_type=F32)
    out = out + lax.dot_general(xh, yl, nt, preferred_element_type=F32)
    return out + lax.dot_general(xl, yh, nt, preferred_element_type=F32)


def _s5_tables_kernel(are_ref, aim_ref, ls_ref, bre_ref, bim_ref, cre_ref, cim_ref,
                      toep_ref, win_ref, woutt_ref, aq_ref, wf_ref, *, q):
    sl = STATE_LANES
    lam_re = jnp.minimum(are_ref[0], -1e-4)
    lam_im = aim_ref[0]
    step = jnp.exp(ls_ref[0])
    mag = jnp.exp(lam_re * step)
    a_re = mag * jnp.cos(lam_im * step)
    a_im = mag * jnp.sin(lam_im * step)
    den = lam_re * lam_re + lam_im * lam_im
    f_re = ((a_re - 1.0) * lam_re + a_im * lam_im) / den
    f_im = (a_im * lam_re - (a_re - 1.0) * lam_im) / den
    b_re = bre_ref[0]
    b_im = bim_ref[0]
    bb_re = f_re * b_re - f_im * b_im
    bb_im = f_re * b_im + f_im * b_re
    c_re = cre_ref[0]
    c_im = cim_ref[0]

    pows = [(jnp.ones((1, sl), F32), jnp.zeros((1, sl), F32))]
    for _ in range(q):
        p_re, p_im = pows[-1]
        pows.append((p_re * a_re - p_im * a_im, p_re * a_im + p_im * a_re))

    row_group = lax.shift_right_logical(lax.broadcasted_iota(jnp.int32, (LANES, sl), 0), S5_GROUP_SHIFT)
    col_group = lax.shift_right_logical(lax.broadcasted_iota(jnp.int32, (LANES, sl), 1), S5_STATE_SHIFT)
    same_group = row_group == col_group

    def tile(k, m_re, m_im, im_sign):
        p_re, p_im = pows[k]
        v_re = p_re * m_re - p_im * m_im
        v_im = (p_re * m_im + p_im * m_re) * im_sign
        e_re = jnp.where(same_group, jnp.concatenate([v_re] * GROUPS_PER_BLOCK, axis=0), 0.0)
        e_im = jnp.where(same_group, jnp.concatenate([v_im] * GROUPS_PER_BLOCK, axis=0), 0.0)
        return jnp.concatenate([e_re, e_im], axis=1)

    for s in range(q):
        rows = slice(s * LANES, (s + 1) * LANES)
        wf_ref[rows, :] = tile(q - 1 - s, bb_re, bb_im, 1.0)
        woutt_ref[0, rows, :] = tile(s + 1, c_re, c_im, -1.0).astype(BF16)
    win_ref[0] = wf_ref[...].astype(BF16)
    taps = _dot_nt_split(wf_ref[...], tile(0, c_re, c_im, -1.0))
    toep_ref[0] = jnp.zeros(toep_ref.shape[1:], BF16)
    for s in range(q):
        for t in range(s, q):
            lag_rows = slice((q - 1 - (t - s)) * LANES, (q - (t - s)) * LANES)
            toep_ref[0, s * LANES:(s + 1) * LANES, t * LANES:(t + 1) * LANES] = taps[lag_rows, :].astype(BF16)
    aq_ref[0, 0:1, :] = pows[q][0]
    aq_ref[0, 1:2, :] = pows[q][1]


def s5_tables(a_re, a_im, b_re, b_im, c_re, c_im, log_step):
    q = S5_CHUNK
    g, p = a_re.shape
    gb = GROUPS_PER_BLOCK
    nblk = g // gb
    sl = gb * p

    def rows(t):
        return t.astype(F32).reshape(nblk, 1, sl)

    def b_mat(t):
        return t.astype(F32).reshape(nblk, gb, p, S5_GROUP).transpose(0, 3, 1, 2).reshape(nblk, S5_GROUP, sl)

    def c_mat(t):
        return t.astype(F32).reshape(nblk, gb, S5_GROUP, p).transpose(0, 2, 1, 3).reshape(nblk, S5_GROUP, sl)

    row_spec = pl.BlockSpec((1, 1, sl), lambda j: (j, 0, 0))
    mat_spec = pl.BlockSpec((1, S5_GROUP, sl), lambda j: (j, 0, 0))
    return pl.pallas_call(
        functools.partial(_s5_tables_kernel, q=q),
        grid=(nblk,),
        in_specs=[row_spec, row_spec, row_spec, mat_spec, mat_spec, mat_spec, mat_spec],
        out_specs=[
            pl.BlockSpec((1, q * LANES, q * LANES), lambda j: (j, 0, 0)),
            pl.BlockSpec((1, q * LANES, 2 * sl), lambda j: (j, 0, 0)),
            pl.BlockSpec((1, q * LANES, 2 * sl), lambda j: (j, 0, 0)),
            pl.BlockSpec((1, 2, sl), lambda j: (j, 0, 0)),
        ],
        out_shape=[
            jax.ShapeDtypeStruct((nblk, q * LANES, q * LANES), BF16),
            jax.ShapeDtypeStruct((nblk, q * LANES, 2 * sl), BF16),
            jax.ShapeDtypeStruct((nblk, q * LANES, 2 * sl), BF16),
            jax.ShapeDtypeStruct((nblk, 2, sl), F32),
        ],
        scratch_shapes=[pltpu.VMEM((q * LANES, 2 * sl), F32)],
        compiler_params=_params("parallel"),
        name="s5_tables",
    )(rows(a_re), rows(a_im), rows(log_step), b_mat(b_re), b_mat(b_im), c_mat(c_re), c_mat(c_im))


def _even_mixer(h, g, w_in, b_forget, tables, batch, seq):
    fw = FOX_HEADS * HEAD_DIM
    rw = RET_HEADS * HEAD_DIM
    scale = HEAD_DIM ** -0.5
    c0 = 3 * fw
    c1 = c0 + FOX_HEADS
    w_main = jnp.concatenate([
        w_in[:, :fw] * scale, w_in[:, fw:c0],
        w_in[:, c1:c1 + rw], w_in[:, c1 + rw:c1 + 2 * rw] * scale, w_in[:, c1 + 2 * rw:],
    ], axis=1).astype(BF16)
    w_forget = jnp.pad(w_in[:, c0:c1], ((0, 0), (0, LANES - FOX_HEADS))).astype(BF16)
    bias = jnp.pad(b_forget.astype(F32), (0, LANES - FOX_HEADS)).reshape(1, LANES)

    t = h.shape[0]
    proj = norm_matmul(h, g, w_main, BF16, tm=_row_tile(t, PROJ_ROWS), tn=PROJ_COLS)
    fl = norm_matmul(h, g, w_forget, F32, tm=_row_tile(t, PROJ_ROWS), tn=LANES)
    call = fox_cumsum(fl, bias, batch, seq)
    nh = FOX_HEADS
    fox = fox_attention(proj, call, batch, seq, 0, nh, 2 * nh, blk=_row_tile(seq, FOX_BLOCK),
                        hp=FOX_HEADS_PER_STEP)
    cos_t, sin_t, dm, aux = tables
    ret = retention(proj, cos_t, sin_t, dm, aux, batch, seq, 3 * nh, 4 * nh, 5 * nh, 6 * nh,
                    rows=_row_tile(seq, RET_ROWS), hp=RET_HEADS_PER_STEP)
    return fox, ret


def _odd_mixer(h, g, w_in, a_re, a_im, b_re, b_im, c_re, c_im, d_skip, log_step, w_glu, conv_w,
               batch, seq):
    sw = a_re.shape[0] * S5_GROUP
    t = h.shape[0]
    proj = norm_matmul(h, g, w_in.astype(BF16), F32, tm=_row_tile(t, PROJ_ROWS), tn=PROJ_COLS)
    toep, win, woutt, aq = s5_tables(a_re, a_im, b_re, b_im, c_re, c_im, log_step)
    y = s5_core(proj, toep, win, woutt, aq, d_skip.reshape(1, sw).astype(F32), batch, seq, 0)
    return y, proj, conv_w.astype(F32), w_glu.astype(BF16)


def kernel(x, mem, norm_mix, norm_xattn, norm_mlp, norm_mem, norm_final, ab_w_in, ab_b_forget, ab_w_out,
           cd_w_in, s5_a_re, s5_a_im, s5_b_re, s5_b_im, s5_c_re, s5_c_im, s5_d, s5_log_step, s5_w_glu,
           conv_w, cd_w_out, xa_wq, xa_wkv, xa_wo, mlp_w1, mlp_w2):
    batch, seq, d = x.shape
    depth = norm_mix.shape[0]
    mlen = mem.shape[1]
    xw = xa_wq.shape[2]
    h = x.reshape(batch * seq, d)
    memf = mem.reshape(batch * mlen, d)
    tables = _retention_tables(seq)
    xa_scale = HEAD_DIM ** -0.5
    w_kv = jnp.transpose(xa_wkv, (1, 0, 2)).reshape(d, depth * 2 * xw).astype(BF16)
    kv_all = norm_matmul(memf, norm_mem, w_kv, BF16, tm=_row_tile(batch * mlen, PROJ_ROWS), tn=PROJ_COLS)
    kv_all = kv_all.reshape(batch, mlen, depth * 2 * xw)
    for layer in range(depth):
        odd = layer % 2 == 1
        if odd:
            o = layer // 2
            mixer_out = _odd_mixer(h, norm_mix[layer], cd_w_in[o], s5_a_re[o], s5_a_im[o], s5_b_re[o],
                                   s5_b_im[o], s5_c_re[o], s5_c_im[o], s5_d[o], s5_log_step[o],
                                   s5_w_glu[o], conv_w[o], batch, seq)
            w_o = cd_w_out[o].astype(BF16)
        else:
            e = layer // 2
            mixer_out = _even_mixer(h, norm_mix[layer], ab_w_in[e], ab_b_forget[e], tables, batch, seq)
            w_o = ab_w_out[e].astype(BF16)
        k1 = w_o.shape[0] // 2
        h = mix_xattn_block(h, mixer_out, w_o[:k1], w_o[k1:], norm_xattn[layer],
                            (xa_wq[layer] * xa_scale).astype(BF16), kv_all, 2 * layer,
                            xa_wo[layer].astype(BF16), seq, tm=_row_tile(seq, MIX_ROWS), odd=odd)
        h = mlp_block(h, norm_mlp[layer], mlp_w1[layer].astype(BF16), mlp_w2[layer].astype(BF16),
                      norm_final, layer == depth - 1, tm=_row_tile(batch * seq, MLP_ROWS), tf=MLP_COLS)
    return h.reshape(batch, seq, d)
```

```python
import functools
import math

import jax
import jax.numpy as jnp
from jax import lax
from jax.experimental import pallas as pl
from jax.experimental.pallas import tpu as pltpu

F32 = jnp.float32
BF16 = jnp.bfloat16

EPS = 1e-6
ROPE_BASE = 10000.0
LANES = 128
HEAD_DIM = 128
FOX_HEADS = 8
RET_HEADS = 8
RET_CHUNK = 128
XA_HEADS = 4
S5_GROUP = 16
S5_STATE = 64
S5_CHUNK = 16
CONV_K = 3
S5_GROUP_SHIFT = S5_GROUP.bit_length() - 1
S5_STATE_SHIFT = S5_STATE.bit_length() - 1
GROUPS_PER_BLOCK = LANES // S5_GROUP
STATE_LANES = GROUPS_PER_BLOCK * S5_STATE
VMEM_LIMIT = 56 * 1024 * 1024
NEG_BIG = -1e30
LOG2E = math.log2(math.e)

PROJ_ROWS, PROJ_COLS = 1024, 1024
MIX_ROWS = 512
MLP_ROWS, MLP_COLS = 1024, 512
FOX_BLOCK = 512
FOX_HEADS_PER_STEP = 4
RET_ROWS = 1024
RET_HEADS_PER_STEP = 8


def _row_tile(rows, want):
    tile = min(rows, want)
    while rows % tile:
        tile //= 2
    return tile


def _params(*sem):
    return pltpu.CompilerParams(dimension_semantics=sem, vmem_limit_bytes=VMEM_LIMIT)


def _rms(x, g):
    ms = jnp.mean(x * x, axis=-1, keepdims=True)
    return x * lax.rsqrt(ms + EPS) * g


def _norm_matmul_kernel(*refs, side):
    if side:
        x_ref, g_ref, w_ref, ws_ref, o_ref, os_ref, xn_ref = refs
    else:
        x_ref, g_ref, w_ref, o_ref, xn_ref = refs

    @pl.when(pl.program_id(1) == 0)
    def _():
        xn_ref[...] = _rms(x_ref[...], g_ref[...]).astype(BF16)
        if side:
            os_ref[...] = jnp.dot(xn_ref[...], ws_ref[...], preferred_element_type=F32)

    o_ref[...] = jnp.dot(xn_ref[...], w_ref[...], preferred_element_type=F32).astype(o_ref.dtype)


def norm_matmul(x, g, w, out_dtype, tm, tn, w_side=None):
    t, d = x.shape
    n = w.shape[1]
    side = w_side is not None
    in_specs = [
        pl.BlockSpec((tm, d), lambda i, j: (i, 0)),
        pl.BlockSpec((1, d), lambda i, j: (0, 0)),
        pl.BlockSpec((d, tn), lambda i, j: (0, j)),
    ]
    out_specs = pl.BlockSpec((tm, tn), lambda i, j: (i, j))
    out_shape = jax.ShapeDtypeStruct((t, n), out_dtype)
    args = (x, g.reshape(1, d), w)
    if side:
        ns = w_side.shape[1]
        in_specs.append(pl.BlockSpec((d, ns), lambda i, j: (0, 0)))
        out_specs = [out_specs, pl.BlockSpec((tm, ns), lambda i, j: (i, 0))]
        out_shape = [out_shape, jax.ShapeDtypeStruct((t, ns), F32)]
        args = args + (w_side,)
    return pl.pallas_call(
        functools.partial(_norm_matmul_kernel, side=side),
        grid=(t // tm, n // tn),
        in_specs=in_specs,
        out_specs=out_specs,
        out_shape=out_shape,
        scratch_shapes=[pltpu.VMEM((tm, d), BF16)],
        compiler_params=_params("parallel", "arbitrary"),
        name="norm_matmul",
    )(*args)


def _mlp_kernel(h_ref, g_ref, w1_ref, w2_ref, gf_ref, o_ref, xn_ref, *, final_norm):
    f = pl.program_id(1)

    @pl.when(f == 0)
    def _():
        x = h_ref[...]
        xn_ref[...] = _rms(x, g_ref[...]).astype(BF16)
        o_ref[...] = x

    a = jnp.dot(xn_ref[...], w1_ref[...], preferred_element_type=F32)
    a = jnp.square(jnp.maximum(a, 0.0)).astype(BF16)
    o_ref[...] += jnp.dot(a, w2_ref[...], preferred_element_type=F32)

    if final_norm:
        @pl.when(f == pl.num_programs(1) - 1)
        def _():
            o_ref[...] = _rms(o_ref[...], gf_ref[...])


def mlp_block(h, g, w1, w2, g_final, final_norm, tm, tf):
    t, d = h.shape
    dff = w1.shape[1]
    return pl.pallas_call(
        functools.partial(_mlp_kernel, final_norm=final_norm),
        grid=(t // tm, dff // tf),
        in_specs=[
            pl.BlockSpec((tm, d), lambda i, f: (i, 0)),
            pl.BlockSpec((1, d), lambda i, f: (0, 0)),
            pl.BlockSpec((d, tf), lambda i, f: (0, f)),
            pl.BlockSpec((tf, d), lambda i, f: (f, 0)),
            pl.BlockSpec((1, d), lambda i, f: (0, 0)),
        ],
        out_specs=pl.BlockSpec((tm, d), lambda i, f: (i, 0)),
        out_shape=jax.ShapeDtypeStruct((t, d), F32),
        scratch_shapes=[pltpu.VMEM((tm, d), BF16)],
        compiler_params=_params("parallel", "arbitrary"),
        name="mlp_block",
    )(h, g.reshape(1, d), w1, w2, g_final.reshape(1, d))


def _s5_gate(y, w_glu):
    g = 0.5 * y * (1.0 + jnp.tanh(math.sqrt(2.0 / math.pi) * (y + 0.044715 * (y * y * y))))
    z = jnp.dot(g.astype(BF16), w_glu, preferred_element_type=F32)
    return (g * jax.nn.sigmoid(z)).astype(BF16)


def _short_conv(hc, gb, gc, z_prev, w_ref):
    z = gc * hc
    row = lax.broadcasted_iota(jnp.int32, z.shape, 0)
    y = w_ref[CONV_K - 1:CONV_K, :] * z
    for lag in range(1, CONV_K):
        zl = pltpu.roll(z, lag, 0)
        for r in range(lag):
            zl = jnp.where(row == r, z_prev[8 - lag + r:8 - lag + r + 1, :], zl)
        y = y + w_ref[CONV_K - 1 - lag:CONV_K - lag, :] * zl
    return (gb * y).astype(BF16)


def _mix_xattn_kernel(*refs, odd, per_batch):
    if odd:
        (h_ref, y_ref, hc_ref, gb_ref, gc_ref, hcp_ref, gcp_ref, cw_ref, wglu_ref,
         w1_ref, w2_ref, g_ref, wq_ref, k_ref, v_ref, wo_ref, o_ref) = refs
        a1 = _s5_gate(y_ref[...], wglu_ref[...])
        first = pl.program_id(0) % per_batch == 0
        z_prev = jnp.where(first, 0.0, gcp_ref[...] * hcp_ref[...])
        a2 = _short_conv(hc_ref[...], gb_ref[...], gc_ref[...], z_prev, cw_ref)
    else:
        h_ref, a1_ref, a2_ref, w1_ref, w2_ref, g_ref, wq_ref, k_ref, v_ref, wo_ref, o_ref = refs
        a1 = a1_ref[...]
        a2 = a2_ref[...]
    x = h_ref[...] + jnp.dot(a1, w1_ref[...], preferred_element_type=F32)
    x = x + jnp.dot(a2, w2_ref[...], preferred_element_type=F32)
    xn = _rms(x, g_ref[...]).astype(BF16)
    q = jnp.dot(xn, wq_ref[...], preferred_element_type=F32).astype(BF16)
    heads = []
    for hd in range(XA_HEADS):
        sl = slice(hd * HEAD_DIM, (hd + 1) * HEAD_DIM)
        s = lax.dot_general(q[:, sl], k_ref[0, :, sl], (((1,), (1,)), ((), ())),
                            preferred_element_type=F32)
        m = jnp.max(s, axis=-1, keepdims=True)
        p = jnp.exp(s - m)
        l = jnp.sum(p, axis=-1, keepdims=True)
        oh = jnp.dot(p.astype(BF16), v_ref[0, :, sl], preferred_element_type=F32)
        heads.append((oh * (1.0 / l)).astype(BF16))
    o = jnp.concatenate(heads, axis=-1)
    o_ref[...] = x + jnp.dot(o, wo_ref[...], preferred_element_type=F32)


def mix_xattn_block(h, mixer_inputs, w1, w2, g, wq, kv, kv_col, wo, seq, tm, odd):
    t, d = h.shape
    k1, k2 = w1.shape[0], w2.shape[0]
    mlen = kv.shape[1]
    xw = wq.shape[1]
    per_batch = seq // tm

    def const(shape):
        return pl.BlockSpec(shape, lambda i: (0,) * len(shape), pipeline_mode=pl.Buffered(1))

    def rows(width, col=0):
        return pl.BlockSpec((tm, width), lambda i: (i, col))

    if odd:
        y, proj, conv_w, w_glu = mixer_inputs
        halo = tm // 8

        def prev_rows(col):
            return pl.BlockSpec((8, k2), lambda i: (jnp.maximum(i * halo - 1, 0), col))

        mixer_args = (y, proj, proj, proj, proj, proj, conv_w, w_glu)
        mixer_specs = [rows(k1), rows(k2, 1), rows(k2, 2), rows(k2, 3), prev_rows(1), prev_rows(3),
                       const(conv_w.shape), const(w_glu.shape)]
    else:
        mixer_args = mixer_inputs
        mixer_specs = [rows(k1), rows(k2)]

    return pl.pallas_call(
        functools.partial(_mix_xattn_kernel, odd=odd, per_batch=per_batch),
        grid=(t // tm,),
        in_specs=[rows(d)] + mixer_specs + [
            const((k1, d)), const((k2, d)), const((1, d)), const((d, xw)),
            pl.BlockSpec((1, mlen, xw), lambda i: (i // per_batch, 0, kv_col)),
            pl.BlockSpec((1, mlen, xw), lambda i: (i // per_batch, 0, kv_col + 1)),
            const((xw, d)),
        ],
        out_specs=rows(d),
        out_shape=jax.ShapeDtypeStruct((t, d), F32),
        compiler_params=_params("parallel"),
        name="mix_xattn_block",
    )(h, *mixer_args, w1, w2, g.reshape(1, d), wq, kv, kv, wo)


def _split3(x):
    hi = x.astype(BF16)
    r1 = x - hi.astype(F32)
    mid = r1.astype(BF16)
    lo = (r1 - mid.astype(F32)).astype(BF16)
    return hi, mid, lo


def _fox_cumsum_kernel(fl_ref, b_ref, call_ref, *, blk):
    seq = fl_ref.shape[0]
    row = lax.broadcasted_iota(jnp.int32, (blk, blk), 0)
    col = lax.broadcasted_iota(jnp.int32, (blk, blk), 1)
    tri = jnp.where(row >= col, 1.0, 0.0).astype(BF16)
    carry = jnp.zeros((1, LANES), F32)
    for i in range(seq // blk):
        rows = slice(i * blk, (i + 1) * blk)
        x = fl_ref[rows, :] + b_ref[...]
        ls = (jnp.minimum(x, 0.0) - jnp.log(1.0 + jnp.exp(-jnp.abs(x)))) * LOG2E
        c = carry
        for piece in _split3(ls):
            c = c + jnp.dot(tri, piece, preferred_element_type=F32)
        call_ref[rows, :] = c
        carry = c[blk - 1:blk, :]


def fox_cumsum(fl, bias, batch, seq):
    blk = min(256, seq)
    return pl.pallas_call(
        functools.partial(_fox_cumsum_kernel, blk=blk),
        grid=(batch,),
        in_specs=[
            pl.BlockSpec((seq, LANES), lambda b: (b, 0)),
            pl.BlockSpec((1, LANES), lambda b: (0, 0)),
        ],
        out_specs=pl.BlockSpec((seq, LANES), lambda b: (b, 0)),
        out_shape=jax.ShapeDtypeStruct((batch * seq, LANES), F32),
        compiler_params=_params("parallel"),
        name="fox_cumsum",
    )(fl, bias)


def _lane_tile(x, reps):
    return jnp.concatenate([x] * reps, axis=1)


def _gate_lanes(c_col, lane, own_first):
    hi, mid, lo = [p.astype(F32) for p in _split3(c_col)]
    base = 0 if own_first else 3
    pieces = jnp.where(lane == base, hi, jnp.where(lane == base + 1, mid, jnp.where(lane == base + 2, lo, 0.0)))
    ones = jnp.where((lane >= 3 - base) & (lane < 6 - base), 1.0, 0.0)
    return (pieces + ones).astype(BF16)


def _fox_attn_kernel(q_ref, k_ref, v_ref, c_ref, o_ref, qaug_ref, kaug_ref, vaug_ref, m_ref, acc_ref,
                     *, blk, hp):
    g = pl.program_id(1)
    i = pl.program_id(2)
    seq = k_ref.shape[0]
    nt = (((1,), (1,)), ((), ()))

    def head_cols(t):
        return slice(t * HEAD_DIM, (t + 1) * HEAD_DIM)

    @pl.when(i == 0)
    def _():
        lane = lax.broadcasted_iota(jnp.int32, (seq, LANES), 1)
        for t in range(hp):
            c_key = jnp.sum(jnp.where(lane == g * hp + t, c_ref[...], 0.0), axis=1, keepdims=True)
            kaug_ref[t, :, :HEAD_DIM] = k_ref[:, head_cols(t)]
            kaug_ref[t, :, HEAD_DIM:] = _gate_lanes(-c_key, lane, own_first=False)
            vaug_ref[t, :, :HEAD_DIM] = v_ref[:, head_cols(t)]
            vaug_ref[t, :, HEAD_DIM:] = jnp.ones((seq, LANES), BF16)

    lane = lax.broadcasted_iota(jnp.int32, (blk, LANES), 1)
    c_rows = c_ref[pl.ds(pl.multiple_of(i * blk, blk), blk), :]
    for t in range(hp):
        c_query = jnp.sum(jnp.where(lane == g * hp + t, c_rows, 0.0), axis=1, keepdims=True)
        qaug_ref[t, :, :HEAD_DIM] = q_ref[:, head_cols(t)]
        qaug_ref[t, :, HEAD_DIM:] = _gate_lanes(c_query, lane, own_first=True)
    m_ref[...] = jnp.full(m_ref.shape, NEG_BIG, F32)
    acc_ref[...] = jnp.zeros(acc_ref.shape, F32)

    def step(j, masked):
        start = pl.multiple_of(j * blk, blk)
        scores = [lax.dot_general(qaug_ref[t], kaug_ref[t, pl.ds(start, blk), :], nt,
                                  preferred_element_type=F32) for t in range(hp)]
        for t in range(hp):
            s = scores[t]
            if masked:
                r = lax.broadcasted_iota(jnp.int32, (blk, blk), 0)
                c = lax.broadcasted_iota(jnp.int32, (blk, blk), 1)
                s = jnp.where(r >= c, s, -jnp.inf)
            m_prev = m_ref[t]
            m_new = jnp.maximum(m_prev, jnp.max(s, axis=-1, keepdims=True))
            alpha = jnp.exp2(m_prev - m_new)
            p = jnp.exp2(s - _lane_tile(m_new, blk // LANES)).astype(BF16)
            acc_ref[t] = (_lane_tile(alpha, acc_ref.shape[2] // LANES) * acc_ref[t]
                          + jnp.dot(p, vaug_ref[t, pl.ds(start, blk), :], preferred_element_type=F32))
            m_ref[t] = m_new

    def body(j, carry):
        step(j, False)
        return carry

    lax.fori_loop(0, i, body, 0)
    step(i, True)
    for t in range(hp):
        acc = acc_ref[t]
        o_ref[:, head_cols(t)] = (acc[:, :HEAD_DIM] * (1.0 / acc[:, HEAD_DIM:])).astype(o_ref.dtype)


def fox_attention(proj, call, batch, seq, q_col, k_col, v_col, blk, hp):
    nq = seq // blk
    width = hp * HEAD_DIM
    aug = HEAD_DIM + LANES
    return pl.pallas_call(
        functools.partial(_fox_attn_kernel, blk=blk, hp=hp),
        grid=(batch, FOX_HEADS // hp, nq),
        in_specs=[
            pl.BlockSpec((blk, width), lambda b, g, i: (b * nq + i, q_col // hp + g)),
            pl.BlockSpec((seq, width), lambda b, g, i: (b, k_col // hp + g)),
            pl.BlockSpec((seq, width), lambda b, g, i: (b, v_col // hp + g)),
            pl.BlockSpec((seq, LANES), lambda b, g, i: (b, 0)),
        ],
        out_specs=pl.BlockSpec((blk, width), lambda b, g, i: (b * nq + i, g)),
        out_shape=jax.ShapeDtypeStruct((batch * seq, FOX_HEADS * HEAD_DIM), BF16),
        scratch_shapes=[
            pltpu.VMEM((hp, blk, aug), BF16),
            pltpu.VMEM((hp, seq, aug), BF16),
            pltpu.VMEM((hp, seq, aug), BF16),
            pltpu.VMEM((hp, blk, LANES), F32),
            pltpu.VMEM((hp, blk, aug), F32),
        ],
        compiler_params=_params("parallel", "parallel", "arbitrary"),
        name="fox_attention",
    )(proj, proj, proj, call)


def _retention_kernel(q_ref, k_ref, v_ref, g_ref, cos_ref, sin_ref, dm_ref, aux_ref, o_ref, r_ref,
                      *, cs, hp):
    rows_per_step = q_ref.shape[0]
    half = HEAD_DIM // 2
    nt = (((1,), (1,)), ((), ()))

    @pl.when(pl.program_id(2) == 0)
    def _():
        r_ref[...] = jnp.zeros(r_ref.shape, F32)

    def body(c, carry):
        rows = pl.ds(pl.multiple_of(c * cs, cs), cs)
        cos = cos_ref[rows, :]
        sin = sin_ref[rows, :]
        for t in range(hp):
            cols = slice(t * HEAD_DIM, (t + 1) * HEAD_DIM)
            q = q_ref[rows, cols].astype(F32)
            k = k_ref[rows, cols].astype(F32)
            v = v_ref[rows, cols]
            qr = q * cos + pltpu.roll(q, half, 1) * sin
            kr = k * cos + pltpu.roll(k, half, 1) * sin
            qb = qr.astype(BF16)
            s = lax.dot_general(qb, kr.astype(BF16), nt, preferred_element_type=F32) * dm_ref[t]
            o = jnp.dot(s.astype(BF16), v, preferred_element_type=F32)
            r = r_ref[t]
            o = o + jnp.dot(qb, r.astype(BF16), preferred_element_type=F32) * aux_ref[t, 0]
            kz = (kr * aux_ref[t, 1]).astype(BF16)
            r_ref[t] = aux_ref[t, 2] * r + lax.dot_general(kz, v, (((0,), (0,)), ((), ())),
                                                          preferred_element_type=F32)
            mu = jnp.mean(o, axis=-1, keepdims=True)
            oc = o - mu
            var = jnp.mean(oc * oc, axis=-1, keepdims=True)
            on = oc * lax.rsqrt(var + EPS)
            g = g_ref[rows, cols].astype(F32)
            o_ref[rows, cols] = (g * jax.nn.sigmoid(g) * on).astype(o_ref.dtype)
        return carry

    lax.fori_loop(0, rows_per_step // cs, body, 0)


def retention(proj, cos, sin, dm, aux, batch, seq, q_col, k_col, v_col, g_col, rows, hp):
    width = hp * HEAD_DIM
    nr = seq // rows

    def col(c0):
        return pl.BlockSpec((rows, width), lambda b, g, i: (b * nr + i, c0 // hp + g))

    return pl.pallas_call(
        functools.partial(_retention_kernel, cs=RET_CHUNK, hp=hp),
        grid=(batch, RET_HEADS // hp, nr),
        in_specs=[
            col(q_col), col(k_col), col(v_col), col(g_col),
            pl.BlockSpec((rows, HEAD_DIM), lambda b, g, i: (i, 0)),
            pl.BlockSpec((rows, HEAD_DIM), lambda b, g, i: (i, 0)),
            pl.BlockSpec((hp, RET_CHUNK, RET_CHUNK), lambda b, g, i: (g, 0, 0)),
            pl.BlockSpec((hp, 3, RET_CHUNK, LANES), lambda b, g, i: (g, 0, 0, 0)),
        ],
        out_specs=pl.BlockSpec((rows, width), lambda b, g, i: (b * nr + i, g)),
        out_shape=jax.ShapeDtypeStruct((batch * seq, RET_HEADS * HEAD_DIM), BF16),
        scratch_shapes=[pltpu.VMEM((hp, HEAD_DIM, HEAD_DIM), F32)],
        compiler_params=_params("parallel", "parallel", "arbitrary"),
        name="retention",
    )(proj, proj, proj, proj, cos, sin, dm, aux)


def _retention_tables(seq):
    half = HEAD_DIM // 2
    inv = ROPE_BASE ** (-jnp.arange(half, dtype=F32) / half)
    ang = jnp.arange(seq, dtype=F32)[:, None] * inv[None, :]
    cos, sin = jnp.cos(ang), jnp.sin(ang)
    cos_t = jnp.concatenate([cos, cos], axis=-1)
    sin_t = jnp.concatenate([-sin, sin], axis=-1)
    cs = RET_CHUNK
    log_g = jnp.log1p(-jnp.exp2(-5.0 - jnp.arange(RET_HEADS, dtype=F32)))
    pos = jnp.arange(cs, dtype=F32)
    diff = pos[:, None] - pos[None, :]
    dm = jnp.where(diff >= 0, jnp.exp(log_g[:, None, None] * jnp.maximum(diff, 0.0)), 0.0)
    zeta = jnp.exp(log_g[:, None] * (cs - 1 - pos)[None, :])
    xi = jnp.exp(log_g[:, None] * (pos + 1)[None, :])
    g_chunk = jnp.broadcast_to(jnp.exp(log_g * cs)[:, None], (RET_HEADS, cs))
    aux = jnp.broadcast_to(jnp.stack([xi, zeta, g_chunk], axis=1)[..., None], (RET_HEADS, 3, cs, LANES))
    return cos_t, sin_t, dm, aux


def _s5_kernel(u_ref, are_ref, aim_ref, ls_ref, bre_ref, bim_ref, cre_ref, cim_ref, d_ref, y_ref,
               toep_ref, win_ref, woutt_ref, aq_ref, wf_ref, ucat_ref, inc_ref, x_ref, *, q):
    m = u_ref.shape[0] // q
    sl = STATE_LANES

    @pl.when(pl.program_id(1) == 0)
    def _():
        _s5_build_tables(are_ref, aim_ref, ls_ref, bre_ref, bim_ref, cre_ref, cim_ref,
                         toep_ref, win_ref, woutt_ref, aq_ref, wf_ref, q=q)

    for s in range(q):
        ucat_ref[:, s * LANES:(s + 1) * LANES] = u_ref[pl.ds(s, m, stride=q), :].astype(BF16)
    ucat = ucat_ref[...]
    inc_ref[...] = jnp.dot(ucat, win_ref[...], preferred_element_type=F32)
    a_re = aq_ref[0:1, :]
    a_im = aq_ref[1:2, :]

    def body(n, carry):
        x_re, x_im = carry
        x_ref[pl.ds(n, 1), 0:sl] = x_re
        x_ref[pl.ds(n, 1), sl:2 * sl] = x_im
        i_re = inc_ref[pl.ds(n, 1), 0:sl]
        i_im = inc_ref[pl.ds(n, 1), sl:2 * sl]
        return (a_re * x_re - a_im * x_im + i_re, a_re * x_im + a_im * x_re + i_im)

    zero = jnp.zeros((1, sl), F32)
    lax.fori_loop(0, m, body, (zero, zero))
    y = jnp.dot(ucat, toep_ref[...], preferred_element_type=F32)
    y = y + lax.dot_general(x_ref[...].astype(BF16), woutt_ref[...], (((1,), (1,)), ((), ())),
                            preferred_element_type=F32)
    for t in range(q):
        rows = pl.ds(t, m, stride=q)
        y_ref[rows, :] = y[:, t * LANES:(t + 1) * LANES] + d_ref[...] * u_ref[rows, :]


def s5_core(proj, a_re, a_im, b_re, b_im, c_re, c_im, log_step, d_skip, batch, seq, u_col):
    q = S5_CHUNK
    m = seq // q
    g, p = a_re.shape
    gb = GROUPS_PER_BLOCK
    nblk = g // gb
    sl = gb * p

    def rows(t):
        return t.astype(F32).reshape(nblk, 1, sl)

    def b_mat(t):
        return t.astype(F32).reshape(nblk, gb, p, S5_GROUP).transpose(0, 3, 1, 2).reshape(nblk, S5_GROUP, sl)

    def c_mat(t):
        return t.astype(F32).reshape(nblk, gb, S5_GROUP, p).transpose(0, 2, 1, 3).reshape(nblk, S5_GROUP, sl)

    row_spec = pl.BlockSpec((1, 1, sl), lambda j, b: (j, 0, 0))
    mat_spec = pl.BlockSpec((1, S5_GROUP, sl), lambda j, b: (j, 0, 0))
    return pl.pallas_call(
        functools.partial(_s5_kernel, q=q),
        grid=(nblk, batch),
        in_specs=[
            pl.BlockSpec((seq, LANES), lambda j, b: (b, u_col + j)),
            row_spec, row_spec, row_spec, mat_spec, mat_spec, mat_spec, mat_spec,
            pl.BlockSpec((1, LANES), lambda j, b: (0, j)),
        ],
        out_specs=pl.BlockSpec((seq, LANES), lambda j, b: (b, j)),
        out_shape=jax.ShapeDtypeStruct((batch * seq, nblk * LANES), F32),
        scratch_shapes=[
            pltpu.VMEM((q * LANES, q * LANES), BF16),
            pltpu.VMEM((q * LANES, 2 * sl), BF16),
            pltpu.VMEM((q * LANES, 2 * sl), BF16),
            pltpu.VMEM((2, sl), F32),
            pltpu.VMEM((q * LANES, 2 * sl), F32),
            pltpu.VMEM((m, q * LANES), BF16),
            pltpu.VMEM((m, 2 * sl), F32),
            pltpu.VMEM((m, 2 * sl), F32),
        ],
        compiler_params=_params("parallel", "arbitrary"),
        name="s5_core",
    )(proj, rows(a_re), rows(a_im), rows(log_step), b_mat(b_re), b_mat(b_im), c_mat(c_re), c_mat(c_im),
      d_skip)


def _dot_nt_split(x, y):
    nt = (((1,), (1,)), ((), ()))
    xh = x.astype(BF16)
    xl = (x - xh.astype(F32)).astype(BF16)
    yh = y.astype(BF16)
    yl = (y - yh.astype(F32)).astype(BF16)
    out = lax.dot_general(xh, yh, nt, preferred_element_type=F32)
    out = out + lax.dot_general(xh, yl, nt, preferred_element_type=F32)
    return out + lax.dot_general(xl, yh, nt, preferred_element_type=F32)


def _s5_build_tables(are_ref, aim_ref, ls_ref, bre_ref, bim_ref, cre_ref, cim_ref,
                     toep_ref, win_ref, woutt_ref, aq_ref, wf_ref, *, q):
    sl = STATE_LANES
    lam_re = jnp.minimum(are_ref[0], -1e-4)
    lam_im = aim_ref[0]
    step = jnp.exp(ls_ref[0])
    mag = jnp.exp(lam_re * step)
    a_re = mag * jnp.cos(lam_im * step)
    a_im = mag * jnp.sin(lam_im * step)
    den = lam_re * lam_re + lam_im * lam_im
    f_re = ((a_re - 1.0) * lam_re + a_im * lam_im) / den
    f_im = (a_im * lam_re - (a_re - 1.0) * lam_im) / den
    b_re = bre_ref[0]
    b_im = bim_ref[0]
    bb_re = f_re * b_re - f_im * b_im
    bb_im = f_re * b_im + f_im * b_re
    c_re = cre_ref[0]
    c_im = cim_ref[0]

    pows = [(jnp.ones((1, sl), F32), jnp.zeros((1, sl), F32))]
    for _ in range(q):
        p_re, p_im = pows[-1]
        pows.append((p_re * a_re - p_im * a_im, p_re * a_im + p_im * a_re))

    row_group = lax.shift_right_logical(lax.broadcasted_iota(jnp.int32, (LANES, sl), 0), S5_GROUP_SHIFT)
    col_group = lax.shift_right_logical(lax.broadcasted_iota(jnp.int32, (LANES, sl), 1), S5_STATE_SHIFT)
    same_group = row_group == col_group

    def tile(k, m_re, m_im, im_sign):
        p_re, p_im = pows[k]
        v_re = p_re * m_re - p_im * m_im
        v_im = (p_re * m_im + p_im * m_re) * im_sign
        e_re = jnp.where(same_group, jnp.concatenate([v_re] * GROUPS_PER_BLOCK, axis=0), 0.0)
        e_im = jnp.where(same_group, jnp.concatenate([v_im] * GROUPS_PER_BLOCK, axis=0), 0.0)
        return jnp.concatenate([e_re, e_im], axis=1)

    for s in range(q):
        rows = slice(s * LANES, (s + 1) * LANES)
        wf_ref[rows, :] = tile(q - 1 - s, bb_re, bb_im, 1.0)
        woutt_ref[rows, :] = tile(s + 1, c_re, c_im, -1.0).astype(BF16)
    win_ref[...] = wf_ref[...].astype(BF16)
    taps = _dot_nt_split(wf_ref[...], tile(0, c_re, c_im, -1.0))
    toep_ref[...] = jnp.zeros(toep_ref.shape, BF16)
    for s in range(q):
        for t in range(s, q):
            lag_rows = slice((q - 1 - (t - s)) * LANES, (q - (t - s)) * LANES)
            toep_ref[s * LANES:(s + 1) * LANES, t * LANES:(t + 1) * LANES] = taps[lag_rows, :].astype(BF16)
    aq_ref[0:1, :] = pows[q][0]
    aq_ref[1:2, :] = pows[q][1]


def _even_mixer(h, g, w_in, b_forget, tables, batch, seq):
    fw = FOX_HEADS * HEAD_DIM
    rw = RET_HEADS * HEAD_DIM
    scale = HEAD_DIM ** -0.5
    c0 = 3 * fw
    c1 = c0 + FOX_HEADS
    w_main = jnp.concatenate([
        w_in[:, :fw] * (scale * LOG2E), w_in[:, fw:c0],
        w_in[:, c1:c1 + rw], w_in[:, c1 + rw:c1 + 2 * rw] * scale, w_in[:, c1 + 2 * rw:],
    ], axis=1).astype(BF16)
    w_forget = jnp.pad(w_in[:, c0:c1], ((0, 0), (0, LANES - FOX_HEADS))).astype(BF16)
    bias = jnp.pad(b_forget.astype(F32), (0, LANES - FOX_HEADS)).reshape(1, LANES)

    t = h.shape[0]
    proj, fl = norm_matmul(h, g, w_main, BF16, tm=_row_tile(t, PROJ_ROWS), tn=PROJ_COLS, w_side=w_forget)
    call = fox_cumsum(fl, bias, batch, seq)
    nh = FOX_HEADS
    fox = fox_attention(proj, call, batch, seq, 0, nh, 2 * nh, blk=_row_tile(seq, FOX_BLOCK),
                        hp=FOX_HEADS_PER_STEP)
    cos_t, sin_t, dm, aux = tables
    ret = retention(proj, cos_t, sin_t, dm, aux, batch, seq, 3 * nh, 4 * nh, 5 * nh, 6 * nh,
                    rows=_row_tile(seq, RET_ROWS), hp=RET_HEADS_PER_STEP)
    return fox, ret


def _odd_mixer(h, g, w_in, a_re, a_im, b_re, b_im, c_re, c_im, d_skip, log_step, w_glu, conv_w,
               batch, seq):
    sw = a_re.shape[0] * S5_GROUP
    t = h.shape[0]
    proj = norm_matmul(h, g, w_in.astype(BF16), F32, tm=_row_tile(t, PROJ_ROWS), tn=PROJ_COLS)
    y = s5_core(proj, a_re, a_im, b_re, b_im, c_re, c_im, log_step, d_skip.reshape(1, sw).astype(F32),
                batch, seq, 0)
    return y, proj, conv_w.astype(F32), w_glu.astype(BF16)


def kernel(x, mem, norm_mix, norm_xattn, norm_mlp, norm_mem, norm_final, ab_w_in, ab_b_forget, ab_w_out,
           cd_w_in, s5_a_re, s5_a_im, s5_b_re, s5_b_im, s5_c_re, s5_c_im, s5_d, s5_log_step, s5_w_glu,
           conv_w, cd_w_out, xa_wq, xa_wkv, xa_wo, mlp_w1, mlp_w2):
    batch, seq, d = x.shape
    depth = norm_mix.shape[0]
    mlen = mem.shape[1]
    xw = xa_wq.shape[2]
    h = x.reshape(batch * seq, d)
    memf = mem.reshape(batch * mlen, d)
    tables = _retention_tables(seq)
    xa_scale = HEAD_DIM ** -0.5
    w_kv = jnp.transpose(xa_wkv, (1, 0, 2)).reshape(d, depth * 2 * xw).astype(BF16)
    kv_all = norm_matmul(memf, norm_mem, w_kv, BF16, tm=_row_tile(batch * mlen, PROJ_ROWS), tn=PROJ_COLS)
    kv_all = kv_all.reshape(batch, mlen, depth * 2 * xw)
    for layer in range(depth):
        odd = layer % 2 == 1
        if odd:
            o = layer // 2
            mixer_out = _odd_mixer(h, norm_mix[layer], cd_w_in[o], s5_a_re[o], s5_a_im[o], s5_b_re[o],
                                   s5_b_im[o], s5_c_re[o], s5_c_im[o], s5_d[o], s5_log_step[o],
                                   s5_w_glu[o], conv_w[o], batch, seq)
            w_o = cd_w_out[o].astype(BF16)
        else:
            e = layer // 2
            mixer_out = _even_mixer(h, norm_mix[layer], ab_w_in[e], ab_b_forget[e], tables, batch, seq)
            w_o = ab_w_out[e].astype(BF16)
        k1 = w_o.shape[0] // 2
        h = mix_xattn_block(h, mixer_out, w_o[:k1], w_o[k1:], norm_xattn[layer],
                            (xa_wq[layer] * xa_scale).astype(BF16), kv_all, 2 * layer,
                            xa_wo[layer].astype(BF16), seq, tm=_row_tile(seq, MIX_ROWS), odd=odd)
        h = mlp_block(h, norm_mlp[layer], mlp_w1[layer].astype(BF16), mlp_w2[layer].astype(BF16),
                      norm_final, layer == depth - 1, tm=_row_tile(batch * seq, MLP_ROWS), tf=MLP_COLS)
    return h.reshape(batch, seq, d)
```

```python
import functools
import math

import jax
import jax.numpy as jnp
from jax import lax
from jax.experimental import pallas as pl
from jax.experimental.pallas import tpu as pltpu

F32 = jnp.float32
BF16 = jnp.bfloat16

EPS = 1e-6
ROPE_BASE = 10000.0
LANES = 128
HEAD_DIM = 128
FOX_HEADS = 8
RET_HEADS = 8
RET_CHUNK = 256
XA_HEADS = 4
S5_GROUP = 16
S5_STATE = 64
S5_CHUNK = 16
S5_OUT_STEPS = 2
CONV_K = 3
S5_GROUP_SHIFT = S5_GROUP.bit_length() - 1
S5_STATE_SHIFT = S5_STATE.bit_length() - 1
GROUPS_PER_BLOCK = LANES // S5_GROUP
STATE_LANES = GROUPS_PER_BLOCK * S5_STATE
VMEM_LIMIT = 56 * 1024 * 1024
NEG_BIG = -1e30
LOG2E = math.log2(math.e)

PROJ_ROWS, PROJ_COLS = 1024, 1024
MIX_ROWS = 512
MLP_ROWS, MLP_COLS = 1024, 512
FOX_BLOCK = 1024
FOX_HEADS_PER_STEP = 2
RET_ROWS = 1024
RET_HEADS_PER_STEP = 8


def _row_tile(rows, want):
    tile = min(rows, want)
    while rows % tile:
        tile //= 2
    return tile


def _params(*sem):
    return pltpu.CompilerParams(dimension_semantics=sem, vmem_limit_bytes=VMEM_LIMIT)


def _rms(x, g):
    ms = jnp.mean(x * x, axis=-1, keepdims=True)
    return x * lax.rsqrt(ms + EPS) * g


def _norm_matmul_kernel(*refs, side):
    if side:
        x_ref, g_ref, w_ref, ws_ref, o_ref, os_ref, xn_ref = refs
    else:
        x_ref, g_ref, w_ref, o_ref, xn_ref = refs

    @pl.when(pl.program_id(1) == 0)
    def _():
        xn_ref[...] = _rms(x_ref[...], g_ref[...]).astype(BF16)
        if side:
            os_ref[...] = jnp.dot(xn_ref[...], ws_ref[...], preferred_element_type=F32)

    o_ref[...] = jnp.dot(xn_ref[...], w_ref[...], preferred_element_type=F32).astype(o_ref.dtype)


def norm_matmul(x, g, w, out_dtype, tm, tn, w_side=None):
    t, d = x.shape
    n = w.shape[1]
    side = w_side is not None
    in_specs = [
        pl.BlockSpec((tm, d), lambda i, j: (i, 0)),
        pl.BlockSpec((1, d), lambda i, j: (0, 0)),
        pl.BlockSpec((d, tn), lambda i, j: (0, j)),
    ]
    out_specs = pl.BlockSpec((tm, tn), lambda i, j: (i, j))
    out_shape = jax.ShapeDtypeStruct((t, n), out_dtype)
    args = (x, g.reshape(1, d), w)
    if side:
        ns = w_side.shape[1]
        in_specs.append(pl.BlockSpec((d, ns), lambda i, j: (0, 0)))
        out_specs = [out_specs, pl.BlockSpec((tm, ns), lambda i, j: (i, 0))]
        out_shape = [out_shape, jax.ShapeDtypeStruct((t, ns), F32)]
        args = args + (w_side,)
    return pl.pallas_call(
        functools.partial(_norm_matmul_kernel, side=side),
        grid=(t // tm, n // tn),
        in_specs=in_specs,
        out_specs=out_specs,
        out_shape=out_shape,
        scratch_shapes=[pltpu.VMEM((tm, d), BF16)],
        compiler_params=_params("parallel", "arbitrary"),
        name="norm_matmul",
    )(*args)


def _mlp_kernel(h_ref, g_ref, w1_ref, w2_ref, gf_ref, o_ref, xn_ref, *, final_norm):
    f = pl.program_id(1)

    @pl.when(f == 0)
    def _():
        x = h_ref[...]
        xn_ref[...] = _rms(x, g_ref[...]).astype(BF16)
        o_ref[...] = x

    a = jnp.dot(xn_ref[...], w1_ref[...], preferred_element_type=F32)
    a = jnp.square(jnp.maximum(a, 0.0)).astype(BF16)
    o_ref[...] += jnp.dot(a, w2_ref[...], preferred_element_type=F32)

    if final_norm:
        @pl.when(f == pl.num_programs(1) - 1)
        def _():
            o_ref[...] = _rms(o_ref[...], gf_ref[...])


def mlp_block(h, g, w1, w2, g_final, final_norm, tm, tf):
    t, d = h.shape
    dff = w1.shape[1]
    return pl.pallas_call(
        functools.partial(_mlp_kernel, final_norm=final_norm),
        grid=(t // tm, dff // tf),
        in_specs=[
            pl.BlockSpec((tm, d), lambda i, f: (i, 0)),
            pl.BlockSpec((1, d), lambda i, f: (0, 0)),
            pl.BlockSpec((d, tf), lambda i, f: (0, f)),
            pl.BlockSpec((tf, d), lambda i, f: (f, 0)),
            pl.BlockSpec((1, d), lambda i, f: (0, 0)),
        ],
        out_specs=pl.BlockSpec((tm, d), lambda i, f: (i, 0)),
        out_shape=jax.ShapeDtypeStruct((t, d), F32),
        scratch_shapes=[pltpu.VMEM((tm, d), BF16)],
        compiler_params=_params("parallel", "arbitrary"),
        name="mlp_block",
    )(h, g.reshape(1, d), w1, w2, g_final.reshape(1, d))


def _s5_gate(y, w_glu):
    g = 0.5 * y * (1.0 + jnp.tanh(math.sqrt(2.0 / math.pi) * (y + 0.044715 * (y * y * y))))
    z = jnp.dot(g.astype(BF16), w_glu, preferred_element_type=F32)
    return (g * jax.nn.sigmoid(z)).astype(BF16)


def _short_conv(hc, gb, gc, z_prev, w_ref):
    z = gc * hc
    row = lax.broadcasted_iota(jnp.int32, z.shape, 0)
    y = w_ref[CONV_K - 1:CONV_K, :] * z
    for lag in range(1, CONV_K):
        zl = pltpu.roll(z, lag, 0)
        for r in range(lag):
            zl = jnp.where(row == r, z_prev[8 - lag + r:8 - lag + r + 1, :], zl)
        y = y + w_ref[CONV_K - 1 - lag:CONV_K - lag, :] * zl
    return (gb * y).astype(BF16)


def _mix_xattn_kernel(*refs, odd, per_batch):
    if odd:
        (h_ref, y_ref, hc_ref, gb_ref, gc_ref, hcp_ref, gcp_ref, cw_ref, wglu_ref,
         w1_ref, w2_ref, g_ref, wq_ref, k_ref, v_ref, wo_ref, o_ref) = refs
        a1 = _s5_gate(y_ref[...], wglu_ref[...])
        first = pl.program_id(0) % per_batch == 0
        z_prev = jnp.where(first, 0.0, gcp_ref[...] * hcp_ref[...])
        a2 = _short_conv(hc_ref[...], gb_ref[...], gc_ref[...], z_prev, cw_ref)
    else:
        h_ref, a1_ref, a2_ref, w1_ref, w2_ref, g_ref, wq_ref, k_ref, v_ref, wo_ref, o_ref = refs
        a1 = a1_ref[...]
        a2 = a2_ref[...]
    x = h_ref[...] + jnp.dot(a1, w1_ref[...], preferred_element_type=F32)
    x = x + jnp.dot(a2, w2_ref[...], preferred_element_type=F32)
    xn = _rms(x, g_ref[...]).astype(BF16)
    q = jnp.dot(xn, wq_ref[...], preferred_element_type=F32).astype(BF16)
    heads = []
    for hd in range(XA_HEADS):
        sl = slice(hd * HEAD_DIM, (hd + 1) * HEAD_DIM)
        s = lax.dot_general(q[:, sl], k_ref[0, :, sl], (((1,), (1,)), ((), ())),
                            preferred_element_type=F32)
        m = jnp.max(s, axis=-1, keepdims=True)
        p = jnp.exp(s - m)
        l = jnp.sum(p, axis=-1, keepdims=True)
        oh = jnp.dot(p.astype(BF16), v_ref[0, :, sl], preferred_element_type=F32)
        heads.append((oh * (1.0 / l)).astype(BF16))
    o = jnp.concatenate(heads, axis=-1)
    o_ref[...] = x + jnp.dot(o, wo_ref[...], preferred_element_type=F32)


def mix_xattn_block(h, mixer_inputs, w1, w2, g, wq, kv, kv_col, wo, seq, tm, odd):
    t, d = h.shape
    k1, k2 = w1.shape[0], w2.shape[0]
    mlen = kv.shape[1]
    xw = wq.shape[1]
    per_batch = seq // tm

    def const(shape):
        return pl.BlockSpec(shape, lambda i: (0,) * len(shape), pipeline_mode=pl.Buffered(1))

    def rows(width, col=0):
        return pl.BlockSpec((tm, width), lambda i: (i, col))

    if odd:
        y, proj, conv_w, w_glu = mixer_inputs
        halo = tm // 8

        def prev_rows(col):
            return pl.BlockSpec((8, k2), lambda i: (jnp.maximum(i * halo - 1, 0), col))

        mixer_args = (y, proj, proj, proj, proj, proj, conv_w, w_glu)
        mixer_specs = [rows(k1), rows(k2, 1), rows(k2, 2), rows(k2, 3), prev_rows(1), prev_rows(3),
                       const(conv_w.shape), const(w_glu.shape)]
    else:
        mixer_args = mixer_inputs
        mixer_specs = [rows(k1), rows(k2)]

    return pl.pallas_call(
        functools.partial(_mix_xattn_kernel, odd=odd, per_batch=per_batch),
        grid=(t // tm,),
        in_specs=[rows(d)] + mixer_specs + [
            const((k1, d)), const((k2, d)), const((1, d)), const((d, xw)),
            pl.BlockSpec((1, mlen, xw), lambda i: (i // per_batch, 0, kv_col)),
            pl.BlockSpec((1, mlen, xw), lambda i: (i // per_batch, 0, kv_col + 1)),
            const((xw, d)),
        ],
        out_specs=rows(d),
        out_shape=jax.ShapeDtypeStruct((t, d), F32),
        compiler_params=_params("parallel"),
        name="mix_xattn_block",
    )(h, *mixer_args, w1, w2, g.reshape(1, d), wq, kv, kv, wo)


def _split3(x):
    hi = x.astype(BF16)
    r1 = x - hi.astype(F32)
    mid = r1.astype(BF16)
    lo = (r1 - mid.astype(F32)).astype(BF16)
    return hi, mid, lo


def _fox_cumsum_kernel(fl_ref, b_ref, call_ref, *, blk):
    seq = fl_ref.shape[0]
    row = lax.broadcasted_iota(jnp.int32, (blk, blk), 0)
    col = lax.broadcasted_iota(jnp.int32, (blk, blk), 1)
    tri = jnp.where(row >= col, 1.0, 0.0).astype(BF16)
    carry = jnp.zeros((1, LANES), F32)
    for i in range(seq // blk):
        rows = slice(i * blk, (i + 1) * blk)
        x = fl_ref[rows, :] + b_ref[...]
        ls = (jnp.minimum(x, 0.0) - jnp.log(1.0 + jnp.exp(-jnp.abs(x)))) * LOG2E
        c = carry
        for piece in _split3(ls):
            c = c + jnp.dot(tri, piece, preferred_element_type=F32)
        call_ref[rows, :] = c
        carry = c[blk - 1:blk, :]


def fox_cumsum(fl, bias, batch, seq):
    blk = min(256, seq)
    return pl.pallas_call(
        functools.partial(_fox_cumsum_kernel, blk=blk),
        grid=(batch,),
        in_specs=[
            pl.BlockSpec((seq, LANES), lambda b: (b, 0)),
            pl.BlockSpec((1, LANES), lambda b: (0, 0)),
        ],
        out_specs=pl.BlockSpec((seq, LANES), lambda b: (b, 0)),
        out_shape=jax.ShapeDtypeStruct((batch * seq, LANES), F32),
        compiler_params=_params("parallel"),
        name="fox_cumsum",
    )(fl, bias)


def _lane_tile(x, reps):
    return jnp.concatenate([x] * reps, axis=1)


def _gate_lanes(c_col, lane, own_first):
    hi, mid, lo = [p.astype(F32) for p in _split3(c_col)]
    base = 0 if own_first else 3
    pieces = jnp.where(lane == base, hi, jnp.where(lane == base + 1, mid, jnp.where(lane == base + 2, lo, 0.0)))
    ones = jnp.where((lane >= 3 - base) & (lane < 6 - base), 1.0, 0.0)
    return (pieces + ones).astype(BF16)


def _fox_attn_kernel(q_ref, k_ref, v_ref, c_ref, o_ref, qaug_ref, kaug_ref, vaug_ref, m_ref, acc_ref,
                     *, blk, hp):
    g = pl.program_id(1)
    i = pl.program_id(2)
    seq = k_ref.shape[0]
    nt = (((1,), (1,)), ((), ()))

    def head_cols(t):
        return slice(t * HEAD_DIM, (t + 1) * HEAD_DIM)

    @pl.when(i == 0)
    def _():
        lane = lax.broadcasted_iota(jnp.int32, (seq, LANES), 1)
        for t in range(hp):
            c_key = jnp.sum(jnp.where(lane == g * hp + t, c_ref[...], 0.0), axis=1, keepdims=True)
            kaug_ref[t, :, :HEAD_DIM] = k_ref[:, head_cols(t)]
            kaug_ref[t, :, HEAD_DIM:] = _gate_lanes(-c_key, lane, own_first=False)
            vaug_ref[t, :, :HEAD_DIM] = v_ref[:, head_cols(t)]
            vaug_ref[t, :, HEAD_DIM:] = jnp.ones((seq, LANES), BF16)

    lane = lax.broadcasted_iota(jnp.int32, (blk, LANES), 1)
    c_rows = c_ref[pl.ds(pl.multiple_of(i * blk, blk), blk), :]
    for t in range(hp):
        c_query = jnp.sum(jnp.where(lane == g * hp + t, c_rows, 0.0), axis=1, keepdims=True)
        qaug_ref[t, :, :HEAD_DIM] = q_ref[:, head_cols(t)]
        qaug_ref[t, :, HEAD_DIM:] = _gate_lanes(c_query, lane, own_first=True)
    m_ref[...] = jnp.full(m_ref.shape, NEG_BIG, F32)
    acc_ref[...] = jnp.zeros(acc_ref.shape, F32)

    def step(j, masked):
        start = pl.multiple_of(j * blk, blk)
        scores = [lax.dot_general(qaug_ref[t], kaug_ref[t, pl.ds(start, blk), :], nt,
                                  preferred_element_type=F32) for t in range(hp)]
        for t in range(hp):
            s = scores[t]
            if masked:
                r = lax.broadcasted_iota(jnp.int32, (blk, blk), 0)
                c = lax.broadcasted_iota(jnp.int32, (blk, blk), 1)
                s = jnp.where(r >= c, s, -jnp.inf)
            m_prev = m_ref[t]
            m_new = jnp.maximum(m_prev, jnp.max(s, axis=-1, keepdims=True))
            alpha = jnp.exp2(m_prev - m_new)
            p = jnp.exp2(s - _lane_tile(m_new, blk // LANES)).astype(BF16)
            acc_ref[t] = (_lane_tile(alpha, acc_ref.shape[2] // LANES) * acc_ref[t]
                          + jnp.dot(p, vaug_ref[t, pl.ds(start, blk), :], preferred_element_type=F32))
            m_ref[t] = m_new

    def body(j, carry):
        step(j, False)
        return carry

    lax.fori_loop(0, i, body, 0)
    step(i, True)
    for t in range(hp):
        acc = acc_ref[t]
        o_ref[:, head_cols(t)] = (acc[:, :HEAD_DIM] * (1.0 / acc[:, HEAD_DIM:])).astype(o_ref.dtype)


def fox_attention(proj, call, batch, seq, q_col, k_col, v_col, blk, hp):
    nq = seq // blk
    width = hp * HEAD_DIM
    aug = HEAD_DIM + LANES
    return pl.pallas_call(
        functools.partial(_fox_attn_kernel, blk=blk, hp=hp),
        grid=(batch, FOX_HEADS // hp, nq),
        in_specs=[
            pl.BlockSpec((blk, width), lambda b, g, i: (b * nq + i, q_col // hp + g)),
            pl.BlockSpec((seq, width), lambda b, g, i: (b, k_col // hp + g)),
            pl.BlockSpec((seq, width), lambda b, g, i: (b, v_col // hp + g)),
            pl.BlockSpec((seq, LANES), lambda b, g, i: (b, 0)),
        ],
        out_specs=pl.BlockSpec((blk, width), lambda b, g, i: (b * nq + i, g)),
        out_shape=jax.ShapeDtypeStruct((batch * seq, FOX_HEADS * HEAD_DIM), BF16),
        scratch_shapes=[
            pltpu.VMEM((hp, blk, aug), BF16),
            pltpu.VMEM((hp, seq, aug), BF16),
            pltpu.VMEM((hp, seq, aug), BF16),
            pltpu.VMEM((hp, blk, LANES), F32),
            pltpu.VMEM((hp, blk, aug), F32),
        ],
        compiler_params=_params("parallel", "parallel", "arbitrary"),
        name="fox_attention",
    )(proj, proj, proj, call)


def _retention_kernel(q_ref, k_ref, v_ref, g_ref, cos_ref, sin_ref, dm_ref, aux_ref, o_ref,
                      r_ref, qr_ref, qx_ref, kr_ref, kz_ref, of_ref, *, cs, hp):
    rows_per_step = q_ref.shape[0]
    n_chunks = rows_per_step // cs
    half = HEAD_DIM // 2
    nt = (((1,), (1,)), ((), ()))

    @pl.when(pl.program_id(2) == 0)
    def _():
        r_ref[...] = jnp.zeros(r_ref.shape, F32)

    def head_cols(t):
        return slice(t * HEAD_DIM, (t + 1) * HEAD_DIM)

    cos = cos_ref[...]
    sin = sin_ref[...]
    for t in range(hp):
        q = q_ref[:, head_cols(t)].astype(F32)
        k = k_ref[:, head_cols(t)].astype(F32)
        qr = q * cos + pltpu.roll(q, half, 1) * sin
        kr = k * cos + pltpu.roll(k, half, 1) * sin
        xi = jnp.concatenate([aux_ref[t, 0]] * n_chunks, axis=0)
        zeta = jnp.concatenate([aux_ref[t, 1]] * n_chunks, axis=0)
        qr_ref[:, head_cols(t)] = qr.astype(BF16)
        qx_ref[:, head_cols(t)] = (qr * xi).astype(BF16)
        kr_ref[:, head_cols(t)] = kr.astype(BF16)
        kz_ref[:, head_cols(t)] = (kr * zeta).astype(BF16)

    def body(c, carry):
        rows = pl.ds(pl.multiple_of(c * cs, cs), cs)
        for t in range(hp):
            v = v_ref[rows, head_cols(t)]
            s = lax.dot_general(qr_ref[rows, head_cols(t)], kr_ref[rows, head_cols(t)], nt,
                                preferred_element_type=F32) * dm_ref[t]
            r = r_ref[t]
            lhs = jnp.concatenate([s.astype(BF16), qx_ref[rows, head_cols(t)]], axis=1)
            rhs = jnp.concatenate([v, r.astype(BF16)], axis=0)
            of_ref[rows, head_cols(t)] = jnp.dot(lhs, rhs, preferred_element_type=F32)
            r_ref[t] = aux_ref[t, 2, :HEAD_DIM] * r + lax.dot_general(
                kz_ref[rows, head_cols(t)], v, (((0,), (0,)), ((), ())), preferred_element_type=F32)
        return carry

    lax.fori_loop(0, n_chunks, body, 0)

    for t in range(hp):
        o = of_ref[:, head_cols(t)]
        mu = jnp.mean(o, axis=-1, keepdims=True)
        oc = o - mu
        var = jnp.mean(oc * oc, axis=-1, keepdims=True)
        on = oc * lax.rsqrt(var + EPS)
        g = g_ref[:, head_cols(t)].astype(F32)
        o_ref[:, head_cols(t)] = (g * jax.nn.sigmoid(g) * on).astype(o_ref.dtype)


def retention(proj, cos, sin, dm, aux, batch, seq, q_col, k_col, v_col, g_col, rows, hp):
    width = hp * HEAD_DIM
    nr = seq // rows

    def col(c0):
        return pl.BlockSpec((rows, width), lambda b, g, i: (b * nr + i, c0 // hp + g))

    return pl.pallas_call(
        functools.partial(_retention_kernel, cs=RET_CHUNK, hp=hp),
        grid=(batch, RET_HEADS // hp, nr),
        in_specs=[
            col(q_col), col(k_col), col(v_col), col(g_col),
            pl.BlockSpec((rows, HEAD_DIM), lambda b, g, i: (i, 0)),
            pl.BlockSpec((rows, HEAD_DIM), lambda b, g, i: (i, 0)),
            pl.BlockSpec((hp, RET_CHUNK, RET_CHUNK), lambda b, g, i: (g, 0, 0)),
            pl.BlockSpec((hp, 3, RET_CHUNK, LANES), lambda b, g, i: (g, 0, 0, 0)),
        ],
        out_specs=pl.BlockSpec((rows, width), lambda b, g, i: (b * nr + i, g)),
        out_shape=jax.ShapeDtypeStruct((batch * seq, RET_HEADS * HEAD_DIM), BF16),
        scratch_shapes=[
            pltpu.VMEM((hp, HEAD_DIM, HEAD_DIM), F32),
            pltpu.VMEM((rows, width), BF16),
            pltpu.VMEM((rows, width), BF16),
            pltpu.VMEM((rows, width), BF16),
            pltpu.VMEM((rows, width), BF16),
            pltpu.VMEM((rows, width), F32),
        ],
        compiler_params=_params("parallel", "parallel", "arbitrary"),
        name="retention",
    )(proj, proj, proj, proj, cos, sin, dm, aux)


def _retention_tables(seq):
    half = HEAD_DIM // 2
    inv = ROPE_BASE ** (-jnp.arange(half, dtype=F32) / half)
    ang = jnp.arange(seq, dtype=F32)[:, None] * inv[None, :]
    cos, sin = jnp.cos(ang), jnp.sin(ang)
    cos_t = jnp.concatenate([cos, cos], axis=-1)
    sin_t = jnp.concatenate([-sin, sin], axis=-1)
    cs = RET_CHUNK
    log_g = jnp.log1p(-jnp.exp2(-5.0 - jnp.arange(RET_HEADS, dtype=F32)))
    pos = jnp.arange(cs, dtype=F32)
    diff = pos[:, None] - pos[None, :]
    dm = jnp.where(diff >= 0, jnp.exp(log_g[:, None, None] * jnp.maximum(diff, 0.0)), 0.0)
    zeta = jnp.exp(log_g[:, None] * (cs - 1 - pos)[None, :])
    xi = jnp.exp(log_g[:, None] * (pos + 1)[None, :])
    g_chunk = jnp.broadcast_to(jnp.exp(log_g * cs)[:, None], (RET_HEADS, cs))
    aux = jnp.broadcast_to(jnp.stack([xi, zeta, g_chunk], axis=1)[..., None], (RET_HEADS, 3, cs, LANES))
    return cos_t, sin_t, dm, aux


def _s5_kernel(u_ref, are_ref, aim_ref, ls_ref, bre_ref, bim_ref, cre_ref, cim_ref, d_ref, y_ref,
               toep_ref, win_ref, woutt_ref, aq_ref, wf_ref, ucat_ref, inc_ref, x_ref, *, q):
    m = u_ref.shape[0] // q
    sl = STATE_LANES

    @pl.when(pl.program_id(1) == 0)
    def _():
        _s5_build_tables(are_ref, aim_ref, ls_ref, bre_ref, bim_ref, cre_ref, cim_ref,
                         toep_ref, win_ref, woutt_ref, aq_ref, wf_ref, q=q)

    for s in range(q):
        ucat_ref[:, s * LANES:(s + 1) * LANES] = u_ref[pl.ds(s, m, stride=q), :].astype(BF16)
    ucat = ucat_ref[...]
    inc_ref[...] = jnp.dot(ucat, win_ref[...], preferred_element_type=F32)
    a_re = aq_ref[0:1, :]
    a_im = aq_ref[1:2, :]

    def body(n, carry):
        x_re, x_im = carry
        x_ref[pl.ds(n, 1), 0:sl] = x_re
        x_ref[pl.ds(n, 1), sl:2 * sl] = x_im
        i_re = inc_ref[pl.ds(n, 1), 0:sl]
        i_im = inc_ref[pl.ds(n, 1), sl:2 * sl]
        return (a_re * x_re - a_im * x_im + i_re, a_re * x_im + a_im * x_re + i_im)

    zero = jnp.zeros((1, sl), F32)
    lax.fori_loop(0, m, body, (zero, zero))
    xb = x_ref[...].astype(BF16)
    nt = (((1,), (1,)), ((), ()))
    for t0 in range(0, q, S5_OUT_STEPS):
        hi = (t0 + S5_OUT_STEPS) * LANES
        cols = slice(t0 * LANES, hi)
        y = jnp.dot(ucat_ref[:, :hi], toep_ref[:hi, cols], preferred_element_type=F32)
        y = y + lax.dot_general(xb, woutt_ref[cols, :], nt, preferred_element_type=F32)
        for t in range(t0, t0 + S5_OUT_STEPS):
            rows = pl.ds(t, m, stride=q)
            y_ref[rows, :] = (y[:, (t - t0) * LANES:(t - t0 + 1) * LANES]
                              + d_ref[...] * u_ref[rows, :])


def s5_core(proj, a_re, a_im, b_re, b_im, c_re, c_im, log_step, d_skip, batch, seq, u_col):
    q = S5_CHUNK
    m = seq // q
    g, p = a_re.shape
    gb = GROUPS_PER_BLOCK
    nblk = g // gb
    sl = gb * p

    def rows(t):
        return t.astype(F32).reshape(nblk, 1, sl)

    def b_mat(t):
        return t.astype(F32).reshape(nblk, gb, p, S5_GROUP).transpose(0, 3, 1, 2).reshape(nblk, S5_GROUP, sl)

    def c_mat(t):
        return t.astype(F32).reshape(nblk, gb, S5_GROUP, p).transpose(0, 2, 1, 3).reshape(nblk, S5_GROUP, sl)

    row_spec = pl.BlockSpec((1, 1, sl), lambda j, b: (j, 0, 0))
    mat_spec = pl.BlockSpec((1, S5_GROUP, sl), lambda j, b: (j, 0, 0))
    return pl.pallas_call(
        functools.partial(_s5_kernel, q=q),
        grid=(nblk, batch),
        in_specs=[
            pl.BlockSpec((seq, LANES), lambda j, b: (b, u_col + j)),
            row_spec, row_spec, row_spec, mat_spec, mat_spec, mat_spec, mat_spec,
            pl.BlockSpec((1, LANES), lambda j, b: (0, j)),
        ],
        out_specs=pl.BlockSpec((seq, LANES), lambda j, b: (b, j)),
        out_shape=jax.ShapeDtypeStruct((batch * seq, nblk * LANES), F32),
        scratch_shapes=[
            pltpu.VMEM((q * LANES, q * LANES), BF16),
            pltpu.VMEM((q * LANES, 2 * sl), BF16),
            pltpu.VMEM((q * LANES, 2 * sl), BF16),
            pltpu.VMEM((2, sl), F32),
            pltpu.VMEM((q * LANES, 2 * sl), F32),
            pltpu.VMEM((m, q * LANES), BF16),
            pltpu.VMEM((m, 2 * sl), F32),
            pltpu.VMEM((m, 2 * sl), F32),
        ],
        compiler_params=_params("parallel", "arbitrary"),
        name="s5_core",
    )(proj, rows(a_re), rows(a_im), rows(log_step), b_mat(b_re), b_mat(b_im), c_mat(c_re), c_mat(c_im),
      d_skip)


def _dot_nt_split(x, y):
    nt = (((1,), (1,)), ((), ()))
    xh = x.astype(BF16)
    xl = (x - xh.astype(F32)).astype(BF16)
    yh = y.astype(BF16)
    yl = (y - yh.astype(F32)).astype(BF16)
    out = lax.dot_general(xh, yh, nt, preferred_element_type=F32)
    out = out + lax.dot_general(xh, yl, nt, preferred_element_type=F32)
    return out + lax.dot_general(xl, yh, nt, preferred_element_type=F32)


def _s5_build_tables(are_ref, aim_ref, ls_ref, bre_ref, bim_ref, cre_ref, cim_ref,
                     toep_ref, win_ref, woutt_ref, aq_ref, wf_ref, *, q):
    sl = STATE_LANES
    lam_re = jnp.minimum(are_ref[0], -1e-4)
    lam_im = aim_ref[0]
    step = jnp.exp(ls_ref[0])
    mag = jnp.exp(lam_re * step)
    a_re = mag * jnp.cos(lam_im * step)
    a_im = mag * jnp.sin(lam_im * step)
    den = lam_re * lam_re + lam_im * lam_im
    f_re = ((a_re - 1.0) * lam_re + a_im * lam_im) / den
    f_im = (a_im * lam_re - (a_re - 1.0) * lam_im) / den
    b_re = bre_ref[0]
    b_im = bim_ref[0]
    bb_re = f_re * b_re - f_im * b_im
    bb_im = f_re * b_im + f_im * b_re
    c_re = cre_ref[0]
    c_im = cim_ref[0]

    pows = [(jnp.ones((1, sl), F32), jnp.zeros((1, sl), F32))]
    for _ in range(q):
        p_re, p_im = pows[-1]
        pows.append((p_re * a_re - p_im * a_im, p_re * a_im + p_im * a_re))

    row_group = lax.shift_right_logical(lax.broadcasted_iota(jnp.int32, (LANES, sl), 0), S5_GROUP_SHIFT)
    col_group = lax.shift_right_logical(lax.broadcasted_iota(jnp.int32, (LANES, sl), 1), S5_STATE_SHIFT)
    same_group = row_group == col_group

    def tile(k, m_re, m_im, im_sign):
        p_re, p_im = pows[k]
        v_re = p_re * m_re - p_im * m_im
        v_im = (p_re * m_im + p_im * m_re) * im_sign
        e_re = jnp.where(same_group, jnp.concatenate([v_re] * GROUPS_PER_BLOCK, axis=0), 0.0)
        e_im = jnp.where(same_group, jnp.concatenate([v_im] * GROUPS_PER_BLOCK, axis=0), 0.0)
        return jnp.concatenate([e_re, e_im], axis=1)

    for s in range(q):
        rows = slice(s * LANES, (s + 1) * LANES)
        wf_ref[rows, :] = tile(q - 1 - s, bb_re, bb_im, 1.0)
        woutt_ref[rows, :] = tile(s + 1, c_re, c_im, -1.0).astype(BF16)
    win_ref[...] = wf_ref[...].astype(BF16)
    taps = _dot_nt_split(wf_ref[...], tile(0, c_re, c_im, -1.0))
    toep_ref[...] = jnp.zeros(toep_ref.shape, BF16)
    for s in range(q):
        for t in range(s, q):
            lag_rows = slice((q - 1 - (t - s)) * LANES, (q - (t - s)) * LANES)
            toep_ref[s * LANES:(s + 1) * LANES, t * LANES:(t + 1) * LANES] = taps[lag_rows, :].astype(BF16)
    aq_ref[0:1, :] = pows[q][0]
    aq_ref[1:2, :] = pows[q][1]


def _even_mixer(h, g, w_in, b_forget, tables, batch, seq):
    fw = FOX_HEADS * HEAD_DIM
    rw = RET_HEADS * HEAD_DIM
    scale = HEAD_DIM ** -0.5
    c0 = 3 * fw
    c1 = c0 + FOX_HEADS
    w_main = jnp.concatenate([
        w_in[:, :fw] * (scale * LOG2E), w_in[:, fw:c0],
        w_in[:, c1:c1 + rw], w_in[:, c1 + rw:c1 + 2 * rw] * scale, w_in[:, c1 + 2 * rw:],
    ], axis=1).astype(BF16)
    w_forget = jnp.pad(w_in[:, c0:c1], ((0, 0), (0, LANES - FOX_HEADS))).astype(BF16)
    bias = jnp.pad(b_forget.astype(F32), (0, LANES - FOX_HEADS)).reshape(1, LANES)

    t = h.shape[0]
    proj, fl = norm_matmul(h, g, w_main, BF16, tm=_row_tile(t, PROJ_ROWS), tn=PROJ_COLS, w_side=w_forget)
    call = fox_cumsum(fl, bias, batch, seq)
    nh = FOX_HEADS
    fox = fox_attention(proj, call, batch, seq, 0, nh, 2 * nh, blk=_row_tile(seq, FOX_BLOCK),
                        hp=FOX_HEADS_PER_STEP)
    cos_t, sin_t, dm, aux = tables
    ret = retention(proj, cos_t, sin_t, dm, aux, batch, seq, 3 * nh, 4 * nh, 5 * nh, 6 * nh,
                    rows=_row_tile(seq, RET_ROWS), hp=RET_HEADS_PER_STEP)
    return fox, ret


def _odd_mixer(h, g, w_in, a_re, a_im, b_re, b_im, c_re, c_im, d_skip, log_step, w_glu, conv_w,
               batch, seq):
    sw = a_re.shape[0] * S5_GROUP
    t = h.shape[0]
    proj = norm_matmul(h, g, w_in.astype(BF16), F32, tm=_row_tile(t, PROJ_ROWS), tn=PROJ_COLS)
    y = s5_core(proj, a_re, a_im, b_re, b_im, c_re, c_im, log_step, d_skip.reshape(1, sw).astype(F32),
                batch, seq, 0)
    return y, proj, conv_w.astype(F32), w_glu.astype(BF16)


def kernel(x, mem, norm_mix, norm_xattn, norm_mlp, norm_mem, norm_final, ab_w_in, ab_b_forget, ab_w_out,
           cd_w_in, s5_a_re, s5_a_im, s5_b_re, s5_b_im, s5_c_re, s5_c_im, s5_d, s5_log_step, s5_w_glu,
           conv_w, cd_w_out, xa_wq, xa_wkv, xa_wo, mlp_w1, mlp_w2):
    batch, seq, d = x.shape
    depth = norm_mix.shape[0]
    mlen = mem.shape[1]
    xw = xa_wq.shape[2]
    h = x.reshape(batch * seq, d)
    memf = mem.reshape(batch * mlen, d)
    tables = _retention_tables(seq)
    xa_scale = HEAD_DIM ** -0.5
    w_kv = jnp.transpose(xa_wkv, (1, 0, 2)).reshape(d, depth * 2 * xw).astype(BF16)
    kv_all = norm_matmul(memf, norm_mem, w_kv, BF16, tm=_row_tile(batch * mlen, PROJ_ROWS), tn=PROJ_COLS)
    kv_all = kv_all.reshape(batch, mlen, depth * 2 * xw)
    for layer in range(depth):
        odd = layer % 2 == 1
        if odd:
            o = layer // 2
            mixer_out = _odd_mixer(h, norm_mix[layer], cd_w_in[o], s5_a_re[o], s5_a_im[o], s5_b_re[o],
                                   s5_b_im[o], s5_c_re[o], s5_c_im[o], s5_d[o], s5_log_step[o],
                                   s5_w_glu[o], conv_w[o], batch, seq)
            w_o = cd_w_out[o].astype(BF16)
        else:
            e = layer // 2
            mixer_out = _even_mixer(h, norm_mix[layer], ab_w_in[e], ab_b_forget[e], tables, batch, seq)
            w_o = ab_w_out[e].astype(BF16)
        k1 = w_o.shape[0] // 2
        h = mix_xattn_block(h, mixer_out, w_o[:k1], w_o[k1:], norm_xattn[layer],
                            (xa_wq[layer] * xa_scale).astype(BF16), kv_all, 2 * layer,
                            xa_wo[layer].astype(BF16), seq, tm=_row_tile(seq, MIX_ROWS), odd=odd)
        h = mlp_block(h, norm_mlp[layer], mlp_w1[layer].astype(BF16), mlp_w2[layer].astype(BF16),
                      norm_final, layer == depth - 1, tm=_row_tile(batch * seq, MLP_ROWS), tf=MLP_COLS)
    return h.reshape(batch, seq, d)
```

```python
import functools
import math

import jax
import jax.numpy as jnp
from jax import lax
from jax.experimental import pallas as pl
from jax.experimental.pallas import tpu as pltpu

F32 = jnp.float32
BF16 = jnp.bfloat16

EPS = 1e-6
ROPE_BASE = 10000.0
LANES = 128
HEAD_DIM = 128
FOX_HEADS = 8
RET_HEADS = 8
RET_CHUNK = 256
XA_HEADS = 4
S5_GROUP = 16
S5_STATE = 64
S5_CHUNK = 16
S5_OUT_STEPS = 2
CONV_K = 3
S5_GROUP_SHIFT = S5_GROUP.bit_length() - 1
S5_STATE_SHIFT = S5_STATE.bit_length() - 1
GROUPS_PER_BLOCK = LANES // S5_GROUP
STATE_LANES = GROUPS_PER_BLOCK * S5_STATE
VMEM_LIMIT = 56 * 1024 * 1024
NEG_BIG = -1e30
LOG2E = math.log2(math.e)

PROJ_ROWS, PROJ_COLS = 1024, 1024
MIX_ROWS = 512
MLP_ROWS, MLP_COLS = 1024, 512
FOX_BLOCK = 1024
FOX_HEADS_PER_STEP = 2
RET_ROWS = 1024
RET_HEADS_PER_STEP = 8


def _row_tile(rows, want):
    tile = min(rows, want)
    while rows % tile:
        tile //= 2
    return tile


def _params(*sem):
    return pltpu.CompilerParams(dimension_semantics=sem, vmem_limit_bytes=VMEM_LIMIT)


def _rms(x, g):
    ms = jnp.mean(x * x, axis=-1, keepdims=True)
    return x * lax.rsqrt(ms + EPS) * g


def _norm_matmul_kernel(*refs, side):
    if side:
        x_ref, g_ref, w_ref, ws_ref, o_ref, os_ref, xn_ref = refs
    else:
        x_ref, g_ref, w_ref, o_ref, xn_ref = refs

    @pl.when(pl.program_id(1) == 0)
    def _():
        xn_ref[...] = _rms(x_ref[...], g_ref[...]).astype(BF16)
        if side:
            os_ref[...] = jnp.dot(xn_ref[...], ws_ref[...], preferred_element_type=F32)

    o_ref[...] = jnp.dot(xn_ref[...], w_ref[...], preferred_element_type=F32).astype(o_ref.dtype)


def norm_matmul(x, g, w, layer, n_layers, out_dtype, tm, tn, w_side=None):
    t, d = x.shape
    n = w.shape[2]
    nj = n // tn
    side = w_side is not None
    in_specs = [
        pl.BlockSpec((tm, d), lambda i, j: (i, 0)),
        pl.BlockSpec((1, d), lambda i, j: (0, 0)),
        pl.BlockSpec((None, d, tn), lambda i, j: (layer + j // nj, 0, j % nj)),
    ]
    out_specs = pl.BlockSpec((tm, tn), lambda i, j: (i, j))
    out_shape = jax.ShapeDtypeStruct((t, n_layers * n), out_dtype)
    args = (x, g.reshape(1, d), w)
    if side:
        ns = w_side.shape[2]
        in_specs.append(pl.BlockSpec((None, d, ns), lambda i, j: (layer, 0, 0)))
        out_specs = [out_specs, pl.BlockSpec((tm, ns), lambda i, j: (i, 0))]
        out_shape = [out_shape, jax.ShapeDtypeStruct((t, ns), F32)]
        args = args + (w_side,)
    return pl.pallas_call(
        functools.partial(_norm_matmul_kernel, side=side),
        grid=(t // tm, n_layers * nj),
        in_specs=in_specs,
        out_specs=out_specs,
        out_shape=out_shape,
        scratch_shapes=[pltpu.VMEM((tm, d), BF16)],
        compiler_params=_params("parallel", "arbitrary"),
        name="norm_matmul",
    )(*args)


def _mlp_kernel(h_ref, g_ref, w1_ref, w2_ref, gf_ref, o_ref, xn_ref, *, final_norm):
    f = pl.program_id(1)

    @pl.when(f == 0)
    def _():
        x = h_ref[...]
        xn_ref[...] = _rms(x, g_ref[...]).astype(BF16)
        o_ref[...] = x

    a = jnp.dot(xn_ref[...], w1_ref[...], preferred_element_type=F32)
    a = jnp.square(jnp.maximum(a, 0.0)).astype(BF16)
    o_ref[...] += jnp.dot(a, w2_ref[...], preferred_element_type=F32)

    if final_norm:
        @pl.when(f == pl.num_programs(1) - 1)
        def _():
            o_ref[...] = _rms(o_ref[...], gf_ref[...])


def mlp_block(h, g, w1, w2, layer, g_final, final_norm, tm, tf):
    t, d = h.shape
    dff = w1.shape[2]
    return pl.pallas_call(
        functools.partial(_mlp_kernel, final_norm=final_norm),
        grid=(t // tm, dff // tf),
        in_specs=[
            pl.BlockSpec((tm, d), lambda i, f: (i, 0)),
            pl.BlockSpec((1, d), lambda i, f: (0, 0)),
            pl.BlockSpec((None, d, tf), lambda i, f: (layer, 0, f)),
            pl.BlockSpec((None, tf, d), lambda i, f: (layer, f, 0)),
            pl.BlockSpec((1, d), lambda i, f: (0, 0)),
        ],
        out_specs=pl.BlockSpec((tm, d), lambda i, f: (i, 0)),
        out_shape=jax.ShapeDtypeStruct((t, d), F32),
        scratch_shapes=[pltpu.VMEM((tm, d), BF16)],
        compiler_params=_params("parallel", "arbitrary"),
        name="mlp_block",
    )(h, g.reshape(1, d), w1, w2, g_final.reshape(1, d))


def _s5_gate(y, w_glu):
    g = 0.5 * y * (1.0 + jnp.tanh(math.sqrt(2.0 / math.pi) * (y + 0.044715 * (y * y * y))))
    z = jnp.dot(g.astype(BF16), w_glu, preferred_element_type=F32)
    return (g * jax.nn.sigmoid(z)).astype(BF16)


def _short_conv(hc, gb, gc, z_prev, w_ref):
    z = gc * hc
    row = lax.broadcasted_iota(jnp.int32, z.shape, 0)
    y = w_ref[CONV_K - 1:CONV_K, :] * z
    for lag in range(1, CONV_K):
        zl = pltpu.roll(z, lag, 0)
        for r in range(lag):
            zl = jnp.where(row == r, z_prev[8 - lag + r:8 - lag + r + 1, :], zl)
        y = y + w_ref[CONV_K - 1 - lag:CONV_K - lag, :] * zl
    return (gb * y).astype(BF16)


def _mix_xattn_kernel(*refs, odd, per_batch):
    if odd:
        (h_ref, y_ref, hc_ref, gb_ref, gc_ref, hcp_ref, gcp_ref, cw_ref, wglu_ref,
         w1_ref, w2_ref, g_ref, wq_ref, k_ref, v_ref, wo_ref, o_ref) = refs
        a1 = _s5_gate(y_ref[...], wglu_ref[...])
        first = pl.program_id(0) % per_batch == 0
        z_prev = jnp.where(first, 0.0, gcp_ref[...] * hcp_ref[...])
        a2 = _short_conv(hc_ref[...], gb_ref[...], gc_ref[...], z_prev, cw_ref)
    else:
        h_ref, a1_ref, a2_ref, w1_ref, w2_ref, g_ref, wq_ref, k_ref, v_ref, wo_ref, o_ref = refs
        a1 = a1_ref[...]
        a2 = a2_ref[...]
    x = h_ref[...] + jnp.dot(a1, w1_ref[...], preferred_element_type=F32)
    x = x + jnp.dot(a2, w2_ref[...], preferred_element_type=F32)
    xn = _rms(x, g_ref[...]).astype(BF16)
    q = jnp.dot(xn, wq_ref[...], preferred_element_type=F32).astype(BF16)
    heads = []
    for hd in range(XA_HEADS):
        sl = slice(hd * HEAD_DIM, (hd + 1) * HEAD_DIM)
        s = lax.dot_general(q[:, sl], k_ref[0, :, sl], (((1,), (1,)), ((), ())),
                            preferred_element_type=F32)
        m = jnp.max(s, axis=-1, keepdims=True)
        p = jnp.exp(s - m)
        l = jnp.sum(p, axis=-1, keepdims=True)
        oh = jnp.dot(p.astype(BF16), v_ref[0, :, sl], preferred_element_type=F32)
        heads.append((oh * (1.0 / l)).astype(BF16))
    o = jnp.concatenate(heads, axis=-1)
    o_ref[...] = x + jnp.dot(o, wo_ref[...], preferred_element_type=F32)


def mix_xattn_block(h, mixer_inputs, w_out, mixer_idx, g, wq, kv, wo, layer, seq, tm, odd):
    t, d = h.shape
    k1 = w_out.shape[1] // 2
    k2 = k1
    mlen = kv.shape[1]
    xw = wq.shape[2]
    per_batch = seq // tm

    def fixed(shape, *index):
        return pl.BlockSpec(shape, lambda i: index, pipeline_mode=pl.Buffered(1))

    def rows(width, col=0):
        return pl.BlockSpec((tm, width), lambda i: (i, col))

    if odd:
        y, proj, conv_w, w_glu = mixer_inputs
        halo = tm // 8

        def prev_rows(col):
            return pl.BlockSpec((8, k2), lambda i: (jnp.maximum(i * halo - 1, 0), col))

        mixer_args = (y, proj, proj, proj, proj, proj, conv_w, w_glu)
        mixer_specs = [rows(k1), rows(k2, 1), rows(k2, 2), rows(k2, 3), prev_rows(1), prev_rows(3),
                       fixed((None,) + conv_w.shape[1:], mixer_idx, 0, 0),
                       fixed((None,) + w_glu.shape[1:], mixer_idx, 0, 0)]
    else:
        mixer_args = mixer_inputs
        mixer_specs = [rows(k1), rows(k2)]

    return pl.pallas_call(
        functools.partial(_mix_xattn_kernel, odd=odd, per_batch=per_batch),
        grid=(t // tm,),
        in_specs=[rows(d)] + mixer_specs + [
            fixed((None, k1, d), mixer_idx, 0, 0), fixed((None, k2, d), mixer_idx, 1, 0),
            fixed((1, d), 0, 0), fixed((None, d, xw), layer, 0, 0),
            pl.BlockSpec((1, mlen, xw), lambda i: (i // per_batch, 0, 2 * layer)),
            pl.BlockSpec((1, mlen, xw), lambda i: (i // per_batch, 0, 2 * layer + 1)),
            fixed((None, xw, d), layer, 0, 0),
        ],
        out_specs=rows(d),
        out_shape=jax.ShapeDtypeStruct((t, d), F32),
        compiler_params=_params("parallel"),
        name="mix_xattn_block",
    )(h, *mixer_args, w_out, w_out, g.reshape(1, d), wq, kv, kv, wo)


def _split3(x):
    hi = x.astype(BF16)
    r1 = x - hi.astype(F32)
    mid = r1.astype(BF16)
    lo = (r1 - mid.astype(F32)).astype(BF16)
    return hi, mid, lo


def _fox_cumsum_kernel(fl_ref, b_ref, call_ref, *, blk):
    seq = fl_ref.shape[0]
    row = lax.broadcasted_iota(jnp.int32, (blk, blk), 0)
    col = lax.broadcasted_iota(jnp.int32, (blk, blk), 1)
    tri = jnp.where(row >= col, 1.0, 0.0).astype(BF16)
    carry = jnp.zeros((1, LANES), F32)
    for i in range(seq // blk):
        rows = slice(i * blk, (i + 1) * blk)
        x = fl_ref[rows, :] + b_ref[...]
        ls = (jnp.minimum(x, 0.0) - jnp.log(1.0 + jnp.exp(-jnp.abs(x)))) * LOG2E
        c = carry
        for piece in _split3(ls):
            c = c + jnp.dot(tri, piece, preferred_element_type=F32)
        call_ref[rows, :] = c
        carry = c[blk - 1:blk, :]


def fox_cumsum(fl, bias, batch, seq):
    blk = min(256, seq)
    return pl.pallas_call(
        functools.partial(_fox_cumsum_kernel, blk=blk),
        grid=(batch,),
        in_specs=[
            pl.BlockSpec((seq, LANES), lambda b: (b, 0)),
            pl.BlockSpec((1, LANES), lambda b: (0, 0)),
        ],
        out_specs=pl.BlockSpec((seq, LANES), lambda b: (b, 0)),
        out_shape=jax.ShapeDtypeStruct((batch * seq, LANES), F32),
        compiler_params=_params("parallel"),
        name="fox_cumsum",
    )(fl, bias)


def _lane_tile(x, reps):
    return jnp.concatenate([x] * reps, axis=1)


def _gate_lanes(c_col, lane, own_first):
    hi, mid, lo = [p.astype(F32) for p in _split3(c_col)]
    base = 0 if own_first else 3
    pieces = jnp.where(lane == base, hi, jnp.where(lane == base + 1, mid, jnp.where(lane == base + 2, lo, 0.0)))
    ones = jnp.where((lane >= 3 - base) & (lane < 6 - base), 1.0, 0.0)
    return (pieces + ones).astype(BF16)


def _fox_attn_kernel(q_ref, k_ref, v_ref, c_ref, o_ref, qaug_ref, kaug_ref, vaug_ref, m_ref, acc_ref,
                     *, blk, hp):
    g = pl.program_id(1)
    i = pl.program_id(2)
    seq = k_ref.shape[0]
    nt = (((1,), (1,)), ((), ()))

    def head_cols(t):
        return slice(t * HEAD_DIM, (t + 1) * HEAD_DIM)

    @pl.when(i == 0)
    def _():
        lane = lax.broadcasted_iota(jnp.int32, (seq, LANES), 1)
        for t in range(hp):
            c_key = jnp.sum(jnp.where(lane == g * hp + t, c_ref[...], 0.0), axis=1, keepdims=True)
            kaug_ref[t, :, :HEAD_DIM] = k_ref[:, head_cols(t)]
            kaug_ref[t, :, HEAD_DIM:] = _gate_lanes(-c_key, lane, own_first=False)
            vaug_ref[t, :, :HEAD_DIM] = v_ref[:, head_cols(t)]
            vaug_ref[t, :, HEAD_DIM:] = jnp.ones((seq, LANES), BF16)

    lane = lax.broadcasted_iota(jnp.int32, (blk, LANES), 1)
    c_rows = c_ref[pl.ds(pl.multiple_of(i * blk, blk), blk), :]
    for t in range(hp):
        c_query = jnp.sum(jnp.where(lane == g * hp + t, c_rows, 0.0), axis=1, keepdims=True)
        qaug_ref[t, :, :HEAD_DIM] = q_ref[:, head_cols(t)]
        qaug_ref[t, :, HEAD_DIM:] = _gate_lanes(c_query, lane, own_first=True)
    m_ref[...] = jnp.full(m_ref.shape, NEG_BIG, F32)
    acc_ref[...] = jnp.zeros(acc_ref.shape, F32)

    def step(j, masked):
        start = pl.multiple_of(j * blk, blk)
        scores = [lax.dot_general(qaug_ref[t], kaug_ref[t, pl.ds(start, blk), :], nt,
                                  preferred_element_type=F32) for t in range(hp)]
        for t in range(hp):
            s = scores[t]
            if masked:
                r = lax.broadcasted_iota(jnp.int32, (blk, blk), 0)
                c = lax.broadcasted_iota(jnp.int32, (blk, blk), 1)
                s = jnp.where(r >= c, s, -jnp.inf)
            m_prev = m_ref[t]
            m_new = jnp.maximum(m_prev, jnp.max(s, axis=-1, keepdims=True))
            alpha = jnp.exp2(m_prev - m_new)
            p = jnp.exp2(s - _lane_tile(m_new, blk // LANES)).astype(BF16)
            acc_ref[t] = (_lane_tile(alpha, acc_ref.shape[2] // LANES) * acc_ref[t]
                          + jnp.dot(p, vaug_ref[t, pl.ds(start, blk), :], preferred_element_type=F32))
            m_ref[t] = m_new

    def body(j, carry):
        step(j, False)
        return carry

    lax.fori_loop(0, i, body, 0)
    step(i, True)
    for t in range(hp):
        acc = acc_ref[t]
        o_ref[:, head_cols(t)] = (acc[:, :HEAD_DIM] * (1.0 / acc[:, HEAD_DIM:])).astype(o_ref.dtype)


def fox_attention(proj, call, batch, seq, q_col, k_col, v_col, blk, hp):
    nq = seq // blk
    width = hp * HEAD_DIM
    aug = HEAD_DIM + LANES
    return pl.pallas_call(
        functools.partial(_fox_attn_kernel, blk=blk, hp=hp),
        grid=(batch, FOX_HEADS // hp, nq),
        in_specs=[
            pl.BlockSpec((blk, width), lambda b, g, i: (b * nq + i, q_col // hp + g)),
            pl.BlockSpec((seq, width), lambda b, g, i: (b, k_col // hp + g)),
            pl.BlockSpec((seq, width), lambda b, g, i: (b, v_col // hp + g)),
            pl.BlockSpec((seq, LANES), lambda b, g, i: (b, 0)),
        ],
        out_specs=pl.BlockSpec((blk, width), lambda b, g, i: (b * nq + i, g)),
        out_shape=jax.ShapeDtypeStruct((batch * seq, FOX_HEADS * HEAD_DIM), BF16),
        scratch_shapes=[
            pltpu.VMEM((hp, blk, aug), BF16),
            pltpu.VMEM((hp, seq, aug), BF16),
            pltpu.VMEM((hp, seq, aug), BF16),
            pltpu.VMEM((hp, blk, LANES), F32),
            pltpu.VMEM((hp, blk, aug), F32),
        ],
        compiler_params=_params("parallel", "parallel", "arbitrary"),
        name="fox_attention",
    )(proj, proj, proj, call)


def _retention_kernel(q_ref, k_ref, v_ref, g_ref, cos_ref, sin_ref, dm_ref, aux_ref, o_ref,
                      r_ref, qr_ref, qx_ref, kr_ref, kz_ref, of_ref, *, cs, hp):
    rows_per_step = q_ref.shape[0]
    n_chunks = rows_per_step // cs
    half = HEAD_DIM // 2
    nt = (((1,), (1,)), ((), ()))

    @pl.when(pl.program_id(2) == 0)
    def _():
        r_ref[...] = jnp.zeros(r_ref.shape, F32)

    def head_cols(t):
        return slice(t * HEAD_DIM, (t + 1) * HEAD_DIM)

    cos = cos_ref[...]
    sin = sin_ref[...]
    for t in range(hp):
        q = q_ref[:, head_cols(t)].astype(F32)
        k = k_ref[:, head_cols(t)].astype(F32)
        qr = q * cos + pltpu.roll(q, half, 1) * sin
        kr = k * cos + pltpu.roll(k, half, 1) * sin
        xi = jnp.concatenate([aux_ref[t, 0]] * n_chunks, axis=0)
        zeta = jnp.concatenate([aux_ref[t, 1]] * n_chunks, axis=0)
        qr_ref[:, head_cols(t)] = qr.astype(BF16)
        qx_ref[:, head_cols(t)] = (qr * xi).astype(BF16)
        kr_ref[:, head_cols(t)] = kr.astype(BF16)
        kz_ref[:, head_cols(t)] = (kr * zeta).astype(BF16)

    def body(c, carry):
        rows = pl.ds(pl.multiple_of(c * cs, cs), cs)
        for t in range(hp):
            v = v_ref[rows, head_cols(t)]
            s = lax.dot_general(qr_ref[rows, head_cols(t)], kr_ref[rows, head_cols(t)], nt,
                                preferred_element_type=F32) * dm_ref[t]
            r = r_ref[t]
            lhs = jnp.concatenate([s.astype(BF16), qx_ref[rows, head_cols(t)]], axis=1)
            rhs = jnp.concatenate([v, r.astype(BF16)], axis=0)
            of_ref[rows, head_cols(t)] = jnp.dot(lhs, rhs, preferred_element_type=F32)
            r_ref[t] = aux_ref[t, 2, :HEAD_DIM] * r + lax.dot_general(
                kz_ref[rows, head_cols(t)], v, (((0,), (0,)), ((), ())), preferred_element_type=F32)
        return carry

    lax.fori_loop(0, n_chunks, body, 0)

    for t in range(hp):
        o = of_ref[:, head_cols(t)]
        mu = jnp.mean(o, axis=-1, keepdims=True)
        oc = o - mu
        var = jnp.mean(oc * oc, axis=-1, keepdims=True)
        on = oc * lax.rsqrt(var + EPS)
        g = g_ref[:, head_cols(t)].astype(F32)
        o_ref[:, head_cols(t)] = (g * jax.nn.sigmoid(g) * on).astype(o_ref.dtype)


def retention(proj, cos, sin, dm, aux, batch, seq, q_col, k_col, v_col, g_col, rows, hp):
    width = hp * HEAD_DIM
    nr = seq // rows

    def col(c0):
        return pl.BlockSpec((rows, width), lambda b, g, i: (b * nr + i, c0 // hp + g))

    return pl.pallas_call(
        functools.partial(_retention_kernel, cs=RET_CHUNK, hp=hp),
        grid=(batch, RET_HEADS // hp, nr),
        in_specs=[
            col(q_col), col(k_col), col(v_col), col(g_col),
            pl.BlockSpec((rows, HEAD_DIM), lambda b, g, i: (i, 0)),
            pl.BlockSpec((rows, HEAD_DIM), lambda b, g, i: (i, 0)),
            pl.BlockSpec((hp, RET_CHUNK, RET_CHUNK), lambda b, g, i: (g, 0, 0)),
            pl.BlockSpec((hp, 3, RET_CHUNK, LANES), lambda b, g, i: (g, 0, 0, 0)),
        ],
        out_specs=pl.BlockSpec((rows, width), lambda b, g, i: (b * nr + i, g)),
        out_shape=jax.ShapeDtypeStruct((batch * seq, RET_HEADS * HEAD_DIM), BF16),
        scratch_shapes=[
            pltpu.VMEM((hp, HEAD_DIM, HEAD_DIM), F32),
            pltpu.VMEM((rows, width), BF16),
            pltpu.VMEM((rows, width), BF16),
            pltpu.VMEM((rows, width), BF16),
            pltpu.VMEM((rows, width), BF16),
            pltpu.VMEM((rows, width), F32),
        ],
        compiler_params=_params("parallel", "parallel", "arbitrary"),
        name="retention",
    )(proj, proj, proj, proj, cos, sin, dm, aux)


def _retention_tables(seq):
    half = HEAD_DIM // 2
    inv = ROPE_BASE ** (-jnp.arange(half, dtype=F32) / half)
    ang = jnp.arange(seq, dtype=F32)[:, None] * inv[None, :]
    cos, sin = jnp.cos(ang), jnp.sin(ang)
    cos_t = jnp.concatenate([cos, cos], axis=-1)
    sin_t = jnp.concatenate([-sin, sin], axis=-1)
    cs = RET_CHUNK
    log_g = jnp.log1p(-jnp.exp2(-5.0 - jnp.arange(RET_HEADS, dtype=F32)))
    pos = jnp.arange(cs, dtype=F32)
    diff = pos[:, None] - pos[None, :]
    dm = jnp.where(diff >= 0, jnp.exp(log_g[:, None, None] * jnp.maximum(diff, 0.0)), 0.0)
    zeta = jnp.exp(log_g[:, None] * (cs - 1 - pos)[None, :])
    xi = jnp.exp(log_g[:, None] * (pos + 1)[None, :])
    g_chunk = jnp.broadcast_to(jnp.exp(log_g * cs)[:, None], (RET_HEADS, cs))
    aux = jnp.broadcast_to(jnp.stack([xi, zeta, g_chunk], axis=1)[..., None], (RET_HEADS, 3, cs, LANES))
    return cos_t, sin_t, dm, aux


def _s5_kernel(u_ref, are_ref, aim_ref, ls_ref, bre_ref, bim_ref, cre_ref, cim_ref, d_ref, y_ref,
               toep_ref, win_ref, woutt_ref, aq_ref, wf_ref, ucat_ref, inc_ref, x_ref, *, q):
    m = u_ref.shape[0] // q
    sl = STATE_LANES

    @pl.when(pl.program_id(1) == 0)
    def _():
        _s5_build_tables(are_ref, aim_ref, ls_ref, bre_ref, bim_ref, cre_ref, cim_ref,
                         toep_ref, win_ref, woutt_ref, aq_ref, wf_ref, q=q)

    for s in range(q):
        ucat_ref[:, s * LANES:(s + 1) * LANES] = u_ref[pl.ds(s, m, stride=q), :].astype(BF16)
    ucat = ucat_ref[...]
    inc_ref[...] = jnp.dot(ucat, win_ref[...], preferred_element_type=F32)
    a_re = aq_ref[0:1, :]
    a_im = aq_ref[1:2, :]

    def body(n, carry):
        x_re, x_im = carry
        x_ref[pl.ds(n, 1), 0:sl] = x_re
        x_ref[pl.ds(n, 1), sl:2 * sl] = x_im
        i_re = inc_ref[pl.ds(n, 1), 0:sl]
        i_im = inc_ref[pl.ds(n, 1), sl:2 * sl]
        return (a_re * x_re - a_im * x_im + i_re, a_re * x_im + a_im * x_re + i_im)

    zero = jnp.zeros((1, sl), F32)
    lax.fori_loop(0, m, body, (zero, zero))
    xb = x_ref[...].astype(BF16)
    nt = (((1,), (1,)), ((), ()))
    for t0 in range(0, q, S5_OUT_STEPS):
        hi = (t0 + S5_OUT_STEPS) * LANES
        cols = slice(t0 * LANES, hi)
        y = jnp.dot(ucat_ref[:, :hi], toep_ref[:hi, cols], preferred_element_type=F32)
        y = y + lax.dot_general(xb, woutt_ref[cols, :], nt, preferred_element_type=F32)
        for t in range(t0, t0 + S5_OUT_STEPS):
            rows = pl.ds(t, m, stride=q)
            y_ref[rows, :] = (y[:, (t - t0) * LANES:(t - t0 + 1) * LANES]
                              + d_ref[...] * u_ref[rows, :])


def s5_core(proj, a_re, a_im, b_re, b_im, c_re, c_im, log_step, d_skip, batch, seq, u_col):
    q = S5_CHUNK
    m = seq // q
    g, p = a_re.shape
    gb = GROUPS_PER_BLOCK
    nblk = g // gb
    sl = gb * p

    def rows(t):
        return t.astype(F32).reshape(nblk, 1, sl)

    def b_mat(t):
        return t.astype(F32).reshape(nblk, gb, p, S5_GROUP).transpose(0, 3, 1, 2).reshape(nblk, S5_GROUP, sl)

    def c_mat(t):
        return t.astype(F32).reshape(nblk, gb, S5_GROUP, p).transpose(0, 2, 1, 3).reshape(nblk, S5_GROUP, sl)

    row_spec = pl.BlockSpec((1, 1, sl), lambda j, b: (j, 0, 0))
    mat_spec = pl.BlockSpec((1, S5_GROUP, sl), lambda j, b: (j, 0, 0))
    return pl.pallas_call(
        functools.partial(_s5_kernel, q=q),
        grid=(nblk, batch),
        in_specs=[
            pl.BlockSpec((seq, LANES), lambda j, b: (b, u_col + j)),
            row_spec, row_spec, row_spec, mat_spec, mat_spec, mat_spec, mat_spec,
            pl.BlockSpec((1, LANES), lambda j, b: (0, j)),
        ],
        out_specs=pl.BlockSpec((seq, LANES), lambda j, b: (b, j)),
        out_shape=jax.ShapeDtypeStruct((batch * seq, nblk * LANES), F32),
        scratch_shapes=[
            pltpu.VMEM((q * LANES, q * LANES), BF16),
            pltpu.VMEM((q * LANES, 2 * sl), BF16),
            pltpu.VMEM((q * LANES, 2 * sl), BF16),
            pltpu.VMEM((2, sl), F32),
            pltpu.VMEM((q * LANES, 2 * sl), F32),
            pltpu.VMEM((m, q * LANES), BF16),
            pltpu.VMEM((m, 2 * sl), F32),
            pltpu.VMEM((m, 2 * sl), F32),
        ],
        compiler_params=_params("parallel", "arbitrary"),
        name="s5_core",
    )(proj, rows(a_re), rows(a_im), rows(log_step), b_mat(b_re), b_mat(b_im), c_mat(c_re), c_mat(c_im),
      d_skip)


def _dot_nt_split(x, y):
    nt = (((1,), (1,)), ((), ()))
    xh = x.astype(BF16)
    xl = (x - xh.astype(F32)).astype(BF16)
    yh = y.astype(BF16)
    yl = (y - yh.astype(F32)).astype(BF16)
    out = lax.dot_general(xh, yh, nt, preferred_element_type=F32)
    out = out + lax.dot_general(xh, yl, nt, preferred_element_type=F32)
    return out + lax.dot_general(xl, yh, nt, preferred_element_type=F32)


def _s5_build_tables(are_ref, aim_ref, ls_ref, bre_ref, bim_ref, cre_ref, cim_ref,
                     toep_ref, win_ref, woutt_ref, aq_ref, wf_ref, *, q):
    sl = STATE_LANES
    lam_re = jnp.minimum(are_ref[0], -1e-4)
    lam_im = aim_ref[0]
    step = jnp.exp(ls_ref[0])
    mag = jnp.exp(lam_re * step)
    a_re = mag * jnp.cos(lam_im * step)
    a_im = mag * jnp.sin(lam_im * step)
    den = lam_re * lam_re + lam_im * lam_im
    f_re = ((a_re - 1.0) * lam_re + a_im * lam_im) / den
    f_im = (a_im * lam_re - (a_re - 1.0) * lam_im) / den
    b_re = bre_ref[0]
    b_im = bim_ref[0]
    bb_re = f_re * b_re - f_im * b_im
    bb_im = f_re * b_im + f_im * b_re
    c_re = cre_ref[0]
    c_im = cim_ref[0]

    pows = [(jnp.ones((1, sl), F32), jnp.zeros((1, sl), F32))]
    for _ in range(q):
        p_re, p_im = pows[-1]
        pows.append((p_re * a_re - p_im * a_im, p_re * a_im + p_im * a_re))

    row_group = lax.shift_right_logical(lax.broadcasted_iota(jnp.int32, (LANES, sl), 0), S5_GROUP_SHIFT)
    col_group = lax.shift_right_logical(lax.broadcasted_iota(jnp.int32, (LANES, sl), 1), S5_STATE_SHIFT)
    same_group = row_group == col_group

    def tile(k, m_re, m_im, im_sign):
        p_re, p_im = pows[k]
        v_re = p_re * m_re - p_im * m_im
        v_im = (p_re * m_im + p_im * m_re) * im_sign
        e_re = jnp.where(same_group, jnp.concatenate([v_re] * GROUPS_PER_BLOCK, axis=0), 0.0)
        e_im = jnp.where(same_group, jnp.concatenate([v_im] * GROUPS_PER_BLOCK, axis=0), 0.0)
        return jnp.concatenate([e_re, e_im], axis=1)

    for s in range(q):
        rows = slice(s * LANES, (s + 1) * LANES)
        wf_ref[rows, :] = tile(q - 1 - s, bb_re, bb_im, 1.0)
        woutt_ref[rows, :] = tile(s + 1, c_re, c_im, -1.0).astype(BF16)
    win_ref[...] = wf_ref[...].astype(BF16)
    taps = _dot_nt_split(wf_ref[...], tile(0, c_re, c_im, -1.0))
    toep_ref[...] = jnp.zeros(toep_ref.shape, BF16)
    for s in range(q):
        for t in range(s, q):
            lag_rows = slice((q - 1 - (t - s)) * LANES, (q - (t - s)) * LANES)
            toep_ref[s * LANES:(s + 1) * LANES, t * LANES:(t + 1) * LANES] = taps[lag_rows, :].astype(BF16)
    aq_ref[0:1, :] = pows[q][0]
    aq_ref[1:2, :] = pows[q][1]


def _even_weights(w_in):
    fw = FOX_HEADS * HEAD_DIM
    rw = RET_HEADS * HEAD_DIM
    scale = HEAD_DIM ** -0.5
    c0 = 3 * fw
    c1 = c0 + FOX_HEADS
    fox_scale = jnp.concatenate([jnp.full((fw,), scale * LOG2E, F32), jnp.ones((2 * fw,), F32)])
    ret_scale = jnp.concatenate([jnp.ones((rw,), F32), jnp.full((rw,), scale, F32), jnp.ones((2 * rw,), F32)])
    w_fox = (w_in[:, :, :c0] * fox_scale).astype(BF16)
    w_ret = (w_in[:, :, c1:] * ret_scale).astype(BF16)
    w_forget = jnp.pad(w_in[:, :, c0:c1], ((0, 0), (0, 0), (0, LANES - FOX_HEADS))).astype(BF16)
    return w_fox, w_forget, w_ret


def _even_mixer(h, g, weights, e, b_forget, tables, batch, seq):
    w_fox, w_forget, w_ret = weights
    bias = jnp.pad(b_forget.astype(F32), (0, LANES - FOX_HEADS)).reshape(1, LANES)
    t = h.shape[0]
    tm = _row_tile(t, PROJ_ROWS)
    proj_fox, fl = norm_matmul(h, g, w_fox, e, 1, BF16, tm=tm, tn=PROJ_COLS, w_side=w_forget)
    proj_ret = norm_matmul(h, g, w_ret, e, 1, BF16, tm=tm, tn=PROJ_COLS)
    call = fox_cumsum(fl, bias, batch, seq)
    nh = FOX_HEADS
    fox = fox_attention(proj_fox, call, batch, seq, 0, nh, 2 * nh, blk=_row_tile(seq, FOX_BLOCK),
                        hp=FOX_HEADS_PER_STEP)
    cos_t, sin_t, dm, aux = tables
    nr = RET_HEADS
    ret = retention(proj_ret, cos_t, sin_t, dm, aux, batch, seq, 0, nr, 2 * nr, 3 * nr,
                    rows=_row_tile(seq, RET_ROWS), hp=RET_HEADS_PER_STEP)
    return fox, ret


def _odd_mixer(h, g, w_in, o, a_re, a_im, b_re, b_im, c_re, c_im, d_skip, log_step, batch, seq):
    sw = a_re.shape[0] * S5_GROUP
    t = h.shape[0]
    proj = norm_matmul(h, g, w_in, o, 1, F32, tm=_row_tile(t, PROJ_ROWS), tn=PROJ_COLS)
    y = s5_core(proj, a_re, a_im, b_re, b_im, c_re, c_im, log_step, d_skip.reshape(1, sw).astype(F32),
                batch, seq, 0)
    return y, proj


def kernel(x, mem, norm_mix, norm_xattn, norm_mlp, norm_mem, norm_final, ab_w_in, ab_b_forget, ab_w_out,
           cd_w_in, s5_a_re, s5_a_im, s5_b_re, s5_b_im, s5_c_re, s5_c_im, s5_d, s5_log_step, s5_w_glu,
           conv_w, cd_w_out, xa_wq, xa_wkv, xa_wo, mlp_w1, mlp_w2):
    batch, seq, d = x.shape
    depth = norm_mix.shape[0]
    mlen = mem.shape[1]
    xw = xa_wq.shape[2]
    h = x.reshape(batch * seq, d)
    memf = mem.reshape(batch * mlen, d)
    tables = _retention_tables(seq)
    even_w = _even_weights(ab_w_in)
    odd_w_in = cd_w_in.astype(BF16)
    even_w_out = ab_w_out.astype(BF16)
    odd_w_out = cd_w_out.astype(BF16)
    w_glu = s5_w_glu.astype(BF16)
    conv_wf = conv_w.astype(F32)
    wq = (xa_wq * HEAD_DIM ** -0.5).astype(BF16)
    wo = xa_wo.astype(BF16)
    w1 = mlp_w1.astype(BF16)
    w2 = mlp_w2.astype(BF16)
    kv_all = norm_matmul(memf, norm_mem, xa_wkv.astype(BF16), 0, depth, BF16,
                         tm=_row_tile(batch * mlen, PROJ_ROWS), tn=PROJ_COLS)
    kv_all = kv_all.reshape(batch, mlen, depth * 2 * xw)
    for layer in range(depth):
        odd = layer % 2 == 1
        idx = layer // 2
        if odd:
            y, proj = _odd_mixer(h, norm_mix[layer], odd_w_in, idx, s5_a_re[idx], s5_a_im[idx],
                                 s5_b_re[idx], s5_b_im[idx], s5_c_re[idx], s5_c_im[idx], s5_d[idx],
                                 s5_log_step[idx], batch, seq)
            mixer_out = (y, proj, conv_wf, w_glu)
            w_out = odd_w_out
        else:
            mixer_out = _even_mixer(h, norm_mix[layer], even_w, idx, ab_b_forget[idx], tables, batch, seq)
            w_out = even_w_out
        h = mix_xattn_block(h, mixer_out, w_out, idx, norm_xattn[layer], wq, kv_all, wo, layer,
                            seq, tm=_row_tile(seq, MIX_ROWS), odd=odd)
        h = mlp_block(h, norm_mlp[layer], w1, w2, layer, norm_final, layer == depth - 1,
                      tm=_row_tile(batch * seq, MLP_ROWS), tf=MLP_COLS)
    return h.reshape(batch, seq, d)
```

```python
import functools
import math

import jax
import jax.numpy as jnp
from jax import lax
from jax.experimental import pallas as pl
from jax.experimental.pallas import tpu as pltpu

F32 = jnp.float32
BF16 = jnp.bfloat16

EPS = 1e-6
ROPE_BASE = 10000.0
LANES = 128
HEAD_DIM = 128
FOX_HEADS = 8
RET_HEADS = 8
RET_CHUNK = 256
XA_HEADS = 4
S5_GROUP = 16
S5_STATE = 64
S5_CHUNK = 16
S5_OUT_STEPS = 2
CONV_K = 3
S5_GROUP_SHIFT = S5_GROUP.bit_length() - 1
S5_STATE_SHIFT = S5_STATE.bit_length() - 1
GROUPS_PER_BLOCK = LANES // S5_GROUP
STATE_LANES = GROUPS_PER_BLOCK * S5_STATE
VMEM_LIMIT = 56 * 1024 * 1024
NEG_BIG = -1e30
LOG2E = math.log2(math.e)

PROJ_ROWS, PROJ_COLS = 1024, 1024
MIX_ROWS = 512
MLP_ROWS, MLP_COLS = 1024, 512
FOX_BLOCK = 1024
FOX_HEADS_PER_STEP = 2
RET_ROWS = 1024
RET_HEADS_PER_STEP = 8


def _row_tile(rows, want):
    tile = min(rows, want)
    while rows % tile:
        tile //= 2
    return tile


def _params(*sem):
    return pltpu.CompilerParams(dimension_semantics=sem, vmem_limit_bytes=VMEM_LIMIT)


def _rms(x, g):
    ms = jnp.mean(x * x, axis=-1, keepdims=True)
    return x * lax.rsqrt(ms + EPS) * g


def _norm_matmul_kernel(*refs, side, transposed):
    if side:
        x_ref, g_ref, w_ref, ws_ref, o_ref, os_ref, xn_ref = refs
    else:
        x_ref, g_ref, w_ref, o_ref, xn_ref = refs
    dims = (((1,), (1 if transposed else 0,)), ((), ()))

    @pl.when(pl.program_id(1) == 0)
    def _():
        xn_ref[...] = _rms(x_ref[...], g_ref[...]).astype(BF16)
        if side:
            os_ref[...] = lax.dot_general(xn_ref[...], ws_ref[...], dims, preferred_element_type=F32)

    o_ref[...] = lax.dot_general(xn_ref[...], w_ref[...], dims,
                                 preferred_element_type=F32).astype(o_ref.dtype)


def norm_matmul(x, g, w, layer, n_layers, out_dtype, tm, tn, w_side=None, transposed=False):
    t, d = x.shape
    n = w.shape[1 if transposed else 2]
    nj = n // tn

    def w_spec(cols, index):
        if transposed:
            return pl.BlockSpec((None, cols, d), lambda i, j: (index(j)[0], index(j)[1], 0))
        return pl.BlockSpec((None, d, cols), lambda i, j: (index(j)[0], 0, index(j)[1]))

    side = w_side is not None
    in_specs = [
        pl.BlockSpec((tm, d), lambda i, j: (i, 0)),
        pl.BlockSpec((1, d), lambda i, j: (0, 0)),
        w_spec(tn, lambda j: (layer + j // nj, j % nj)),
    ]
    out_specs = pl.BlockSpec((tm, tn), lambda i, j: (i, j))
    out_shape = jax.ShapeDtypeStruct((t, n_layers * n), out_dtype)
    args = (x, g.reshape(1, d), w)
    if side:
        ns = w_side.shape[1 if transposed else 2]
        in_specs.append(w_spec(ns, lambda j: (layer, 0)))
        out_specs = [out_specs, pl.BlockSpec((tm, ns), lambda i, j: (i, 0))]
        out_shape = [out_shape, jax.ShapeDtypeStruct((t, ns), F32)]
        args = args + (w_side,)
    return pl.pallas_call(
        functools.partial(_norm_matmul_kernel, side=side, transposed=transposed),
        grid=(t // tm, n_layers * nj),
        in_specs=in_specs,
        out_specs=out_specs,
        out_shape=out_shape,
        scratch_shapes=[pltpu.VMEM((tm, d), BF16)],
        compiler_params=_params("parallel", "arbitrary"),
        name="norm_matmul",
    )(*args)


def _mlp_kernel(h_ref, g_ref, w1_ref, w2_ref, gf_ref, o_ref, xn_ref, *, final_norm):
    f = pl.program_id(1)

    @pl.when(f == 0)
    def _():
        x = h_ref[...]
        xn_ref[...] = _rms(x, g_ref[...]).astype(BF16)
        o_ref[...] = x

    a = jnp.dot(xn_ref[...], w1_ref[...], preferred_element_type=F32)
    a = jnp.square(jnp.maximum(a, 0.0)).astype(BF16)
    o_ref[...] += jnp.dot(a, w2_ref[...], preferred_element_type=F32)

    if final_norm:
        @pl.when(f == pl.num_programs(1) - 1)
        def _():
            o_ref[...] = _rms(o_ref[...], gf_ref[...])


def mlp_block(h, g, w1, w2, layer, g_final, final_norm, tm, tf):
    t, d = h.shape
    dff = w1.shape[2]
    return pl.pallas_call(
        functools.partial(_mlp_kernel, final_norm=final_norm),
        grid=(t // tm, dff // tf),
        in_specs=[
            pl.BlockSpec((tm, d), lambda i, f: (i, 0)),
            pl.BlockSpec((1, d), lambda i, f: (0, 0)),
            pl.BlockSpec((None, d, tf), lambda i, f: (layer, 0, f)),
            pl.BlockSpec((None, tf, d), lambda i, f: (layer, f, 0)),
            pl.BlockSpec((1, d), lambda i, f: (0, 0)),
        ],
        out_specs=pl.BlockSpec((tm, d), lambda i, f: (i, 0)),
        out_shape=jax.ShapeDtypeStruct((t, d), F32),
        scratch_shapes=[pltpu.VMEM((tm, d), BF16)],
        compiler_params=_params("parallel", "arbitrary"),
        name="mlp_block",
    )(h, g.reshape(1, d), w1, w2, g_final.reshape(1, d))


def _s5_gate(y, w_glu):
    g = 0.5 * y * (1.0 + jnp.tanh(math.sqrt(2.0 / math.pi) * (y + 0.044715 * (y * y * y))))
    z = jnp.dot(g.astype(BF16), w_glu, preferred_element_type=F32)
    return (g * jax.nn.sigmoid(z)).astype(BF16)


def _short_conv(hc, gb, gc, z_prev, w_ref):
    z = gc * hc
    row = lax.broadcasted_iota(jnp.int32, z.shape, 0)
    y = w_ref[CONV_K - 1:CONV_K, :] * z
    for lag in range(1, CONV_K):
        zl = pltpu.roll(z, lag, 0)
        for r in range(lag):
            zl = jnp.where(row == r, z_prev[8 - lag + r:8 - lag + r + 1, :], zl)
        y = y + w_ref[CONV_K - 1 - lag:CONV_K - lag, :] * zl
    return (gb * y).astype(BF16)


def _mix_xattn_kernel(*refs, odd, per_batch):
    if odd:
        (h_ref, y_ref, hc_ref, gb_ref, gc_ref, hcp_ref, gcp_ref, cw_ref, wglu_ref,
         w1_ref, w2_ref, g_ref, wq_ref, k_ref, v_ref, wo_ref, o_ref) = refs
        a1 = _s5_gate(y_ref[...], wglu_ref[...])
        first = pl.program_id(0) % per_batch == 0
        z_prev = jnp.where(first, 0.0, gcp_ref[...] * hcp_ref[...])
        a2 = _short_conv(hc_ref[...], gb_ref[...], gc_ref[...], z_prev, cw_ref)
    else:
        h_ref, a1_ref, a2_ref, w1_ref, w2_ref, g_ref, wq_ref, k_ref, v_ref, wo_ref, o_ref = refs
        a1 = a1_ref[...]
        a2 = a2_ref[...]
    x = h_ref[...] + jnp.dot(a1, w1_ref[...], preferred_element_type=F32)
    x = x + jnp.dot(a2, w2_ref[...], preferred_element_type=F32)
    xn = _rms(x, g_ref[...]).astype(BF16)
    q = jnp.dot(xn, wq_ref[...], preferred_element_type=F32).astype(BF16)
    heads = []
    for hd in range(XA_HEADS):
        sl = slice(hd * HEAD_DIM, (hd + 1) * HEAD_DIM)
        s = lax.dot_general(q[:, sl], k_ref[0, :, sl], (((1,), (1,)), ((), ())),
                            preferred_element_type=F32)
        m = jnp.max(s, axis=-1, keepdims=True)
        p = jnp.exp(s - m)
        l = jnp.sum(p, axis=-1, keepdims=True)
        oh = jnp.dot(p.astype(BF16), v_ref[0, :, sl], preferred_element_type=F32)
        heads.append((oh * (1.0 / l)).astype(BF16))
    o = jnp.concatenate(heads, axis=-1)
    o_ref[...] = x + jnp.dot(o, wo_ref[...], preferred_element_type=F32)


def mix_xattn_block(h, mixer_inputs, w_out, mixer_idx, g, wq, kv, wo, layer, seq, tm, odd):
    t, d = h.shape
    k1 = w_out.shape[1] // 2
    k2 = k1
    mlen = kv.shape[1]
    xw = wq.shape[2]
    per_batch = seq // tm

    def fixed(shape, *index):
        return pl.BlockSpec(shape, lambda i: index, pipeline_mode=pl.Buffered(1))

    def rows(width, col=0):
        return pl.BlockSpec((tm, width), lambda i: (i, col))

    if odd:
        y, proj, conv_w, w_glu = mixer_inputs
        halo = tm // 8

        def prev_rows(col):
            return pl.BlockSpec((8, k2), lambda i: (jnp.maximum(i * halo - 1, 0), col))

        mixer_args = (y, proj, proj, proj, proj, proj, conv_w, w_glu)
        mixer_specs = [rows(k1), rows(k2, 1), rows(k2, 2), rows(k2, 3), prev_rows(1), prev_rows(3),
                       fixed((None,) + conv_w.shape[1:], mixer_idx, 0, 0),
                       fixed((None,) + w_glu.shape[1:], mixer_idx, 0, 0)]
    else:
        mixer_args = mixer_inputs
        mixer_specs = [rows(k1), rows(k2)]

    return pl.pallas_call(
        functools.partial(_mix_xattn_kernel, odd=odd, per_batch=per_batch),
        grid=(t // tm,),
        in_specs=[rows(d)] + mixer_specs + [
            fixed((None, k1, d), mixer_idx, 0, 0), fixed((None, k2, d), mixer_idx, 1, 0),
            fixed((1, d), 0, 0), fixed((None, d, xw), layer, 0, 0),
            pl.BlockSpec((1, mlen, xw), lambda i: (i // per_batch, 0, 2 * layer)),
            pl.BlockSpec((1, mlen, xw), lambda i: (i // per_batch, 0, 2 * layer + 1)),
            fixed((None, xw, d), layer, 0, 0),
        ],
        out_specs=rows(d),
        out_shape=jax.ShapeDtypeStruct((t, d), F32),
        compiler_params=_params("parallel"),
        name="mix_xattn_block",
    )(h, *mixer_args, w_out, w_out, g.reshape(1, d), wq, kv, kv, wo)


def _split3(x):
    hi = x.astype(BF16)
    r1 = x - hi.astype(F32)
    mid = r1.astype(BF16)
    lo = (r1 - mid.astype(F32)).astype(BF16)
    return hi, mid, lo


def _fox_cumsum_kernel(fl_ref, b_ref, call_ref, *, blk):
    seq = fl_ref.shape[0]
    row = lax.broadcasted_iota(jnp.int32, (blk, blk), 0)
    col = lax.broadcasted_iota(jnp.int32, (blk, blk), 1)
    tri = jnp.where(row >= col, 1.0, 0.0).astype(BF16)
    carry = jnp.zeros((1, LANES), F32)
    for i in range(seq // blk):
        rows = slice(i * blk, (i + 1) * blk)
        x = fl_ref[rows, :] + b_ref[...]
        ls = (jnp.minimum(x, 0.0) - jnp.log(1.0 + jnp.exp(-jnp.abs(x)))) * LOG2E
        c = carry
        for piece in _split3(ls):
            c = c + jnp.dot(tri, piece, preferred_element_type=F32)
        call_ref[rows, :] = c
        carry = c[blk - 1:blk, :]


def fox_cumsum(fl, bias, batch, seq):
    blk = min(256, seq)
    return pl.pallas_call(
        functools.partial(_fox_cumsum_kernel, blk=blk),
        grid=(batch,),
        in_specs=[
            pl.BlockSpec((seq, LANES), lambda b: (b, 0)),
            pl.BlockSpec((1, LANES), lambda b: (0, 0)),
        ],
        out_specs=pl.BlockSpec((seq, LANES), lambda b: (b, 0)),
        out_shape=jax.ShapeDtypeStruct((batch * seq, LANES), F32),
        compiler_params=_params("parallel"),
        name="fox_cumsum",
    )(fl, bias)


def _lane_tile(x, reps):
    return jnp.concatenate([x] * reps, axis=1)


def _gate_lanes(c_col, lane, own_first):
    hi, mid, lo = [p.astype(F32) for p in _split3(c_col)]
    base = 0 if own_first else 3
    pieces = jnp.where(lane == base, hi, jnp.where(lane == base + 1, mid, jnp.where(lane == base + 2, lo, 0.0)))
    ones = jnp.where((lane >= 3 - base) & (lane < 6 - base), 1.0, 0.0)
    return (pieces + ones).astype(BF16)


def _fox_attn_kernel(q_ref, k_ref, v_ref, c_ref, o_ref, qaug_ref, kaug_ref, vaug_ref, m_ref, acc_ref,
                     *, blk, hp):
    g = pl.program_id(1)
    i = pl.program_id(2)
    seq = k_ref.shape[0]
    nt = (((1,), (1,)), ((), ()))

    def head_cols(t):
        return slice(t * HEAD_DIM, (t + 1) * HEAD_DIM)

    @pl.when(i == 0)
    def _():
        lane = lax.broadcasted_iota(jnp.int32, (seq, LANES), 1)
        for t in range(hp):
            c_key = jnp.sum(jnp.where(lane == g * hp + t, c_ref[...], 0.0), axis=1, keepdims=True)
            kaug_ref[t, :, :HEAD_DIM] = k_ref[:, head_cols(t)]
            kaug_ref[t, :, HEAD_DIM:] = _gate_lanes(-c_key, lane, own_first=False)
            vaug_ref[t, :, :HEAD_DIM] = v_ref[:, head_cols(t)]
            vaug_ref[t, :, HEAD_DIM:] = jnp.ones((seq, LANES), BF16)

    lane = lax.broadcasted_iota(jnp.int32, (blk, LANES), 1)
    c_rows = c_ref[pl.ds(pl.multiple_of(i * blk, blk), blk), :]
    for t in range(hp):
        c_query = jnp.sum(jnp.where(lane == g * hp + t, c_rows, 0.0), axis=1, keepdims=True)
        qaug_ref[t, :, :HEAD_DIM] = q_ref[:, head_cols(t)]
        qaug_ref[t, :, HEAD_DIM:] = _gate_lanes(c_query, lane, own_first=True)
    m_ref[...] = jnp.full(m_ref.shape, NEG_BIG, F32)
    acc_ref[...] = jnp.zeros(acc_ref.shape, F32)

    def step(j, masked):
        start = pl.multiple_of(j * blk, blk)
        scores = [lax.dot_general(qaug_ref[t], kaug_ref[t, pl.ds(start, blk), :], nt,
                                  preferred_element_type=F32) for t in range(hp)]
        for t in range(hp):
            s = scores[t]
            if masked:
                r = lax.broadcasted_iota(jnp.int32, (blk, blk), 0)
                c = lax.broadcasted_iota(jnp.int32, (blk, blk), 1)
                s = jnp.where(r >= c, s, -jnp.inf)
            m_prev = m_ref[t]
            m_new = jnp.maximum(m_prev, jnp.max(s, axis=-1, keepdims=True))
            alpha = jnp.exp2(m_prev - m_new)
            p = jnp.exp2(s - _lane_tile(m_new, blk // LANES)).astype(BF16)
            acc_ref[t] = (_lane_tile(alpha, acc_ref.shape[2] // LANES) * acc_ref[t]
                          + jnp.dot(p, vaug_ref[t, pl.ds(start, blk), :], preferred_element_type=F32))
            m_ref[t] = m_new

    def body(j, carry):
        step(j, False)
        return carry

    lax.fori_loop(0, i, body, 0)
    step(i, True)
    for t in range(hp):
        acc = acc_ref[t]
        o_ref[:, head_cols(t)] = (acc[:, :HEAD_DIM] * (1.0 / acc[:, HEAD_DIM:])).astype(o_ref.dtype)


def fox_attention(proj, call, batch, seq, q_col, k_col, v_col, blk, hp):
    nq = seq // blk
    width = hp * HEAD_DIM
    aug = HEAD_DIM + LANES
    return pl.pallas_call(
        functools.partial(_fox_attn_kernel, blk=blk, hp=hp),
        grid=(batch, FOX_HEADS // hp, nq),
        in_specs=[
            pl.BlockSpec((blk, width), lambda b, g, i: (b * nq + i, q_col // hp + g)),
            pl.BlockSpec((seq, width), lambda b, g, i: (b, k_col // hp + g)),
            pl.BlockSpec((seq, width), lambda b, g, i: (b, v_col // hp + g)),
            pl.BlockSpec((seq, LANES), lambda b, g, i: (b, 0)),
        ],
        out_specs=pl.BlockSpec((blk, width), lambda b, g, i: (b * nq + i, g)),
        out_shape=jax.ShapeDtypeStruct((batch * seq, FOX_HEADS * HEAD_DIM), BF16),
        scratch_shapes=[
            pltpu.VMEM((hp, blk, aug), BF16),
            pltpu.VMEM((hp, seq, aug), BF16),
            pltpu.VMEM((hp, seq, aug), BF16),
            pltpu.VMEM((hp, blk, LANES), F32),
            pltpu.VMEM((hp, blk, aug), F32),
        ],
        compiler_params=_params("parallel", "parallel", "arbitrary"),
        name="fox_attention",
    )(proj, proj, proj, call)


def _retention_kernel(q_ref, k_ref, v_ref, g_ref, cos_ref, sin_ref, dm_ref, aux_ref, o_ref,
                      r_ref, qr_ref, qx_ref, kr_ref, kz_ref, of_ref, *, cs, hp):
    rows_per_step = q_ref.shape[0]
    n_chunks = rows_per_step // cs
    half = HEAD_DIM // 2
    nt = (((1,), (1,)), ((), ()))

    @pl.when(pl.program_id(2) == 0)
    def _():
        r_ref[...] = jnp.zeros(r_ref.shape, F32)

    def head_cols(t):
        return slice(t * HEAD_DIM, (t + 1) * HEAD_DIM)

    cos = cos_ref[...]
    sin = sin_ref[...]
    for t in range(hp):
        q = q_ref[:, head_cols(t)].astype(F32)
        k = k_ref[:, head_cols(t)].astype(F32)
        qr = q * cos + pltpu.roll(q, half, 1) * sin
        kr = k * cos + pltpu.roll(k, half, 1) * sin
        xi = jnp.concatenate([aux_ref[t, 0]] * n_chunks, axis=0)
        zeta = jnp.concatenate([aux_ref[t, 1]] * n_chunks, axis=0)
        qr_ref[:, head_cols(t)] = qr.astype(BF16)
        qx_ref[:, head_cols(t)] = (qr * xi).astype(BF16)
        kr_ref[:, head_cols(t)] = kr.astype(BF16)
        kz_ref[:, head_cols(t)] = (kr * zeta).astype(BF16)

    def body(c, carry):
        rows = pl.ds(pl.multiple_of(c * cs, cs), cs)
        for t in range(hp):
            v = v_ref[rows, head_cols(t)]
            s = lax.dot_general(qr_ref[rows, head_cols(t)], kr_ref[rows, head_cols(t)], nt,
                                preferred_element_type=F32) * dm_ref[t]
            r = r_ref[t]
            lhs = jnp.concatenate([s.astype(BF16), qx_ref[rows, head_cols(t)]], axis=1)
            rhs = jnp.concatenate([v, r.astype(BF16)], axis=0)
            of_ref[rows, head_cols(t)] = jnp.dot(lhs, rhs, preferred_element_type=F32)
            r_ref[t] = aux_ref[t, 2, :HEAD_DIM] * r + lax.dot_general(
                kz_ref[rows, head_cols(t)], v, (((0,), (0,)), ((), ())), preferred_element_type=F32)
        return carry

    lax.fori_loop(0, n_chunks, body, 0)

    for t in range(hp):
        o = of_ref[:, head_cols(t)]
        mu = jnp.mean(o, axis=-1, keepdims=True)
        oc = o - mu
        var = jnp.mean(oc * oc, axis=-1, keepdims=True)
        on = oc * lax.rsqrt(var + EPS)
        g = g_ref[:, head_cols(t)].astype(F32)
        o_ref[:, head_cols(t)] = (g * jax.nn.sigmoid(g) * on).astype(o_ref.dtype)


def retention(proj, cos, sin, dm, aux, batch, seq, q_col, k_col, v_col, g_col, rows, hp):
    width = hp * HEAD_DIM
    nr = seq // rows

    def col(c0):
        return pl.BlockSpec((rows, width), lambda b, g, i: (b * nr + i, c0 // hp + g))

    return pl.pallas_call(
        functools.partial(_retention_kernel, cs=RET_CHUNK, hp=hp),
        grid=(batch, RET_HEADS // hp, nr),
        in_specs=[
            col(q_col), col(k_col), col(v_col), col(g_col),
            pl.BlockSpec((rows, HEAD_DIM), lambda b, g, i: (i, 0)),
            pl.BlockSpec((rows, HEAD_DIM), lambda b, g, i: (i, 0)),
            pl.BlockSpec((hp, RET_CHUNK, RET_CHUNK), lambda b, g, i: (g, 0, 0)),
            pl.BlockSpec((hp, 3, RET_CHUNK, LANES), lambda b, g, i: (g, 0, 0, 0)),
        ],
        out_specs=pl.BlockSpec((rows, width), lambda b, g, i: (b * nr + i, g)),
        out_shape=jax.ShapeDtypeStruct((batch * seq, RET_HEADS * HEAD_DIM), BF16),
        scratch_shapes=[
            pltpu.VMEM((hp, HEAD_DIM, HEAD_DIM), F32),
            pltpu.VMEM((rows, width), BF16),
            pltpu.VMEM((rows, width), BF16),
            pltpu.VMEM((rows, width), BF16),
            pltpu.VMEM((rows, width), BF16),
            pltpu.VMEM((rows, width), F32),
        ],
        compiler_params=_params("parallel", "parallel", "arbitrary"),
        name="retention",
    )(proj, proj, proj, proj, cos, sin, dm, aux)


def _retention_tables(seq):
    half = HEAD_DIM // 2
    inv = ROPE_BASE ** (-jnp.arange(half, dtype=F32) / half)
    ang = jnp.arange(seq, dtype=F32)[:, None] * inv[None, :]
    cos, sin = jnp.cos(ang), jnp.sin(ang)
    cos_t = jnp.concatenate([cos, cos], axis=-1)
    sin_t = jnp.concatenate([-sin, sin], axis=-1)
    cs = RET_CHUNK
    log_g = jnp.log1p(-jnp.exp2(-5.0 - jnp.arange(RET_HEADS, dtype=F32)))
    pos = jnp.arange(cs, dtype=F32)
    diff = pos[:, None] - pos[None, :]
    dm = jnp.where(diff >= 0, jnp.exp(log_g[:, None, None] * jnp.maximum(diff, 0.0)), 0.0)
    zeta = jnp.exp(log_g[:, None] * (cs - 1 - pos)[None, :])
    xi = jnp.exp(log_g[:, None] * (pos + 1)[None, :])
    g_chunk = jnp.broadcast_to(jnp.exp(log_g * cs)[:, None], (RET_HEADS, cs))
    aux = jnp.broadcast_to(jnp.stack([xi, zeta, g_chunk], axis=1)[..., None], (RET_HEADS, 3, cs, LANES))
    return cos_t, sin_t, dm, aux


def _s5_kernel(u_ref, are_ref, aim_ref, ls_ref, bre_ref, bim_ref, cre_ref, cim_ref, d_ref, y_ref,
               toep_ref, win_ref, woutt_ref, aq_ref, wf_ref, ucat_ref, inc_ref, x_ref, *, q):
    m = u_ref.shape[0] // q
    sl = STATE_LANES

    @pl.when(pl.program_id(1) == 0)
    def _():
        _s5_build_tables(are_ref, aim_ref, ls_ref, bre_ref, bim_ref, cre_ref, cim_ref,
                         toep_ref, win_ref, woutt_ref, aq_ref, wf_ref, q=q)

    for s in range(q):
        ucat_ref[:, s * LANES:(s + 1) * LANES] = u_ref[pl.ds(s, m, stride=q), :].astype(BF16)
    ucat = ucat_ref[...]
    inc_ref[...] = jnp.dot(ucat, win_ref[...], preferred_element_type=F32)
    a_re = aq_ref[0:1, :]
    a_im = aq_ref[1:2, :]

    def body(n, carry):
        x_re, x_im = carry
        x_ref[pl.ds(n, 1), 0:sl] = x_re
        x_ref[pl.ds(n, 1), sl:2 * sl] = x_im
        i_re = inc_ref[pl.ds(n, 1), 0:sl]
        i_im = inc_ref[pl.ds(n, 1), sl:2 * sl]
        return (a_re * x_re - a_im * x_im + i_re, a_re * x_im + a_im * x_re + i_im)

    zero = jnp.zeros((1, sl), F32)
    lax.fori_loop(0, m, body, (zero, zero))
    xb = x_ref[...].astype(BF16)
    nt = (((1,), (1,)), ((), ()))
    for t0 in range(0, q, S5_OUT_STEPS):
        hi = (t0 + S5_OUT_STEPS) * LANES
        cols = slice(t0 * LANES, hi)
        y = jnp.dot(ucat_ref[:, :hi], toep_ref[:hi, cols], preferred_element_type=F32)
        y = y + lax.dot_general(xb, woutt_ref[cols, :], nt, preferred_element_type=F32)
        for t in range(t0, t0 + S5_OUT_STEPS):
            rows = pl.ds(t, m, stride=q)
            y_ref[rows, :] = (y[:, (t - t0) * LANES:(t - t0 + 1) * LANES]
                              + d_ref[...] * u_ref[rows, :])


def s5_core(proj, a_re, a_im, b_re, b_im, c_re, c_im, log_step, d_skip, batch, seq, u_col):
    q = S5_CHUNK
    m = seq // q
    g, p = a_re.shape
    gb = GROUPS_PER_BLOCK
    nblk = g // gb
    sl = gb * p

    def rows(t):
        return t.astype(F32).reshape(nblk, 1, sl)

    def b_mat(t):
        return t.astype(F32).reshape(nblk, gb, p, S5_GROUP).transpose(0, 3, 1, 2).reshape(nblk, S5_GROUP, sl)

    def c_mat(t):
        return t.astype(F32).reshape(nblk, gb, S5_GROUP, p).transpose(0, 2, 1, 3).reshape(nblk, S5_GROUP, sl)

    row_spec = pl.BlockSpec((1, 1, sl), lambda j, b: (j, 0, 0))
    mat_spec = pl.BlockSpec((1, S5_GROUP, sl), lambda j, b: (j, 0, 0))
    return pl.pallas_call(
        functools.partial(_s5_kernel, q=q),
        grid=(nblk, batch),
        in_specs=[
            pl.BlockSpec((seq, LANES), lambda j, b: (b, u_col + j)),
            row_spec, row_spec, row_spec, mat_spec, mat_spec, mat_spec, mat_spec,
            pl.BlockSpec((1, LANES), lambda j, b: (0, j)),
        ],
        out_specs=pl.BlockSpec((seq, LANES), lambda j, b: (b, j)),
        out_shape=jax.ShapeDtypeStruct((batch * seq, nblk * LANES), F32),
        scratch_shapes=[
            pltpu.VMEM((q * LANES, q * LANES), BF16),
            pltpu.VMEM((q * LANES, 2 * sl), BF16),
            pltpu.VMEM((q * LANES, 2 * sl), BF16),
            pltpu.VMEM((2, sl), F32),
            pltpu.VMEM((q * LANES, 2 * sl), F32),
            pltpu.VMEM((m, q * LANES), BF16),
            pltpu.VMEM((m, 2 * sl), F32),
            pltpu.VMEM((m, 2 * sl), F32),
        ],
        compiler_params=_params("parallel", "arbitrary"),
        name="s5_core",
    )(proj, rows(a_re), rows(a_im), rows(log_step), b_mat(b_re), b_mat(b_im), c_mat(c_re), c_mat(c_im),
      d_skip)


def _dot_nt_split(x, y):
    nt = (((1,), (1,)), ((), ()))
    xh = x.astype(BF16)
    xl = (x - xh.astype(F32)).astype(BF16)
    yh = y.astype(BF16)
    yl = (y - yh.astype(F32)).astype(BF16)
    out = lax.dot_general(xh, yh, nt, preferred_element_type=F32)
    out = out + lax.dot_general(xh, yl, nt, preferred_element_type=F32)
    return out + lax.dot_general(xl, yh, nt, preferred_element_type=F32)


def _s5_build_tables(are_ref, aim_ref, ls_ref, bre_ref, bim_ref, cre_ref, cim_ref,
                     toep_ref, win_ref, woutt_ref, aq_ref, wf_ref, *, q):
    sl = STATE_LANES
    lam_re = jnp.minimum(are_ref[0], -1e-4)
    lam_im = aim_ref[0]
    step = jnp.exp(ls_ref[0])
    mag = jnp.exp(lam_re * step)
    a_re = mag * jnp.cos(lam_im * step)
    a_im = mag * jnp.sin(lam_im * step)
    den = lam_re * lam_re + lam_im * lam_im
    f_re = ((a_re - 1.0) * lam_re + a_im * lam_im) / den
    f_im = (a_im * lam_re - (a_re - 1.0) * lam_im) / den
    b_re = bre_ref[0]
    b_im = bim_ref[0]
    bb_re = f_re * b_re - f_im * b_im
    bb_im = f_re * b_im + f_im * b_re
    c_re = cre_ref[0]
    c_im = cim_ref[0]

    pows = [(jnp.ones((1, sl), F32), jnp.zeros((1, sl), F32))]
    for _ in range(q):
        p_re, p_im = pows[-1]
        pows.append((p_re * a_re - p_im * a_im, p_re * a_im + p_im * a_re))

    row_group = lax.shift_right_logical(lax.broadcasted_iota(jnp.int32, (LANES, sl), 0), S5_GROUP_SHIFT)
    col_group = lax.shift_right_logical(lax.broadcasted_iota(jnp.int32, (LANES, sl), 1), S5_STATE_SHIFT)
    same_group = row_group == col_group

    def tile(k, m_re, m_im, im_sign):
        p_re, p_im = pows[k]
        v_re = p_re * m_re - p_im * m_im
        v_im = (p_re * m_im + p_im * m_re) * im_sign
        e_re = jnp.where(same_group, jnp.concatenate([v_re] * GROUPS_PER_BLOCK, axis=0), 0.0)
        e_im = jnp.where(same_group, jnp.concatenate([v_im] * GROUPS_PER_BLOCK, axis=0), 0.0)
        return jnp.concatenate([e_re, e_im], axis=1)

    for s in range(q):
        rows = slice(s * LANES, (s + 1) * LANES)
        wf_ref[rows, :] = tile(q - 1 - s, bb_re, bb_im, 1.0)
        woutt_ref[rows, :] = tile(s + 1, c_re, c_im, -1.0).astype(BF16)
    win_ref[...] = wf_ref[...].astype(BF16)
    taps = _dot_nt_split(wf_ref[...], tile(0, c_re, c_im, -1.0))
    toep_ref[...] = jnp.zeros(toep_ref.shape, BF16)
    for s in range(q):
        for t in range(s, q):
            lag_rows = slice((q - 1 - (t - s)) * LANES, (q - (t - s)) * LANES)
            toep_ref[s * LANES:(s + 1) * LANES, t * LANES:(t + 1) * LANES] = taps[lag_rows, :].astype(BF16)
    aq_ref[0:1, :] = pows[q][0]
    aq_ref[1:2, :] = pows[q][1]


def _even_weights(w_in):
    fw = FOX_HEADS * HEAD_DIM
    rw = RET_HEADS * HEAD_DIM
    scale = HEAD_DIM ** -0.5
    c0 = 3 * fw
    c1 = c0 + FOX_HEADS
    fox_scale = jnp.concatenate([jnp.full((fw,), scale * LOG2E, F32), jnp.ones((2 * fw,), F32)])
    ret_scale = jnp.concatenate([jnp.ones((rw,), F32), jnp.full((rw,), scale, F32), jnp.ones((2 * rw,), F32)])
    w_t = jnp.swapaxes(w_in, 1, 2)
    w_fox = (w_t[:, :c0] * fox_scale[:, None]).astype(BF16)
    w_ret = (w_t[:, c1:] * ret_scale[:, None]).astype(BF16)
    w_forget = jnp.pad(w_t[:, c0:c1], ((0, 0), (0, LANES - FOX_HEADS), (0, 0))).astype(BF16)
    return w_fox, w_forget, w_ret


def _even_mixer(h, g, weights, e, b_forget, tables, batch, seq):
    w_fox, w_forget, w_ret = weights
    bias = jnp.pad(b_forget.astype(F32), (0, LANES - FOX_HEADS)).reshape(1, LANES)
    t = h.shape[0]
    tm = _row_tile(t, PROJ_ROWS)
    proj_fox, fl = norm_matmul(h, g, w_fox, e, 1, BF16, tm=tm, tn=PROJ_COLS, w_side=w_forget,
                               transposed=True)
    proj_ret = norm_matmul(h, g, w_ret, e, 1, BF16, tm=tm, tn=PROJ_COLS, transposed=True)
    call = fox_cumsum(fl, bias, batch, seq)
    nh = FOX_HEADS
    fox = fox_attention(proj_fox, call, batch, seq, 0, nh, 2 * nh, blk=_row_tile(seq, FOX_BLOCK),
                        hp=FOX_HEADS_PER_STEP)
    cos_t, sin_t, dm, aux = tables
    nr = RET_HEADS
    ret = retention(proj_ret, cos_t, sin_t, dm, aux, batch, seq, 0, nr, 2 * nr, 3 * nr,
                    rows=_row_tile(seq, RET_ROWS), hp=RET_HEADS_PER_STEP)
    return fox, ret


def _odd_mixer(h, g, w_in, o, a_re, a_im, b_re, b_im, c_re, c_im, d_skip, log_step, batch, seq):
    sw = a_re.shape[0] * S5_GROUP
    t = h.shape[0]
    proj = norm_matmul(h, g, w_in, o, 1, F32, tm=_row_tile(t, PROJ_ROWS), tn=PROJ_COLS)
    y = s5_core(proj, a_re, a_im, b_re, b_im, c_re, c_im, log_step, d_skip.reshape(1, sw).astype(F32),
                batch, seq, 0)
    return y, proj


def kernel(x, mem, norm_mix, norm_xattn, norm_mlp, norm_mem, norm_final, ab_w_in, ab_b_forget, ab_w_out,
           cd_w_in, s5_a_re, s5_a_im, s5_b_re, s5_b_im, s5_c_re, s5_c_im, s5_d, s5_log_step, s5_w_glu,
           conv_w, cd_w_out, xa_wq, xa_wkv, xa_wo, mlp_w1, mlp_w2):
    batch, seq, d = x.shape
    depth = norm_mix.shape[0]
    mlen = mem.shape[1]
    xw = xa_wq.shape[2]
    h = x.reshape(batch * seq, d)
    memf = mem.reshape(batch * mlen, d)
    tables = _retention_tables(seq)
    even_w = _even_weights(ab_w_in)
    odd_w_in = cd_w_in.astype(BF16)
    even_w_out = ab_w_out.astype(BF16)
    odd_w_out = cd_w_out.astype(BF16)
    w_glu = s5_w_glu.astype(BF16)
    conv_wf = conv_w.astype(F32)
    wq = (xa_wq * HEAD_DIM ** -0.5).astype(BF16)
    wo = xa_wo.astype(BF16)
    w1 = mlp_w1.astype(BF16)
    w2 = mlp_w2.astype(BF16)
    kv_all = norm_matmul(memf, norm_mem, xa_wkv.astype(BF16), 0, depth, BF16,
                         tm=_row_tile(batch * mlen, PROJ_ROWS), tn=PROJ_COLS)
    kv_all = kv_all.reshape(batch, mlen, depth * 2 * xw)
    for layer in range(depth):
        odd = layer % 2 == 1
        idx = layer // 2
        if odd:
            y, proj = _odd_mixer(h, norm_mix[layer], odd_w_in, idx, s5_a_re[idx], s5_a_im[idx],
                                 s5_b_re[idx], s5_b_im[idx], s5_c_re[idx], s5_c_im[idx], s5_d[idx],
                                 s5_log_step[idx], batch, seq)
            mixer_out = (y, proj, conv_wf, w_glu)
            w_out = odd_w_out
        else:
            mixer_out = _even_mixer(h, norm_mix[layer], even_w, idx, ab_b_forget[idx], tables, batch, seq)
            w_out = even_w_out
        h = mix_xattn_block(h, mixer_out, w_out, idx, norm_xattn[layer], wq, kv_all, wo, layer,
                            seq, tm=_row_tile(seq, MIX_ROWS), odd=odd)
        h = mlp_block(h, norm_mlp[layer], w1, w2, layer, norm_final, layer == depth - 1,
                      tm=_row_tile(batch * seq, MLP_ROWS), tf=MLP_COLS)
    return h.reshape(batch, seq, d)
```

```python
import functools
import math

import jax
import jax.numpy as jnp
from jax import lax
from jax.experimental import pallas as pl
from jax.experimental.pallas import tpu as pltpu

F32 = jnp.float32
BF16 = jnp.bfloat16

EPS = 1e-6
ROPE_BASE = 10000.0
LANES = 128
HEAD_DIM = 128
FOX_HEADS = 8
RET_HEADS = 8
RET_CHUNK = 256
XA_HEADS = 4
S5_GROUP = 16
S5_STATE = 64
S5_CHUNK = 16
S5_OUT_STEPS = 2
CONV_K = 3
S5_GROUP_SHIFT = S5_GROUP.bit_length() - 1
S5_STATE_SHIFT = S5_STATE.bit_length() - 1
GROUPS_PER_BLOCK = LANES // S5_GROUP
STATE_LANES = GROUPS_PER_BLOCK * S5_STATE
VMEM_LIMIT = 56 * 1024 * 1024
NEG_BIG = -1e30
LOG2E = math.log2(math.e)

PROJ_ROWS, PROJ_COLS = 1024, 1024
MIX_ROWS = 512
MLP_ROWS, MLP_COLS = 1024, 512
FOX_BLOCK = 1024
FOX_HEADS_PER_STEP = 2
RET_ROWS = 1024
RET_HEADS_PER_STEP = 8


def _row_tile(rows, want):
    tile = min(rows, want)
    while rows % tile:
        tile //= 2
    return tile


def _params(*sem):
    return pltpu.CompilerParams(dimension_semantics=sem, vmem_limit_bytes=VMEM_LIMIT)


def _rms(x, g):
    ms = jnp.mean(x * x, axis=-1, keepdims=True)
    return x * lax.rsqrt(ms + EPS) * g


def _norm_matmul_kernel(*refs, side, transposed):
    if side:
        x_ref, g_ref, w_ref, ws_ref, o_ref, os_ref, xn_ref = refs
    else:
        x_ref, g_ref, w_ref, o_ref, xn_ref = refs
    dims = (((1,), (1 if transposed else 0,)), ((), ()))

    @pl.when(pl.program_id(1) == 0)
    def _():
        xn_ref[...] = _rms(x_ref[...], g_ref[...]).astype(BF16)
        if side:
            os_ref[...] = lax.dot_general(xn_ref[...], ws_ref[...], dims, preferred_element_type=F32)

    o_ref[...] = lax.dot_general(xn_ref[...], w_ref[...], dims,
                                 preferred_element_type=F32).astype(o_ref.dtype)


def norm_matmul(x, g, w, layer, n_layers, out_dtype, tm, tn, w_side=None, transposed=False):
    t, d = x.shape
    n = w.shape[1 if transposed else 2]
    nj = n // tn

    def w_spec(cols, index):
        if transposed:
            return pl.BlockSpec((None, cols, d), lambda i, j: (index(j)[0], index(j)[1], 0))
        return pl.BlockSpec((None, d, cols), lambda i, j: (index(j)[0], 0, index(j)[1]))

    side = w_side is not None
    in_specs = [
        pl.BlockSpec((tm, d), lambda i, j: (i, 0)),
        pl.BlockSpec((1, d), lambda i, j: (0, 0)),
        w_spec(tn, lambda j: (layer + j // nj, j % nj)),
    ]
    out_specs = pl.BlockSpec((tm, tn), lambda i, j: (i, j))
    out_shape = jax.ShapeDtypeStruct((t, n_layers * n), out_dtype)
    args = (x, g.reshape(1, d), w)
    if side:
        ns = w_side.shape[1 if transposed else 2]
        in_specs.append(w_spec(ns, lambda j: (layer, 0)))
        out_specs = [out_specs, pl.BlockSpec((tm, ns), lambda i, j: (i, 0))]
        out_shape = [out_shape, jax.ShapeDtypeStruct((t, ns), F32)]
        args = args + (w_side,)
    return pl.pallas_call(
        functools.partial(_norm_matmul_kernel, side=side, transposed=transposed),
        grid=(t // tm, n_layers * nj),
        in_specs=in_specs,
        out_specs=out_specs,
        out_shape=out_shape,
        scratch_shapes=[pltpu.VMEM((tm, d), BF16)],
        compiler_params=_params("parallel", "arbitrary"),
        name="norm_matmul",
    )(*args)


def _mlp_kernel(h_ref, g_ref, w1_ref, w2_ref, gf_ref, o_ref, xn_ref, *, final_norm):
    f = pl.program_id(1)

    @pl.when(f == 0)
    def _():
        x = h_ref[...]
        xn_ref[...] = _rms(x, g_ref[...]).astype(BF16)
        o_ref[...] = x

    a = jnp.dot(xn_ref[...], w1_ref[...], preferred_element_type=F32)
    a = jnp.square(jnp.maximum(a, 0.0)).astype(BF16)
    o_ref[...] += jnp.dot(a, w2_ref[...], preferred_element_type=F32)

    if final_norm:
        @pl.when(f == pl.num_programs(1) - 1)
        def _():
            o_ref[...] = _rms(o_ref[...], gf_ref[...])


def mlp_block(h, g, w1, w2, layer, g_final, final_norm, tm, tf):
    t, d = h.shape
    dff = w1.shape[2]
    return pl.pallas_call(
        functools.partial(_mlp_kernel, final_norm=final_norm),
        grid=(t // tm, dff // tf),
        in_specs=[
            pl.BlockSpec((tm, d), lambda i, f: (i, 0)),
            pl.BlockSpec((1, d), lambda i, f: (0, 0)),
            pl.BlockSpec((None, d, tf), lambda i, f: (layer, 0, f)),
            pl.BlockSpec((None, tf, d), lambda i, f: (layer, f, 0)),
            pl.BlockSpec((1, d), lambda i, f: (0, 0)),
        ],
        out_specs=pl.BlockSpec((tm, d), lambda i, f: (i, 0)),
        out_shape=jax.ShapeDtypeStruct((t, d), F32),
        scratch_shapes=[pltpu.VMEM((tm, d), BF16)],
        compiler_params=_params("parallel", "arbitrary"),
        name="mlp_block",
    )(h, g.reshape(1, d), w1, w2, g_final.reshape(1, d))


def _s5_gate(y, w_glu):
    g = 0.5 * y * (1.0 + jnp.tanh(math.sqrt(2.0 / math.pi) * (y + 0.044715 * (y * y * y))))
    z = jnp.dot(g.astype(BF16), w_glu, preferred_element_type=F32)
    return (g * jax.nn.sigmoid(z)).astype(BF16)


def _short_conv(hc, gb, gc, z_prev, w_ref):
    z = gc * hc
    row = lax.broadcasted_iota(jnp.int32, z.shape, 0)
    y = w_ref[CONV_K - 1:CONV_K, :] * z
    for lag in range(1, CONV_K):
        zl = pltpu.roll(z, lag, 0)
        for r in range(lag):
            zl = jnp.where(row == r, z_prev[8 - lag + r:8 - lag + r + 1, :], zl)
        y = y + w_ref[CONV_K - 1 - lag:CONV_K - lag, :] * zl
    return (gb * y).astype(BF16)


def _mix_xattn_kernel(*refs, odd, per_batch):
    if odd:
        (h_ref, y_ref, hc_ref, gb_ref, gc_ref, hcp_ref, gcp_ref, cw_ref, wglu_ref,
         w1_ref, w2_ref, g_ref, wq_ref, k_ref, v_ref, wo_ref, o_ref) = refs
        a1 = _s5_gate(y_ref[...], wglu_ref[...])
        first = pl.program_id(0) % per_batch == 0
        z_prev = jnp.where(first, 0.0, gcp_ref[...] * hcp_ref[...])
        a2 = _short_conv(hc_ref[...], gb_ref[...], gc_ref[...], z_prev, cw_ref)
    else:
        h_ref, a1_ref, a2_ref, w1_ref, w2_ref, g_ref, wq_ref, k_ref, v_ref, wo_ref, o_ref = refs
        a1 = a1_ref[...]
        a2 = a2_ref[...]
    x = h_ref[...] + jnp.dot(a1, w1_ref[...], preferred_element_type=F32)
    x = x + jnp.dot(a2, w2_ref[...], preferred_element_type=F32)
    xn = _rms(x, g_ref[...]).astype(BF16)
    q = jnp.dot(xn, wq_ref[...], preferred_element_type=F32).astype(BF16)
    heads = []
    for hd in range(XA_HEADS):
        sl = slice(hd * HEAD_DIM, (hd + 1) * HEAD_DIM)
        s = lax.dot_general(q[:, sl], k_ref[0, :, sl], (((1,), (1,)), ((), ())),
                            preferred_element_type=F32)
        m = jnp.max(s, axis=-1, keepdims=True)
        p = jnp.exp(s - m)
        l = jnp.sum(p, axis=-1, keepdims=True)
        oh = jnp.dot(p.astype(BF16), v_ref[0, :, sl], preferred_element_type=F32)
        heads.append((oh * (1.0 / l)).astype(BF16))
    o = jnp.concatenate(heads, axis=-1)
    o_ref[...] = x + jnp.dot(o, wo_ref[...], preferred_element_type=F32)


def mix_xattn_block(h, mixer_inputs, w_out, mixer_idx, g, wq, kv, wo, layer, seq, tm, odd):
    t, d = h.shape
    k1 = w_out.shape[1] // 2
    k2 = k1
    mlen = kv.shape[1]
    xw = wq.shape[2]
    per_batch = seq // tm

    def fixed(shape, *index):
        return pl.BlockSpec(shape, lambda i: index, pipeline_mode=pl.Buffered(1))

    def rows(width, col=0):
        return pl.BlockSpec((tm, width), lambda i: (i, col))

    if odd:
        y, proj, conv_w, w_glu = mixer_inputs
        halo = tm // 8

        def prev_rows(col):
            return pl.BlockSpec((8, k2), lambda i: (jnp.maximum(i * halo - 1, 0), col))

        mixer_args = (y, proj, proj, proj, proj, proj, conv_w, w_glu)
        mixer_specs = [rows(k1), rows(k2, 1), rows(k2, 2), rows(k2, 3), prev_rows(1), prev_rows(3),
                       fixed((None,) + conv_w.shape[1:], mixer_idx, 0, 0),
                       fixed((None,) + w_glu.shape[1:], mixer_idx, 0, 0)]
    else:
        mixer_args = mixer_inputs
        mixer_specs = [rows(k1), rows(k2)]

    return pl.pallas_call(
        functools.partial(_mix_xattn_kernel, odd=odd, per_batch=per_batch),
        grid=(t // tm,),
        in_specs=[rows(d)] + mixer_specs + [
            fixed((None, k1, d), mixer_idx, 0, 0), fixed((None, k2, d), mixer_idx, 1, 0),
            fixed((1, d), 0, 0), fixed((None, d, xw), layer, 0, 0),
            pl.BlockSpec((1, mlen, xw), lambda i: (i // per_batch, 0, 2 * layer)),
            pl.BlockSpec((1, mlen, xw), lambda i: (i // per_batch, 0, 2 * layer + 1)),
            fixed((None, xw, d), layer, 0, 0),
        ],
        out_specs=rows(d),
        out_shape=jax.ShapeDtypeStruct((t, d), F32),
        compiler_params=_params("parallel"),
        name="mix_xattn_block",
    )(h, *mixer_args, w_out, w_out, g.reshape(1, d), wq, kv, kv, wo)


def _split3(x):
    hi = x.astype(BF16)
    r1 = x - hi.astype(F32)
    mid = r1.astype(BF16)
    lo = (r1 - mid.astype(F32)).astype(BF16)
    return hi, mid, lo


def _fox_cumsum_kernel(fl_ref, b_ref, call_ref, *, blk):
    seq = fl_ref.shape[0]
    row = lax.broadcasted_iota(jnp.int32, (blk, blk), 0)
    col = lax.broadcasted_iota(jnp.int32, (blk, blk), 1)
    tri = jnp.where(row >= col, 1.0, 0.0).astype(BF16)
    carry = jnp.zeros((1, LANES), F32)
    for i in range(seq // blk):
        rows = slice(i * blk, (i + 1) * blk)
        x = fl_ref[rows, :] + b_ref[...]
        ls = (jnp.minimum(x, 0.0) - jnp.log(1.0 + jnp.exp(-jnp.abs(x)))) * LOG2E
        c = carry
        for piece in _split3(ls):
            c = c + jnp.dot(tri, piece, preferred_element_type=F32)
        call_ref[rows, :] = c
        carry = c[blk - 1:blk, :]


def fox_cumsum(fl, bias, batch, seq):
    blk = min(256, seq)
    return pl.pallas_call(
        functools.partial(_fox_cumsum_kernel, blk=blk),
        grid=(batch,),
        in_specs=[
            pl.BlockSpec((seq, LANES), lambda b: (b, 0)),
            pl.BlockSpec((1, LANES), lambda b: (0, 0)),
        ],
        out_specs=pl.BlockSpec((seq, LANES), lambda b: (b, 0)),
        out_shape=jax.ShapeDtypeStruct((batch * seq, LANES), F32),
        compiler_params=_params("parallel"),
        name="fox_cumsum",
    )(fl, bias)


def _lane_tile(x, reps):
    return jnp.concatenate([x] * reps, axis=1)


def _gate_lanes(c_col, lane, own_first):
    hi, mid, lo = [p.astype(F32) for p in _split3(c_col)]
    base = 0 if own_first else 3
    pieces = jnp.where(lane == base, hi, jnp.where(lane == base + 1, mid, jnp.where(lane == base + 2, lo, 0.0)))
    ones = jnp.where((lane >= 3 - base) & (lane < 6 - base), 1.0, 0.0)
    return (pieces + ones).astype(BF16)


def _fox_attn_kernel(q_ref, k_ref, v_ref, c_ref, o_ref, qaug_ref, kaug_ref, vaug_ref, m_ref, acc_ref,
                     *, blk, hp):
    g = pl.program_id(1)
    i = pl.program_id(2)
    seq = k_ref.shape[0]
    nt = (((1,), (1,)), ((), ()))

    def head_cols(t):
        return slice(t * HEAD_DIM, (t + 1) * HEAD_DIM)

    @pl.when(i == 0)
    def _():
        lane = lax.broadcasted_iota(jnp.int32, (seq, LANES), 1)
        for t in range(hp):
            c_key = jnp.sum(jnp.where(lane == g * hp + t, c_ref[...], 0.0), axis=1, keepdims=True)
            kaug_ref[t, :, :HEAD_DIM] = k_ref[:, head_cols(t)]
            kaug_ref[t, :, HEAD_DIM:] = _gate_lanes(-c_key, lane, own_first=False)
            vaug_ref[t, :, :HEAD_DIM] = v_ref[:, head_cols(t)]
            vaug_ref[t, :, HEAD_DIM:] = jnp.ones((seq, LANES), BF16)

    lane = lax.broadcasted_iota(jnp.int32, (blk, LANES), 1)
    c_rows = c_ref[pl.ds(pl.multiple_of(i * blk, blk), blk), :]
    for t in range(hp):
        c_query = jnp.sum(jnp.where(lane == g * hp + t, c_rows, 0.0), axis=1, keepdims=True)
        qaug_ref[t, :, :HEAD_DIM] = q_ref[:, head_cols(t)]
        qaug_ref[t, :, HEAD_DIM:] = _gate_lanes(c_query, lane, own_first=True)
    m_ref[...] = jnp.full(m_ref.shape, NEG_BIG, F32)
    acc_ref[...] = jnp.zeros(acc_ref.shape, F32)

    def update(rows, keys, diag_shift):
        nr, nk = rows.stop - rows.start, keys.size
        scores = [lax.dot_general(qaug_ref[t, rows], kaug_ref[t, keys, :], nt,
                                  preferred_element_type=F32) for t in range(hp)]
        for t in range(hp):
            s = scores[t]
            if diag_shift is not None:
                r = lax.broadcasted_iota(jnp.int32, (nr, nk), 0)
                c = lax.broadcasted_iota(jnp.int32, (nr, nk), 1)
                s = jnp.where(r + diag_shift >= c, s, -jnp.inf)
            m_prev = m_ref[t, rows]
            m_new = jnp.maximum(m_prev, jnp.max(s, axis=-1, keepdims=True))
            alpha = jnp.exp2(m_prev - m_new)
            p = jnp.exp2(s - _lane_tile(m_new, nk // LANES)).astype(BF16)
            acc_ref[t, rows] = (_lane_tile(alpha, acc_ref.shape[2] // LANES) * acc_ref[t, rows]
                                + jnp.dot(p, vaug_ref[t, keys, :], preferred_element_type=F32))
            m_ref[t, rows] = m_new

    def body(j, carry):
        update(slice(0, blk), pl.ds(pl.multiple_of(j * blk, blk), blk), None)
        return carry

    lax.fori_loop(0, i, body, 0)
    half = blk // 2
    start = pl.multiple_of(i * blk, blk)
    update(slice(0, half), pl.ds(start, half), 0)
    update(slice(half, blk), pl.ds(start, blk), half)
    for t in range(hp):
        acc = acc_ref[t]
        o_ref[:, head_cols(t)] = (acc[:, :HEAD_DIM] * (1.0 / acc[:, HEAD_DIM:])).astype(o_ref.dtype)


def fox_attention(proj, call, batch, seq, q_col, k_col, v_col, blk, hp):
    nq = seq // blk
    width = hp * HEAD_DIM
    aug = HEAD_DIM + LANES
    return pl.pallas_call(
        functools.partial(_fox_attn_kernel, blk=blk, hp=hp),
        grid=(batch, FOX_HEADS // hp, nq),
        in_specs=[
            pl.BlockSpec((blk, width), lambda b, g, i: (b * nq + i, q_col // hp + g)),
            pl.BlockSpec((seq, width), lambda b, g, i: (b, k_col // hp + g)),
            pl.BlockSpec((seq, width), lambda b, g, i: (b, v_col // hp + g)),
            pl.BlockSpec((seq, LANES), lambda b, g, i: (b, 0)),
        ],
        out_specs=pl.BlockSpec((blk, width), lambda b, g, i: (b * nq + i, g)),
        out_shape=jax.ShapeDtypeStruct((batch * seq, FOX_HEADS * HEAD_DIM), BF16),
        scratch_shapes=[
            pltpu.VMEM((hp, blk, aug), BF16),
            pltpu.VMEM((hp, seq, aug), BF16),
            pltpu.VMEM((hp, seq, aug), BF16),
            pltpu.VMEM((hp, blk, LANES), F32),
            pltpu.VMEM((hp, blk, aug), F32),
        ],
        compiler_params=_params("parallel", "parallel", "arbitrary"),
        name="fox_attention",
    )(proj, proj, proj, call)


def _retention_kernel(q_ref, k_ref, v_ref, g_ref, cos_ref, sin_ref, dm_ref, aux_ref, o_ref,
                      r_ref, qr_ref, qx_ref, kr_ref, kz_ref, of_ref, *, cs, hp):
    rows_per_step = q_ref.shape[0]
    n_chunks = rows_per_step // cs
    half = HEAD_DIM // 2
    nt = (((1,), (1,)), ((), ()))

    @pl.when(pl.program_id(2) == 0)
    def _():
        r_ref[...] = jnp.zeros(r_ref.shape, F32)

    def head_cols(t):
        return slice(t * HEAD_DIM, (t + 1) * HEAD_DIM)

    cos = cos_ref[...]
    sin = sin_ref[...]
    for t in range(hp):
        q = q_ref[:, head_cols(t)].astype(F32)
        k = k_ref[:, head_cols(t)].astype(F32)
        qr = q * cos + pltpu.roll(q, half, 1) * sin
        kr = k * cos + pltpu.roll(k, half, 1) * sin
        xi = jnp.concatenate([aux_ref[t, 0]] * n_chunks, axis=0)
        zeta = jnp.concatenate([aux_ref[t, 1]] * n_chunks, axis=0)
        qr_ref[:, head_cols(t)] = qr.astype(BF16)
        qx_ref[:, head_cols(t)] = (qr * xi).astype(BF16)
        kr_ref[:, head_cols(t)] = kr.astype(BF16)
        kz_ref[:, head_cols(t)] = (kr * zeta).astype(BF16)

    def body(c, carry):
        rows = pl.ds(pl.multiple_of(c * cs, cs), cs)
        for t in range(hp):
            v = v_ref[rows, head_cols(t)]
            s = lax.dot_general(qr_ref[rows, head_cols(t)], kr_ref[rows, head_cols(t)], nt,
                                preferred_element_type=F32) * dm_ref[t]
            r = r_ref[t]
            lhs = jnp.concatenate([s.astype(BF16), qx_ref[rows, head_cols(t)]], axis=1)
            rhs = jnp.concatenate([v, r.astype(BF16)], axis=0)
            of_ref[rows, head_cols(t)] = jnp.dot(lhs, rhs, preferred_element_type=F32)
            r_ref[t] = aux_ref[t, 2, :HEAD_DIM] * r + lax.dot_general(
                kz_ref[rows, head_cols(t)], v, (((0,), (0,)), ((), ())), preferred_element_type=F32)
        return carry

    lax.fori_loop(0, n_chunks, body, 0)

    for t in range(hp):
        o = of_ref[:, head_cols(t)]
        mu = jnp.mean(o, axis=-1, keepdims=True)
        oc = o - mu
        var = jnp.mean(oc * oc, axis=-1, keepdims=True)
        on = oc * lax.rsqrt(var + EPS)
        g = g_ref[:, head_cols(t)].astype(F32)
        o_ref[:, head_cols(t)] = (g * jax.nn.sigmoid(g) * on).astype(o_ref.dtype)


def retention(proj, cos, sin, dm, aux, batch, seq, q_col, k_col, v_col, g_col, rows, hp):
    width = hp * HEAD_DIM
    nr = seq // rows

    def col(c0):
        return pl.BlockSpec((rows, width), lambda b, g, i: (b * nr + i, c0 // hp + g))

    return pl.pallas_call(
        functools.partial(_retention_kernel, cs=RET_CHUNK, hp=hp),
        grid=(batch, RET_HEADS // hp, nr),
        in_specs=[
            col(q_col), col(k_col), col(v_col), col(g_col),
            pl.BlockSpec((rows, HEAD_DIM), lambda b, g, i: (i, 0)),
            pl.BlockSpec((rows, HEAD_DIM), lambda b, g, i: (i, 0)),
            pl.BlockSpec((hp, RET_CHUNK, RET_CHUNK), lambda b, g, i: (g, 0, 0)),
            pl.BlockSpec((hp, 3, RET_CHUNK, LANES), lambda b, g, i: (g, 0, 0, 0)),
        ],
        out_specs=pl.BlockSpec((rows, width), lambda b, g, i: (b * nr + i, g)),
        out_shape=jax.ShapeDtypeStruct((batch * seq, RET_HEADS * HEAD_DIM), BF16),
        scratch_shapes=[
            pltpu.VMEM((hp, HEAD_DIM, HEAD_DIM), F32),
            pltpu.VMEM((rows, width), BF16),
            pltpu.VMEM((rows, width), BF16),
            pltpu.VMEM((rows, width), BF16),
            pltpu.VMEM((rows, width), BF16),
            pltpu.VMEM((rows, width), F32),
        ],
        compiler_params=_params("parallel", "parallel", "arbitrary"),
        name="retention",
    )(proj, proj, proj, proj, cos, sin, dm, aux)


def _retention_tables(seq):
    half = HEAD_DIM // 2
    inv = ROPE_BASE ** (-jnp.arange(half, dtype=F32) / half)
    ang = jnp.arange(seq, dtype=F32)[:, None] * inv[None, :]
    cos, sin = jnp.cos(ang), jnp.sin(ang)
    cos_t = jnp.concatenate([cos, cos], axis=-1)
    sin_t = jnp.concatenate([-sin, sin], axis=-1)
    cs = RET_CHUNK
    log_g = jnp.log1p(-jnp.exp2(-5.0 - jnp.arange(RET_HEADS, dtype=F32)))
    pos = jnp.arange(cs, dtype=F32)
    diff = pos[:, None] - pos[None, :]
    dm = jnp.where(diff >= 0, jnp.exp(log_g[:, None, None] * jnp.maximum(diff, 0.0)), 0.0)
    zeta = jnp.exp(log_g[:, None] * (cs - 1 - pos)[None, :])
    xi = jnp.exp(log_g[:, None] * (pos + 1)[None, :])
    g_chunk = jnp.broadcast_to(jnp.exp(log_g * cs)[:, None], (RET_HEADS, cs))
    aux = jnp.broadcast_to(jnp.stack([xi, zeta, g_chunk], axis=1)[..., None], (RET_HEADS, 3, cs, LANES))
    return cos_t, sin_t, dm, aux


def _s5_kernel(u_ref, are_ref, aim_ref, ls_ref, bre_ref, bim_ref, cre_ref, cim_ref, d_ref, y_ref,
               toep_ref, win_ref, woutt_ref, aq_ref, wf_ref, ucat_ref, inc_ref, x_ref, *, q):
    m = u_ref.shape[0] // q
    sl = STATE_LANES

    @pl.when(pl.program_id(1) == 0)
    def _():
        _s5_build_tables(are_ref, aim_ref, ls_ref, bre_ref, bim_ref, cre_ref, cim_ref,
                         toep_ref, win_ref, woutt_ref, aq_ref, wf_ref, q=q)

    for s in range(q):
        ucat_ref[:, s * LANES:(s + 1) * LANES] = u_ref[pl.ds(s, m, stride=q), :].astype(BF16)
    ucat = ucat_ref[...]
    inc_ref[...] = jnp.dot(ucat, win_ref[...], preferred_element_type=F32)
    a_re = aq_ref[0:1, :]
    a_im = aq_ref[1:2, :]

    def body(n, carry):
        x_re, x_im = carry
        x_ref[pl.ds(n, 1), 0:sl] = x_re
        x_ref[pl.ds(n, 1), sl:2 * sl] = x_im
        i_re = inc_ref[pl.ds(n, 1), 0:sl]
        i_im = inc_ref[pl.ds(n, 1), sl:2 * sl]
        return (a_re * x_re - a_im * x_im + i_re, a_re * x_im + a_im * x_re + i_im)

    zero = jnp.zeros((1, sl), F32)
    lax.fori_loop(0, m, body, (zero, zero))
    xb = x_ref[...].astype(BF16)
    nt = (((1,), (1,)), ((), ()))
    for t0 in range(0, q, S5_OUT_STEPS):
        hi = (t0 + S5_OUT_STEPS) * LANES
        cols = slice(t0 * LANES, hi)
        y = jnp.dot(ucat_ref[:, :hi], toep_ref[:hi, cols], preferred_element_type=F32)
        y = y + lax.dot_general(xb, woutt_ref[cols, :], nt, preferred_element_type=F32)
        for t in range(t0, t0 + S5_OUT_STEPS):
            rows = pl.ds(t, m, stride=q)
            y_ref[rows, :] = (y[:, (t - t0) * LANES:(t - t0 + 1) * LANES]
                              + d_ref[...] * u_ref[rows, :])


def s5_core(proj, a_re, a_im, b_re, b_im, c_re, c_im, log_step, d_skip, batch, seq, u_col):
    q = S5_CHUNK
    m = seq // q
    g, p = a_re.shape
    gb = GROUPS_PER_BLOCK
    nblk = g // gb
    sl = gb * p

    def rows(t):
        return t.astype(F32).reshape(nblk, 1, sl)

    def b_mat(t):
        return t.astype(F32).reshape(nblk, gb, p, S5_GROUP).transpose(0, 3, 1, 2).reshape(nblk, S5_GROUP, sl)

    def c_mat(t):
        return t.astype(F32).reshape(nblk, gb, S5_GROUP, p).transpose(0, 2, 1, 3).reshape(nblk, S5_GROUP, sl)

    row_spec = pl.BlockSpec((1, 1, sl), lambda j, b: (j, 0, 0))
    mat_spec = pl.BlockSpec((1, S5_GROUP, sl), lambda j, b: (j, 0, 0))
    return pl.pallas_call(
        functools.partial(_s5_kernel, q=q),
        grid=(nblk, batch),
        in_specs=[
            pl.BlockSpec((seq, LANES), lambda j, b: (b, u_col + j)),
            row_spec, row_spec, row_spec, mat_spec, mat_spec, mat_spec, mat_spec,
            pl.BlockSpec((1, LANES), lambda j, b: (0, j)),
        ],
        out_specs=pl.BlockSpec((seq, LANES), lambda j, b: (b, j)),
        out_shape=jax.ShapeDtypeStruct((batch * seq, nblk * LANES), F32),
        scratch_shapes=[
            pltpu.VMEM((q * LANES, q * LANES), BF16),
            pltpu.VMEM((q * LANES, 2 * sl), BF16),
            pltpu.VMEM((q * LANES, 2 * sl), BF16),
            pltpu.VMEM((2, sl), F32),
            pltpu.VMEM((q * LANES, 2 * sl), F32),
            pltpu.VMEM((m, q * LANES), BF16),
            pltpu.VMEM((m, 2 * sl), F32),
            pltpu.VMEM((m, 2 * sl), F32),
        ],
        compiler_params=_params("parallel", "arbitrary"),
        name="s5_core",
    )(proj, rows(a_re), rows(a_im), rows(log_step), b_mat(b_re), b_mat(b_im), c_mat(c_re), c_mat(c_im),
      d_skip)


def _dot_nt_split(x, y):
    nt = (((1,), (1,)), ((), ()))
    xh = x.astype(BF16)
    xl = (x - xh.astype(F32)).astype(BF16)
    yh = y.astype(BF16)
    yl = (y - yh.astype(F32)).astype(BF16)
    out = lax.dot_general(xh, yh, nt, preferred_element_type=F32)
    out = out + lax.dot_general(xh, yl, nt, preferred_element_type=F32)
    return out + lax.dot_general(xl, yh, nt, preferred_element_type=F32)


def _s5_build_tables(are_ref, aim_ref, ls_ref, bre_ref, bim_ref, cre_ref, cim_ref,
                     toep_ref, win_ref, woutt_ref, aq_ref, wf_ref, *, q):
    sl = STATE_LANES
    lam_re = jnp.minimum(are_ref[0], -1e-4)
    lam_im = aim_ref[0]
    step = jnp.exp(ls_ref[0])
    mag = jnp.exp(lam_re * step)
    a_re = mag * jnp.cos(lam_im * step)
    a_im = mag * jnp.sin(lam_im * step)
    den = lam_re * lam_re + lam_im * lam_im
    f_re = ((a_re - 1.0) * lam_re + a_im * lam_im) / den
    f_im = (a_im * lam_re - (a_re - 1.0) * lam_im) / den
    b_re = bre_ref[0]
    b_im = bim_ref[0]
    bb_re = f_re * b_re - f_im * b_im
    bb_im = f_re * b_im + f_im * b_re
    c_re = cre_ref[0]
    c_im = cim_ref[0]

    pows = [(jnp.ones((1, sl), F32), jnp.zeros((1, sl), F32))]
    for _ in range(q):
        p_re, p_im = pows[-1]
        pows.append((p_re * a_re - p_im * a_im, p_re * a_im + p_im * a_re))

    row_group = lax.shift_right_logical(lax.broadcasted_iota(jnp.int32, (LANES, sl), 0), S5_GROUP_SHIFT)
    col_group = lax.shift_right_logical(lax.broadcasted_iota(jnp.int32, (LANES, sl), 1), S5_STATE_SHIFT)
    same_group = row_group == col_group

    def tile(k, m_re, m_im, im_sign):
        p_re, p_im = pows[k]
        v_re = p_re * m_re - p_im * m_im
        v_im = (p_re * m_im + p_im * m_re) * im_sign
        e_re = jnp.where(same_group, jnp.concatenate([v_re] * GROUPS_PER_BLOCK, axis=0), 0.0)
        e_im = jnp.where(same_group, jnp.concatenate([v_im] * GROUPS_PER_BLOCK, axis=0), 0.0)
        return jnp.concatenate([e_re, e_im], axis=1)

    for s in range(q):
        rows = slice(s * LANES, (s + 1) * LANES)
        wf_ref[rows, :] = tile(q - 1 - s, bb_re, bb_im, 1.0)
        woutt_ref[rows, :] = tile(s + 1, c_re, c_im, -1.0).astype(BF16)
    win_ref[...] = wf_ref[...].astype(BF16)
    taps = _dot_nt_split(wf_ref[...], tile(0, c_re, c_im, -1.0))
    toep_ref[...] = jnp.zeros(toep_ref.shape, BF16)
    for s in range(q):
        for t in range(s, q):
            lag_rows = slice((q - 1 - (t - s)) * LANES, (q - (t - s)) * LANES)
            toep_ref[s * LANES:(s + 1) * LANES, t * LANES:(t + 1) * LANES] = taps[lag_rows, :].astype(BF16)
    aq_ref[0:1, :] = pows[q][0]
    aq_ref[1:2, :] = pows[q][1]


def _even_weights(w_in):
    fw = FOX_HEADS * HEAD_DIM
    rw = RET_HEADS * HEAD_DIM
    scale = HEAD_DIM ** -0.5
    c0 = 3 * fw
    c1 = c0 + FOX_HEADS
    fox_scale = jnp.concatenate([jnp.full((fw,), scale * LOG2E, F32), jnp.ones((2 * fw,), F32)])
    ret_scale = jnp.concatenate([jnp.ones((rw,), F32), jnp.full((rw,), scale, F32), jnp.ones((2 * rw,), F32)])
    w_t = jnp.swapaxes(w_in, 1, 2)
    w_main = jnp.concatenate([w_t[:, :c0] * fox_scale[:, None], w_t[:, c1:] * ret_scale[:, None]],
                             axis=1).astype(BF16)
    w_forget = jnp.pad(w_t[:, c0:c1], ((0, 0), (0, LANES - FOX_HEADS), (0, 0))).astype(BF16)
    return w_main, w_forget


def _even_mixer(h, g, weights, e, b_forget, tables, batch, seq):
    w_main, w_forget = weights
    bias = jnp.pad(b_forget.astype(F32), (0, LANES - FOX_HEADS)).reshape(1, LANES)
    t = h.shape[0]
    proj, fl = norm_matmul(h, g, w_main, e, 1, BF16, tm=_row_tile(t, PROJ_ROWS), tn=PROJ_COLS,
                           w_side=w_forget, transposed=True)
    call = fox_cumsum(fl, bias, batch, seq)
    nh = FOX_HEADS
    fox = fox_attention(proj, call, batch, seq, 0, nh, 2 * nh, blk=_row_tile(seq, FOX_BLOCK),
                        hp=FOX_HEADS_PER_STEP)
    cos_t, sin_t, dm, aux = tables
    ret = retention(proj, cos_t, sin_t, dm, aux, batch, seq, 3 * nh, 4 * nh, 5 * nh, 6 * nh,
                    rows=_row_tile(seq, RET_ROWS), hp=RET_HEADS_PER_STEP)
    return fox, ret


def _odd_mixer(h, g, w_in, o, a_re, a_im, b_re, b_im, c_re, c_im, d_skip, log_step, batch, seq):
    sw = a_re.shape[0] * S5_GROUP
    t = h.shape[0]
    proj = norm_matmul(h, g, w_in, o, 1, F32, tm=_row_tile(t, PROJ_ROWS), tn=PROJ_COLS)
    y = s5_core(proj, a_re, a_im, b_re, b_im, c_re, c_im, log_step, d_skip.reshape(1, sw).astype(F32),
                batch, seq, 0)
    return y, proj


def kernel(x, mem, norm_mix, norm_xattn, norm_mlp, norm_mem, norm_final, ab_w_in, ab_b_forget, ab_w_out,
           cd_w_in, s5_a_re, s5_a_im, s5_b_re, s5_b_im, s5_c_re, s5_c_im, s5_d, s5_log_step, s5_w_glu,
           conv_w, cd_w_out, xa_wq, xa_wkv, xa_wo, mlp_w1, mlp_w2):
    batch, seq, d = x.shape
    depth = norm_mix.shape[0]
    mlen = mem.shape[1]
    xw = xa_wq.shape[2]
    h = x.reshape(batch * seq, d)
    memf = mem.reshape(batch * mlen, d)
    tables = _retention_tables(seq)
    even_w = _even_weights(ab_w_in)
    odd_w_in = cd_w_in.astype(BF16)
    even_w_out = ab_w_out.astype(BF16)
    odd_w_out = cd_w_out.astype(BF16)
    w_glu = s5_w_glu.astype(BF16)
    conv_wf = conv_w.astype(F32)
    wq = (xa_wq * HEAD_DIM ** -0.5).astype(BF16)
    wo = xa_wo.astype(BF16)
    w1 = mlp_w1.astype(BF16)
    w2 = mlp_w2.astype(BF16)
    kv_all = norm_matmul(memf, norm_mem, xa_wkv.astype(BF16), 0, depth, BF16,
                         tm=_row_tile(batch * mlen, PROJ_ROWS), tn=PROJ_COLS)
    kv_all = kv_all.reshape(batch, mlen, depth * 2 * xw)
    for layer in range(depth):
        odd = layer % 2 == 1
        idx = layer // 2
        if odd:
            y, proj = _odd_mixer(h, norm_mix[layer], odd_w_in, idx, s5_a_re[idx], s5_a_im[idx],
                                 s5_b_re[idx], s5_b_im[idx], s5_c_re[idx], s5_c_im[idx], s5_d[idx],
                                 s5_log_step[idx], batch, seq)
            mixer_out = (y, proj, conv_wf, w_glu)
            w_out = odd_w_out
        else:
            mixer_out = _even_mixer(h, norm_mix[layer], even_w, idx, ab_b_forget[idx], tables, batch, seq)
            w_out = even_w_out
        h = mix_xattn_block(h, mixer_out, w_out, idx, norm_xattn[layer], wq, kv_all, wo, layer,
                            seq, tm=_row_tile(seq, MIX_ROWS), odd=odd)
        h = mlp_block(h, norm_mlp[layer], w1, w2, layer, norm_final, layer == depth - 1,
                      tm=_row_tile(batch * seq, MLP_ROWS), tf=MLP_COLS)
    return h.reshape(batch, seq, d)
```

```python
import functools
import math

import jax
import jax.numpy as jnp
from jax import lax
from jax.experimental import pallas as pl
from jax.experimental.pallas import tpu as pltpu

F32 = jnp.float32
BF16 = jnp.bfloat16

EPS = 1e-6
ROPE_BASE = 10000.0
LANES = 128
SUBLANES = 8
HEAD_DIM = 128
FOX_HEADS = 8
RET_HEADS = 8
RET_CHUNK = 256
XA_HEADS = 4
S5_GROUP = 16
S5_STATE = 64
S5_CHUNK = 16
MXU_TILE = 256
S5_OUT_STEPS = MXU_TILE // LANES
CONV_K = 3
S5_GROUP_SHIFT = S5_GROUP.bit_length() - 1
S5_STATE_SHIFT = S5_STATE.bit_length() - 1
GROUPS_PER_BLOCK = LANES // S5_GROUP
STATE_LANES = GROUPS_PER_BLOCK * S5_STATE
VMEM_LIMIT = 56 * 1024 * 1024
NEG_BIG = -1e30
LOG2E = math.log2(math.e)

PROJ_ROWS, PROJ_COLS = 1024, 1024
MIX_ROWS = 512
MLP_ROWS, MLP_COLS = 1024, 512
FOX_BLOCK = 1024
FOX_HEADS_PER_STEP = 2
RET_ROWS = 1024
RET_HEADS_PER_STEP = 8


def _row_tile(rows, want):
    tile = min(rows, want)
    while rows % tile:
        tile //= 2
    return tile


def _params(*sem):
    return pltpu.CompilerParams(dimension_semantics=sem, vmem_limit_bytes=VMEM_LIMIT)


def _rms(x, g):
    ms = jnp.mean(x * x, axis=-1, keepdims=True)
    return x * lax.rsqrt(ms + EPS) * g


def _norm_matmul_kernel(*refs, side, transposed):
    if side:
        x_ref, g_ref, w_ref, ws_ref, o_ref, os_ref, xn_ref = refs
    else:
        x_ref, g_ref, w_ref, o_ref, xn_ref = refs
    dims = (((1,), (1 if transposed else 0,)), ((), ()))

    @pl.when(pl.program_id(1) == 0)
    def _():
        xn_ref[...] = _rms(x_ref[...], g_ref[...]).astype(BF16)
        if side:
            os_ref[...] = lax.dot_general(xn_ref[...], ws_ref[...], dims, preferred_element_type=F32)

    o_ref[...] = lax.dot_general(xn_ref[...], w_ref[...].astype(BF16), dims,
                                 preferred_element_type=F32).astype(o_ref.dtype)


def norm_matmul(x, g, w, layer, n_layers, out_dtype, tm, tn, w_side=None, transposed=False):
    t, d = x.shape
    n = w.shape[1 if transposed else 2]
    nj = n // tn

    def w_spec(cols, index):
        if transposed:
            return pl.BlockSpec((None, cols, d), lambda i, j: (index(j)[0], index(j)[1], 0))
        return pl.BlockSpec((None, d, cols), lambda i, j: (index(j)[0], 0, index(j)[1]))

    side = w_side is not None
    in_specs = [
        pl.BlockSpec((tm, d), lambda i, j: (i, 0)),
        pl.BlockSpec((1, d), lambda i, j: (0, 0)),
        w_spec(tn, lambda j: (layer + j // nj, j % nj)),
    ]
    out_specs = pl.BlockSpec((tm, tn), lambda i, j: (i, j))
    out_shape = jax.ShapeDtypeStruct((t, n_layers * n), out_dtype)
    args = (x, g.reshape(1, d), w)
    if side:
        ns = w_side.shape[1 if transposed else 2]
        in_specs.append(w_spec(ns, lambda j: (layer, 0)))
        out_specs = [out_specs, pl.BlockSpec((tm, ns), lambda i, j: (i, 0))]
        out_shape = [out_shape, jax.ShapeDtypeStruct((t, ns), F32)]
        args = args + (w_side,)
    return pl.pallas_call(
        functools.partial(_norm_matmul_kernel, side=side, transposed=transposed),
        grid=(t // tm, n_layers * nj),
        in_specs=in_specs,
        out_specs=out_specs,
        out_shape=out_shape,
        scratch_shapes=[pltpu.VMEM((tm, d), BF16)],
        compiler_params=_params("parallel", "arbitrary"),
        name="norm_matmul",
    )(*args)


def _mlp_kernel(h_ref, g_ref, w1_ref, w2_ref, gf_ref, o_ref, xn_ref, *, final_norm):
    f = pl.program_id(1)

    @pl.when(f == 0)
    def _():
        x = h_ref[...]
        xn_ref[...] = _rms(x, g_ref[...]).astype(BF16)
        o_ref[...] = x

    a = jnp.dot(xn_ref[...], w1_ref[...].astype(BF16), preferred_element_type=F32)
    a = jnp.square(jnp.maximum(a, 0.0)).astype(BF16)
    o_ref[...] += jnp.dot(a, w2_ref[...].astype(BF16), preferred_element_type=F32)

    if final_norm:
        @pl.when(f == pl.num_programs(1) - 1)
        def _():
            o_ref[...] = _rms(o_ref[...], gf_ref[...])


def mlp_block(h, g, w1, w2, layer, g_final, final_norm, tm, tf):
    t, d = h.shape
    dff = w1.shape[2]
    return pl.pallas_call(
        functools.partial(_mlp_kernel, final_norm=final_norm),
        grid=(t // tm, dff // tf),
        in_specs=[
            pl.BlockSpec((tm, d), lambda i, f: (i, 0)),
            pl.BlockSpec((1, d), lambda i, f: (0, 0)),
            pl.BlockSpec((None, d, tf), lambda i, f: (layer, 0, f)),
            pl.BlockSpec((None, tf, d), lambda i, f: (layer, f, 0)),
            pl.BlockSpec((1, d), lambda i, f: (0, 0)),
        ],
        out_specs=pl.BlockSpec((tm, d), lambda i, f: (i, 0)),
        out_shape=jax.ShapeDtypeStruct((t, d), F32),
        scratch_shapes=[pltpu.VMEM((tm, d), BF16)],
        compiler_params=_params("parallel", "arbitrary"),
        name="mlp_block",
    )(h, g.reshape(1, d), w1, w2, g_final.reshape(1, d))


def _s5_gate(y, w_glu):
    g = 0.5 * y * (1.0 + jnp.tanh(math.sqrt(2.0 / math.pi) * (y + 0.044715 * (y * y * y))))
    z = jnp.dot(g.astype(BF16), w_glu, preferred_element_type=F32)
    return (g * jax.nn.sigmoid(z)).astype(BF16)


def _short_conv(hc, gb, gc, z_prev, w_ref):
    z = gc * hc
    row = lax.broadcasted_iota(jnp.int32, z.shape, 0)
    y = w_ref[CONV_K - 1:CONV_K, :] * z
    for lag in range(1, CONV_K):
        zl = pltpu.roll(z, lag, 0)
        for r in range(lag):
            src = SUBLANES - lag + r
            zl = jnp.where(row == r, z_prev[src:src + 1, :], zl)
        y = y + w_ref[CONV_K - 1 - lag:CONV_K - lag, :] * zl
    return (gb * y).astype(BF16)


def _mix_xattn_kernel(*refs, odd, per_batch):
    if odd:
        (h_ref, y_ref, hc_ref, gb_ref, gc_ref, hcp_ref, gcp_ref, cw_ref, wglu_ref,
         w1_ref, w2_ref, g_ref, wq_ref, k_ref, v_ref, wo_ref, o_ref) = refs
        a1 = _s5_gate(y_ref[...], wglu_ref[...])
        first = pl.program_id(0) % per_batch == 0
        z_prev = jnp.where(first, 0.0, gcp_ref[...] * hcp_ref[...])
        a2 = _short_conv(hc_ref[...], gb_ref[...], gc_ref[...], z_prev, cw_ref)
    else:
        h_ref, a1_ref, a2_ref, w1_ref, w2_ref, g_ref, wq_ref, k_ref, v_ref, wo_ref, o_ref = refs
        a1 = a1_ref[...]
        a2 = a2_ref[...]
    x = h_ref[...] + jnp.dot(a1, w1_ref[...], preferred_element_type=F32)
    x = x + jnp.dot(a2, w2_ref[...], preferred_element_type=F32)
    xn = _rms(x, g_ref[...]).astype(BF16)
    q = jnp.dot(xn, wq_ref[...], preferred_element_type=F32).astype(BF16)
    heads = []
    for hd in range(XA_HEADS):
        sl = slice(hd * HEAD_DIM, (hd + 1) * HEAD_DIM)
        s = lax.dot_general(q[:, sl], k_ref[0, :, sl], (((1,), (1,)), ((), ())),
                            preferred_element_type=F32)
        m = jnp.max(s, axis=-1, keepdims=True)
        p = jnp.exp(s - m)
        l = jnp.sum(p, axis=-1, keepdims=True)
        oh = jnp.dot(p.astype(BF16), v_ref[0, :, sl], preferred_element_type=F32)
        heads.append((oh * (1.0 / l)).astype(BF16))
    o = jnp.concatenate(heads, axis=-1)
    o_ref[...] = x + jnp.dot(o, wo_ref[...], preferred_element_type=F32)


def mix_xattn_block(h, mixer_inputs, w_out, mixer_idx, g, wq, kv, wo, layer, seq, tm, odd):
    t, d = h.shape
    k1 = w_out.shape[1] // 2
    k2 = k1
    mlen = kv.shape[1]
    xw = wq.shape[2]
    per_batch = seq // tm

    def fixed(shape, *index):
        return pl.BlockSpec(shape, lambda i: index, pipeline_mode=pl.Buffered(1))

    def rows(width, col=0):
        return pl.BlockSpec((tm, width), lambda i: (i, col))

    if odd:
        y, proj, conv_w, w_glu = mixer_inputs
        halo = tm // SUBLANES

        def prev_rows(col):
            return pl.BlockSpec((SUBLANES, k2), lambda i: (jnp.maximum(i * halo - 1, 0), col))

        mixer_args = (y, proj, proj, proj, proj, proj, conv_w, w_glu)
        mixer_specs = [rows(k1), rows(k2, 1), rows(k2, 2), rows(k2, 3), prev_rows(1), prev_rows(3),
                       fixed((None,) + conv_w.shape[1:], mixer_idx, 0, 0),
                       fixed((None,) + w_glu.shape[1:], mixer_idx, 0, 0)]
    else:
        mixer_args = mixer_inputs
        mixer_specs = [rows(k1), rows(k2)]

    return pl.pallas_call(
        functools.partial(_mix_xattn_kernel, odd=odd, per_batch=per_batch),
        grid=(t // tm,),
        in_specs=[rows(d)] + mixer_specs + [
            fixed((None, k1, d), mixer_idx, 0, 0), fixed((None, k2, d), mixer_idx, 1, 0),
            fixed((1, d), 0, 0), fixed((None, d, xw), layer, 0, 0),
            pl.BlockSpec((1, mlen, xw), lambda i: (i // per_batch, 0, 2 * layer)),
            pl.BlockSpec((1, mlen, xw), lambda i: (i // per_batch, 0, 2 * layer + 1)),
            fixed((None, xw, d), layer, 0, 0),
        ],
        out_specs=rows(d),
        out_shape=jax.ShapeDtypeStruct((t, d), F32),
        compiler_params=_params("parallel"),
        name="mix_xattn_block",
    )(h, *mixer_args, w_out, w_out, g.reshape(1, d), wq, kv, kv, wo)


def _split3(x):
    hi = x.astype(BF16)
    r1 = x - hi.astype(F32)
    mid = r1.astype(BF16)
    lo = (r1 - mid.astype(F32)).astype(BF16)
    return hi, mid, lo


def _fox_cumsum_kernel(fl_ref, b_ref, call_ref, *, blk):
    seq = fl_ref.shape[0]
    row = lax.broadcasted_iota(jnp.int32, (blk, blk), 0)
    col = lax.broadcasted_iota(jnp.int32, (blk, blk), 1)
    tri = jnp.where(row >= col, 1.0, 0.0).astype(BF16)
    carry = jnp.zeros((1, LANES), F32)
    for i in range(seq // blk):
        rows = slice(i * blk, (i + 1) * blk)
        x = fl_ref[rows, :] + b_ref[...]
        ls = (jnp.minimum(x, 0.0) - jnp.log(1.0 + jnp.exp(-jnp.abs(x)))) * LOG2E
        c = carry
        for piece in _split3(ls):
            c = c + jnp.dot(tri, piece, preferred_element_type=F32)
        call_ref[rows, :] = c
        carry = c[blk - 1:blk, :]


def fox_cumsum(fl, bias, batch, seq):
    blk = min(MXU_TILE, seq)
    return pl.pallas_call(
        functools.partial(_fox_cumsum_kernel, blk=blk),
        grid=(batch,),
        in_specs=[
            pl.BlockSpec((seq, LANES), lambda b: (b, 0)),
            pl.BlockSpec((1, LANES), lambda b: (0, 0)),
        ],
        out_specs=pl.BlockSpec((seq, LANES), lambda b: (b, 0)),
        out_shape=jax.ShapeDtypeStruct((batch * seq, LANES), F32),
        compiler_params=_params("parallel"),
        name="fox_cumsum",
    )(fl, bias)


def _lane_tile(x, reps):
    return jnp.concatenate([x] * reps, axis=1)


def _gate_lanes(c_col, lane, own_first):
    hi, mid, lo = [p.astype(F32) for p in _split3(c_col)]
    base = 0 if own_first else 3
    pieces = jnp.where(lane == base, hi, jnp.where(lane == base + 1, mid, jnp.where(lane == base + 2, lo, 0.0)))
    ones = jnp.where((lane >= 3 - base) & (lane < 6 - base), 1.0, 0.0)
    return (pieces + ones).astype(BF16)


def _fox_attn_kernel(q_ref, k_ref, v_ref, c_ref, o_ref, qaug_ref, kaug_ref, vaug_ref, m_ref, acc_ref,
                     *, blk, hp):
    g = pl.program_id(1)
    i = pl.program_id(2)
    seq = k_ref.shape[0]
    nt = (((1,), (1,)), ((), ()))

    def head_cols(t):
        return slice(t * HEAD_DIM, (t + 1) * HEAD_DIM)

    @pl.when(i == 0)
    def _():
        lane = lax.broadcasted_iota(jnp.int32, (seq, LANES), 1)
        for t in range(hp):
            c_key = jnp.sum(jnp.where(lane == g * hp + t, c_ref[...], 0.0), axis=1, keepdims=True)
            kaug_ref[t, :, :HEAD_DIM] = k_ref[:, head_cols(t)]
            kaug_ref[t, :, HEAD_DIM:] = _gate_lanes(-c_key, lane, own_first=False)
            vaug_ref[t, :, :HEAD_DIM] = v_ref[:, head_cols(t)]
            vaug_ref[t, :, HEAD_DIM:] = jnp.ones((seq, LANES), BF16)

    lane = lax.broadcasted_iota(jnp.int32, (blk, LANES), 1)
    c_rows = c_ref[pl.ds(pl.multiple_of(i * blk, blk), blk), :]
    for t in range(hp):
        c_query = jnp.sum(jnp.where(lane == g * hp + t, c_rows, 0.0), axis=1, keepdims=True)
        qaug_ref[t, :, :HEAD_DIM] = q_ref[:, head_cols(t)]
        qaug_ref[t, :, HEAD_DIM:] = _gate_lanes(c_query, lane, own_first=True)
    m_ref[...] = jnp.full(m_ref.shape, NEG_BIG, F32)
    acc_ref[...] = jnp.zeros(acc_ref.shape, F32)

    def update(rows, keys, diag_shift):
        nr, nk = rows.stop - rows.start, keys.size
        scores = [lax.dot_general(qaug_ref[t, rows], kaug_ref[t, keys, :], nt,
                                  preferred_element_type=F32) for t in range(hp)]
        for t in range(hp):
            s = scores[t]
            if diag_shift is not None:
                r = lax.broadcasted_iota(jnp.int32, (nr, nk), 0)
                c = lax.broadcasted_iota(jnp.int32, (nr, nk), 1)
                s = jnp.where(r + diag_shift >= c, s, -jnp.inf)
            m_prev = m_ref[t, rows]
            m_new = jnp.maximum(m_prev, jnp.max(s, axis=-1, keepdims=True))
            alpha = jnp.exp2(m_prev - m_new)
            p = jnp.exp2(s - _lane_tile(m_new, nk // LANES)).astype(BF16)
            acc_ref[t, rows] = (_lane_tile(alpha, acc_ref.shape[2] // LANES) * acc_ref[t, rows]
                                + jnp.dot(p, vaug_ref[t, keys, :], preferred_element_type=F32))
            m_ref[t, rows] = m_new

    def body(j, carry):
        update(slice(0, blk), pl.ds(pl.multiple_of(j * blk, blk), blk), None)
        return carry

    lax.fori_loop(0, i, body, 0)
    half = blk // 2
    start = pl.multiple_of(i * blk, blk)
    update(slice(0, half), pl.ds(start, half), 0)
    update(slice(half, blk), pl.ds(start, blk), half)
    for t in range(hp):
        acc = acc_ref[t]
        o_ref[:, head_cols(t)] = (acc[:, :HEAD_DIM] * (1.0 / acc[:, HEAD_DIM:])).astype(o_ref.dtype)


def fox_attention(proj, call, batch, seq, q_col, k_col, v_col, blk, hp):
    nq = seq // blk
    width = hp * HEAD_DIM
    aug = HEAD_DIM + LANES
    return pl.pallas_call(
        functools.partial(_fox_attn_kernel, blk=blk, hp=hp),
        grid=(batch, FOX_HEADS // hp, nq),
        in_specs=[
            pl.BlockSpec((blk, width), lambda b, g, i: (b * nq + i, q_col // hp + g)),
            pl.BlockSpec((seq, width), lambda b, g, i: (b, k_col // hp + g)),
            pl.BlockSpec((seq, width), lambda b, g, i: (b, v_col // hp + g)),
            pl.BlockSpec((seq, LANES), lambda b, g, i: (b, 0)),
        ],
        out_specs=pl.BlockSpec((blk, width), lambda b, g, i: (b * nq + i, g)),
        out_shape=jax.ShapeDtypeStruct((batch * seq, FOX_HEADS * HEAD_DIM), BF16),
        scratch_shapes=[
            pltpu.VMEM((hp, blk, aug), BF16),
            pltpu.VMEM((hp, seq, aug), BF16),
            pltpu.VMEM((hp, seq, aug), BF16),
            pltpu.VMEM((hp, blk, LANES), F32),
            pltpu.VMEM((hp, blk, aug), F32),
        ],
        compiler_params=_params("parallel", "parallel", "arbitrary"),
        name="fox_attention",
    )(proj, proj, proj, call)


def _retention_kernel(q_ref, k_ref, v_ref, g_ref, cos_ref, sin_ref, dm_ref, aux_ref, o_ref,
                      r_ref, qr_ref, qx_ref, kr_ref, kz_ref, of_ref, *, cs, hp):
    rows_per_step = q_ref.shape[0]
    n_chunks = rows_per_step // cs
    half = HEAD_DIM // 2
    nt = (((1,), (1,)), ((), ()))

    @pl.when(pl.program_id(2) == 0)
    def _():
        r_ref[...] = jnp.zeros(r_ref.shape, F32)

    def head_cols(t):
        return slice(t * HEAD_DIM, (t + 1) * HEAD_DIM)

    cos = cos_ref[...]
    sin = sin_ref[...]
    for t in range(hp):
        q = q_ref[:, head_cols(t)].astype(F32)
        k = k_ref[:, head_cols(t)].astype(F32)
        qr = q * cos + pltpu.roll(q, half, 1) * sin
        kr = k * cos + pltpu.roll(k, half, 1) * sin
        xi = jnp.concatenate([aux_ref[t, 0]] * n_chunks, axis=0)
        zeta = jnp.concatenate([aux_ref[t, 1]] * n_chunks, axis=0)
        qr_ref[:, head_cols(t)] = qr.astype(BF16)
        qx_ref[:, head_cols(t)] = (qr * xi).astype(BF16)
        kr_ref[:, head_cols(t)] = kr.astype(BF16)
        kz_ref[:, head_cols(t)] = (kr * zeta).astype(BF16)

    def body(c, carry):
        rows = pl.ds(pl.multiple_of(c * cs, cs), cs)
        for t in range(hp):
            v = v_ref[rows, head_cols(t)]
            s = lax.dot_general(qr_ref[rows, head_cols(t)], kr_ref[rows, head_cols(t)], nt,
                                preferred_element_type=F32) * dm_ref[t]
            r = r_ref[t]
            lhs = jnp.concatenate([s.astype(BF16), qx_ref[rows, head_cols(t)]], axis=1)
            rhs = jnp.concatenate([v, r.astype(BF16)], axis=0)
            of_ref[rows, head_cols(t)] = jnp.dot(lhs, rhs, preferred_element_type=F32)
            r_ref[t] = aux_ref[t, 2, :HEAD_DIM] * r + lax.dot_general(
                kz_ref[rows, head_cols(t)], v, (((0,), (0,)), ((), ())), preferred_element_type=F32)
        return carry

    lax.fori_loop(0, n_chunks, body, 0)

    for t in range(hp):
        o = of_ref[:, head_cols(t)]
        mu = jnp.mean(o, axis=-1, keepdims=True)
        oc = o - mu
        var = jnp.mean(oc * oc, axis=-1, keepdims=True)
        on = oc * lax.rsqrt(var + EPS)
        g = g_ref[:, head_cols(t)].astype(F32)
        o_ref[:, head_cols(t)] = (g * jax.nn.sigmoid(g) * on).astype(o_ref.dtype)


def retention(proj, cos, sin, dm, aux, batch, seq, q_col, k_col, v_col, g_col, rows, hp):
    width = hp * HEAD_DIM
    nr = seq // rows

    def col(c0):
        return pl.BlockSpec((rows, width), lambda b, g, i: (b * nr + i, c0 // hp + g))

    return pl.pallas_call(
        functools.partial(_retention_kernel, cs=RET_CHUNK, hp=hp),
        grid=(batch, RET_HEADS // hp, nr),
        in_specs=[
            col(q_col), col(k_col), col(v_col), col(g_col),
            pl.BlockSpec((rows, HEAD_DIM), lambda b, g, i: (i, 0)),
            pl.BlockSpec((rows, HEAD_DIM), lambda b, g, i: (i, 0)),
            pl.BlockSpec((hp, RET_CHUNK, RET_CHUNK), lambda b, g, i: (g, 0, 0)),
            pl.BlockSpec((hp, 3, RET_CHUNK, LANES), lambda b, g, i: (g, 0, 0, 0)),
        ],
        out_specs=pl.BlockSpec((rows, width), lambda b, g, i: (b * nr + i, g)),
        out_shape=jax.ShapeDtypeStruct((batch * seq, RET_HEADS * HEAD_DIM), BF16),
        scratch_shapes=[
            pltpu.VMEM((hp, HEAD_DIM, HEAD_DIM), F32),
            pltpu.VMEM((rows, width), BF16),
            pltpu.VMEM((rows, width), BF16),
            pltpu.VMEM((rows, width), BF16),
            pltpu.VMEM((rows, width), BF16),
            pltpu.VMEM((rows, width), F32),
        ],
        compiler_params=_params("parallel", "parallel", "arbitrary"),
        name="retention",
    )(proj, proj, proj, proj, cos, sin, dm, aux)


def _retention_tables(seq):
    half = HEAD_DIM // 2
    inv = ROPE_BASE ** (-jnp.arange(half, dtype=F32) / half)
    ang = jnp.arange(seq, dtype=F32)[:, None] * inv[None, :]
    cos, sin = jnp.cos(ang), jnp.sin(ang)
    cos_t = jnp.concatenate([cos, cos], axis=-1)
    sin_t = jnp.concatenate([-sin, sin], axis=-1)
    cs = RET_CHUNK
    log_g = jnp.log1p(-jnp.exp2(-5.0 - jnp.arange(RET_HEADS, dtype=F32)))
    pos = jnp.arange(cs, dtype=F32)
    diff = pos[:, None] - pos[None, :]
    dm = jnp.where(diff >= 0, jnp.exp(log_g[:, None, None] * jnp.maximum(diff, 0.0)), 0.0)
    zeta = jnp.exp(log_g[:, None] * (cs - 1 - pos)[None, :])
    xi = jnp.exp(log_g[:, None] * (pos + 1)[None, :])
    g_chunk = jnp.broadcast_to(jnp.exp(log_g * cs)[:, None], (RET_HEADS, cs))
    aux = jnp.broadcast_to(jnp.stack([xi, zeta, g_chunk], axis=1)[..., None], (RET_HEADS, 3, cs, LANES))
    return cos_t, sin_t, dm, aux


def _s5_kernel(u_ref, are_ref, aim_ref, ls_ref, bre_ref, bim_ref, cre_ref, cim_ref, d_ref, y_ref,
               toep_ref, win_ref, woutt_ref, aq_ref, wf_ref, ucat_ref, inc_ref, x_ref, *, q):
    m = u_ref.shape[0] // q
    sl = STATE_LANES

    @pl.when(pl.program_id(1) == 0)
    def _():
        _s5_build_tables(are_ref, aim_ref, ls_ref, bre_ref, bim_ref, cre_ref, cim_ref,
                         toep_ref, win_ref, woutt_ref, aq_ref, wf_ref, q=q)

    for s in range(q):
        ucat_ref[:, s * LANES:(s + 1) * LANES] = u_ref[pl.ds(s, m, stride=q), :].astype(BF16)
    ucat = ucat_ref[...]
    inc_ref[...] = jnp.dot(ucat, win_ref[...], preferred_element_type=F32)
    a_re = aq_ref[0:1, :]
    a_im = aq_ref[1:2, :]

    def body(n, carry):
        x_re, x_im = carry
        x_ref[pl.ds(n, 1), 0:sl] = x_re
        x_ref[pl.ds(n, 1), sl:2 * sl] = x_im
        i_re = inc_ref[pl.ds(n, 1), 0:sl]
        i_im = inc_ref[pl.ds(n, 1), sl:2 * sl]
        return (a_re * x_re - a_im * x_im + i_re, a_re * x_im + a_im * x_re + i_im)

    zero = jnp.zeros((1, sl), F32)
    lax.fori_loop(0, m, body, (zero, zero))
    xb = x_ref[...].astype(BF16)
    nt = (((1,), (1,)), ((), ()))
    for t0 in range(0, q, S5_OUT_STEPS):
        hi = (t0 + S5_OUT_STEPS) * LANES
        cols = slice(t0 * LANES, hi)
        y = jnp.dot(ucat_ref[:, :hi], toep_ref[:hi, cols], preferred_element_type=F32)
        y = y + lax.dot_general(xb, woutt_ref[cols, :], nt, preferred_element_type=F32)
        for t in range(t0, t0 + S5_OUT_STEPS):
            rows = pl.ds(t, m, stride=q)
            y_ref[rows, :] = (y[:, (t - t0) * LANES:(t - t0 + 1) * LANES]
                              + d_ref[...] * u_ref[rows, :])


def s5_core(proj, a_re, a_im, b_re, b_im, c_re, c_im, log_step, d_skip, batch, seq, u_col):
    q = S5_CHUNK
    m = seq // q
    g, p = a_re.shape
    gb = GROUPS_PER_BLOCK
    nblk = g // gb
    sl = gb * p

    def rows(t):
        return t.astype(F32).reshape(nblk, 1, sl)

    def b_mat(t):
        return t.astype(F32).reshape(nblk, gb, p, S5_GROUP).transpose(0, 3, 1, 2).reshape(nblk, S5_GROUP, sl)

    def c_mat(t):
        return t.astype(F32).reshape(nblk, gb, S5_GROUP, p).transpose(0, 2, 1, 3).reshape(nblk, S5_GROUP, sl)

    row_spec = pl.BlockSpec((1, 1, sl), lambda j, b: (j, 0, 0))
    mat_spec = pl.BlockSpec((1, S5_GROUP, sl), lambda j, b: (j, 0, 0))
    return pl.pallas_call(
        functools.partial(_s5_kernel, q=q),
        grid=(nblk, batch),
        in_specs=[
            pl.BlockSpec((seq, LANES), lambda j, b: (b, u_col + j)),
            row_spec, row_spec, row_spec, mat_spec, mat_spec, mat_spec, mat_spec,
            pl.BlockSpec((1, LANES), lambda j, b: (0, j)),
        ],
        out_specs=pl.BlockSpec((seq, LANES), lambda j, b: (b, j)),
        out_shape=jax.ShapeDtypeStruct((batch * seq, nblk * LANES), F32),
        scratch_shapes=[
            pltpu.VMEM((q * LANES, q * LANES), BF16),
            pltpu.VMEM((q * LANES, 2 * sl), BF16),
            pltpu.VMEM((q * LANES, 2 * sl), BF16),
            pltpu.VMEM((2, sl), F32),
            pltpu.VMEM((q * LANES, 2 * sl), F32),
            pltpu.VMEM((m, q * LANES), BF16),
            pltpu.VMEM((m, 2 * sl), F32),
            pltpu.VMEM((m, 2 * sl), F32),
        ],
        compiler_params=_params("parallel", "arbitrary"),
        name="s5_core",
    )(proj, rows(a_re), rows(a_im), rows(log_step), b_mat(b_re), b_mat(b_im), c_mat(c_re), c_mat(c_im),
      d_skip)


def _dot_nt_split(x, y):
    nt = (((1,), (1,)), ((), ()))
    xh = x.astype(BF16)
    xl = (x - xh.astype(F32)).astype(BF16)
    yh = y.astype(BF16)
    yl = (y - yh.astype(F32)).astype(BF16)
    out = lax.dot_general(xh, yh, nt, preferred_element_type=F32)
    out = out + lax.dot_general(xh, yl, nt, preferred_element_type=F32)
    return out + lax.dot_general(xl, yh, nt, preferred_element_type=F32)


def _s5_build_tables(are_ref, aim_ref, ls_ref, bre_ref, bim_ref, cre_ref, cim_ref,
                     toep_ref, win_ref, woutt_ref, aq_ref, wf_ref, *, q):
    sl = STATE_LANES
    lam_re = jnp.minimum(are_ref[0], -1e-4)
    lam_im = aim_ref[0]
    step = jnp.exp(ls_ref[0])
    mag = jnp.exp(lam_re * step)
    a_re = mag * jnp.cos(lam_im * step)
    a_im = mag * jnp.sin(lam_im * step)
    den = lam_re * lam_re + lam_im * lam_im
    f_re = ((a_re - 1.0) * lam_re + a_im * lam_im) / den
    f_im = (a_im * lam_re - (a_re - 1.0) * lam_im) / den
    b_re = bre_ref[0]
    b_im = bim_ref[0]
    bb_re = f_re * b_re - f_im * b_im
    bb_im = f_re * b_im + f_im * b_re
    c_re = cre_ref[0]
    c_im = cim_ref[0]

    pows = [(jnp.ones((1, sl), F32), jnp.zeros((1, sl), F32))]
    for _ in range(q):
        p_re, p_im = pows[-1]
        pows.append((p_re * a_re - p_im * a_im, p_re * a_im + p_im * a_re))

    row_group = lax.shift_right_logical(lax.broadcasted_iota(jnp.int32, (LANES, sl), 0), S5_GROUP_SHIFT)
    col_group = lax.shift_right_logical(lax.broadcasted_iota(jnp.int32, (LANES, sl), 1), S5_STATE_SHIFT)
    same_group = row_group == col_group

    def tile(k, m_re, m_im, im_sign):
        p_re, p_im = pows[k]
        v_re = p_re * m_re - p_im * m_im
        v_im = (p_re * m_im + p_im * m_re) * im_sign
        e_re = jnp.where(same_group, jnp.concatenate([v_re] * GROUPS_PER_BLOCK, axis=0), 0.0)
        e_im = jnp.where(same_group, jnp.concatenate([v_im] * GROUPS_PER_BLOCK, axis=0), 0.0)
        return jnp.concatenate([e_re, e_im], axis=1)

    for s in range(q):
        rows = slice(s * LANES, (s + 1) * LANES)
        wf_ref[rows, :] = tile(q - 1 - s, bb_re, bb_im, 1.0)
        woutt_ref[rows, :] = tile(s + 1, c_re, c_im, -1.0).astype(BF16)
    win_ref[...] = wf_ref[...].astype(BF16)
    taps = _dot_nt_split(wf_ref[...], tile(0, c_re, c_im, -1.0))
    toep_ref[...] = jnp.zeros(toep_ref.shape, BF16)
    for s in range(q):
        for t in range(s, q):
            lag_rows = slice((q - 1 - (t - s)) * LANES, (q - (t - s)) * LANES)
            toep_ref[s * LANES:(s + 1) * LANES, t * LANES:(t + 1) * LANES] = taps[lag_rows, :].astype(BF16)
    aq_ref[0:1, :] = pows[q][0]
    aq_ref[1:2, :] = pows[q][1]


def _even_weights(w_in):
    fw = FOX_HEADS * HEAD_DIM
    rw = RET_HEADS * HEAD_DIM
    scale = HEAD_DIM ** -0.5
    c0 = 3 * fw
    c1 = c0 + FOX_HEADS
    fox_scale = jnp.concatenate([jnp.full((fw,), scale * LOG2E, F32), jnp.ones((2 * fw,), F32)])
    ret_scale = jnp.concatenate([jnp.ones((rw,), F32), jnp.full((rw,), scale, F32), jnp.ones((2 * rw,), F32)])
    w_t = jnp.swapaxes(w_in, 1, 2)
    w_main = jnp.concatenate([w_t[:, :c0] * fox_scale[:, None], w_t[:, c1:] * ret_scale[:, None]],
                             axis=1).astype(BF16)
    w_forget = jnp.pad(w_t[:, c0:c1], ((0, 0), (0, LANES - FOX_HEADS), (0, 0))).astype(BF16)
    return w_main, w_forget


def _even_mixer(h, g, weights, e, b_forget, tables, batch, seq):
    w_main, w_forget = weights
    bias = jnp.pad(b_forget.astype(F32), (0, LANES - FOX_HEADS)).reshape(1, LANES)
    t = h.shape[0]
    proj, fl = norm_matmul(h, g, w_main, e, 1, BF16, tm=_row_tile(t, PROJ_ROWS), tn=PROJ_COLS,
                           w_side=w_forget, transposed=True)
    call = fox_cumsum(fl, bias, batch, seq)
    nh = FOX_HEADS
    fox = fox_attention(proj, call, batch, seq, 0, nh, 2 * nh, blk=_row_tile(seq, FOX_BLOCK),
                        hp=FOX_HEADS_PER_STEP)
    cos_t, sin_t, dm, aux = tables
    ret = retention(proj, cos_t, sin_t, dm, aux, batch, seq, 3 * nh, 4 * nh, 5 * nh, 6 * nh,
                    rows=_row_tile(seq, RET_ROWS), hp=RET_HEADS_PER_STEP)
    return fox, ret


def _odd_mixer(h, g, w_in, o, a_re, a_im, b_re, b_im, c_re, c_im, d_skip, log_step, batch, seq):
    sw = a_re.shape[0] * S5_GROUP
    t = h.shape[0]
    proj = norm_matmul(h, g, w_in, o, 1, F32, tm=_row_tile(t, PROJ_ROWS), tn=PROJ_COLS)
    y = s5_core(proj, a_re, a_im, b_re, b_im, c_re, c_im, log_step, d_skip.reshape(1, sw).astype(F32),
                batch, seq, 0)
    return y, proj


def kernel(x, mem, norm_mix, norm_xattn, norm_mlp, norm_mem, norm_final, ab_w_in, ab_b_forget, ab_w_out,
           cd_w_in, s5_a_re, s5_a_im, s5_b_re, s5_b_im, s5_c_re, s5_c_im, s5_d, s5_log_step, s5_w_glu,
           conv_w, cd_w_out, xa_wq, xa_wkv, xa_wo, mlp_w1, mlp_w2):
    batch, seq, d = x.shape
    depth = norm_mix.shape[0]
    mlen = mem.shape[1]
    xw = xa_wq.shape[2]
    h = x.reshape(batch * seq, d)
    memf = mem.reshape(batch * mlen, d)
    tables = _retention_tables(seq)
    even_w = _even_weights(ab_w_in)
    odd_w_in = cd_w_in
    even_w_out = ab_w_out.astype(BF16)
    odd_w_out = cd_w_out.astype(BF16)
    w_glu = s5_w_glu.astype(BF16)
    conv_wf = conv_w.astype(F32)
    wq = (xa_wq * HEAD_DIM ** -0.5).astype(BF16)
    wo = xa_wo.astype(BF16)
    w1 = mlp_w1
    w2 = mlp_w2
    kv_all = norm_matmul(memf, norm_mem, xa_wkv, 0, depth, BF16,
                         tm=_row_tile(batch * mlen, PROJ_ROWS), tn=PROJ_COLS)
    kv_all = kv_all.reshape(batch, mlen, depth * 2 * xw)
    for layer in range(depth):
        odd = layer % 2 == 1
        idx = layer // 2
        if odd:
            y, proj = _odd_mixer(h, norm_mix[layer], odd_w_in, idx, s5_a_re[idx], s5_a_im[idx],
                                 s5_b_re[idx], s5_b_im[idx], s5_c_re[idx], s5_c_im[idx], s5_d[idx],
                                 s5_log_step[idx], batch, seq)
            mixer_out = (y, proj, conv_wf, w_glu)
            w_out = odd_w_out
        else:
            mixer_out = _even_mixer(h, norm_mix[layer], even_w, idx, ab_b_forget[idx], tables, batch, seq)
            w_out = even_w_out
        h = mix_xattn_block(h, mixer_out, w_out, idx, norm_xattn[layer], wq, kv_all, wo, layer,
                            seq, tm=_row_tile(seq, MIX_ROWS), odd=odd)
        h = mlp_block(h, norm_mlp[layer], w1, w2, layer, norm_final, layer == depth - 1,
                      tm=_row_tile(batch * seq, MLP_ROWS), tf=MLP_COLS)
    return h.reshape(batch, seq, d)
```

```python
import functools
import math

import jax
import jax.numpy as jnp
from jax import lax
from jax.experimental import pallas as pl
from jax.experimental.pallas import tpu as pltpu

F32 = jnp.float32
BF16 = jnp.bfloat16

EPS = 1e-6
ROPE_BASE = 10000.0
LANES = 128
SUBLANES = 8
HEAD_DIM = 128
FOX_HEADS = 8
RET_HEADS = 8
RET_CHUNK = 256
XA_HEADS = 4
S5_GROUP = 16
S5_STATE = 64
S5_CHUNK = 16
MXU_TILE = 256
S5_OUT_STEPS = MXU_TILE // LANES
CONV_K = 3
S5_GROUP_SHIFT = S5_GROUP.bit_length() - 1
S5_STATE_SHIFT = S5_STATE.bit_length() - 1
GROUPS_PER_BLOCK = LANES // S5_GROUP
STATE_LANES = GROUPS_PER_BLOCK * S5_STATE
VMEM_LIMIT = 56 * 1024 * 1024
NEG_BIG = -1e30
LOG2E = math.log2(math.e)

PROJ_ROWS, PROJ_COLS = 1024, 1024
MIX_ROWS = 512
MLP_ROWS, MLP_COLS = 1024, 512
FOX_BLOCK = 1024
FOX_HEADS_PER_STEP = 2
RET_ROWS = 1024
RET_HEADS_PER_STEP = 8


def _row_tile(rows, want):
    tile = min(rows, want)
    while rows % tile:
        tile //= 2
    return tile


def _params(*sem):
    return pltpu.CompilerParams(dimension_semantics=sem, vmem_limit_bytes=VMEM_LIMIT)


def _rms(x, g):
    ms = jnp.mean(x * x, axis=-1, keepdims=True)
    return x * lax.rsqrt(ms + EPS) * g


def _norm_matmul_kernel(*refs, side, transposed, slab_first):
    if side:
        x_ref, g_ref, w_ref, ws_ref, o_ref, os_ref, xn_ref = refs
    elif slab_first:
        x_ref, g_ref, w_ref, o_ref, slab_ref, xn_ref = refs
    else:
        x_ref, g_ref, w_ref, o_ref, xn_ref = refs
    dims = (((1,), (1 if transposed else 0,)), ((), ()))
    j = pl.program_id(1)

    @pl.when(j == 0)
    def _():
        xn_ref[...] = _rms(x_ref[...], g_ref[...]).astype(BF16)
        if side:
            os_ref[...] = lax.dot_general(xn_ref[...], ws_ref[...], dims, preferred_element_type=F32)

    res = lax.dot_general(xn_ref[...], w_ref[...].astype(BF16), dims, preferred_element_type=F32)
    if slab_first:
        @pl.when(j == 0)
        def _():
            for s in range(slab_ref.shape[0]):
                slab_ref[s] = res[:, s * LANES:(s + 1) * LANES].astype(slab_ref.dtype)

        @pl.when(j > 0)
        def _():
            o_ref[...] = res.astype(o_ref.dtype)
    else:
        o_ref[...] = res.astype(o_ref.dtype)


def norm_matmul(x, g, w, layer, n_layers, out_dtype, tm, tn, w_side=None, transposed=False,
                slab_first=False):
    t, d = x.shape
    n = w.shape[1 if transposed else 2]
    nj = n // tn
    assert not (slab_first and (w_side is not None or n_layers != 1))

    def w_spec(cols, index):
        if transposed:
            return pl.BlockSpec((None, cols, d), lambda i, j: (index(j)[0], index(j)[1], 0))
        return pl.BlockSpec((None, d, cols), lambda i, j: (index(j)[0], 0, index(j)[1]))

    side = w_side is not None
    in_specs = [
        pl.BlockSpec((tm, d), lambda i, j: (i, 0)),
        pl.BlockSpec((1, d), lambda i, j: (0, 0)),
        w_spec(tn, lambda j: (layer + j // nj, j % nj)),
    ]
    out_specs = pl.BlockSpec((tm, tn), lambda i, j: (i, j))
    out_shape = jax.ShapeDtypeStruct((t, n_layers * n), out_dtype)
    args = (x, g.reshape(1, d), w)
    if side:
        ns = w_side.shape[1 if transposed else 2]
        in_specs.append(w_spec(ns, lambda j: (layer, 0)))
        out_specs = [out_specs, pl.BlockSpec((tm, ns), lambda i, j: (i, 0))]
        out_shape = [out_shape, jax.ShapeDtypeStruct((t, ns), F32)]
        args = args + (w_side,)
    if slab_first:
        n_slabs = tn // LANES
        out_specs = [pl.BlockSpec((tm, tn), lambda i, j: (i, jnp.maximum(j - 1, 0))),
                     pl.BlockSpec((n_slabs, tm, LANES), lambda i, j: (0, i, 0))]
        out_shape = [jax.ShapeDtypeStruct((t, n - tn), out_dtype),
                     jax.ShapeDtypeStruct((n_slabs, t, LANES), out_dtype)]
    return pl.pallas_call(
        functools.partial(_norm_matmul_kernel, side=side, transposed=transposed, slab_first=slab_first),
        grid=(t // tm, n_layers * nj),
        in_specs=in_specs,
        out_specs=out_specs,
        out_shape=out_shape,
        scratch_shapes=[pltpu.VMEM((tm, d), BF16)],
        compiler_params=_params("parallel", "arbitrary"),
        name="norm_matmul",
    )(*args)


def _mlp_kernel(h_ref, g_ref, w1_ref, w2_ref, gf_ref, o_ref, xn_ref, *, final_norm):
    f = pl.program_id(1)

    @pl.when(f == 0)
    def _():
        x = h_ref[...]
        xn_ref[...] = _rms(x, g_ref[...]).astype(BF16)
        o_ref[...] = x

    a = jnp.dot(xn_ref[...], w1_ref[...].astype(BF16), preferred_element_type=F32)
    a = jnp.square(jnp.maximum(a, 0.0)).astype(BF16)
    o_ref[...] += jnp.dot(a, w2_ref[...].astype(BF16), preferred_element_type=F32)

    if final_norm:
        @pl.when(f == pl.num_programs(1) - 1)
        def _():
            o_ref[...] = _rms(o_ref[...], gf_ref[...])


def mlp_block(h, g, w1, w2, layer, g_final, final_norm, tm, tf):
    t, d = h.shape
    dff = w1.shape[2]
    return pl.pallas_call(
        functools.partial(_mlp_kernel, final_norm=final_norm),
        grid=(t // tm, dff // tf),
        in_specs=[
            pl.BlockSpec((tm, d), lambda i, f: (i, 0)),
            pl.BlockSpec((1, d), lambda i, f: (0, 0)),
            pl.BlockSpec((None, d, tf), lambda i, f: (layer, 0, f)),
            pl.BlockSpec((None, tf, d), lambda i, f: (layer, f, 0)),
            pl.BlockSpec((1, d), lambda i, f: (0, 0)),
        ],
        out_specs=pl.BlockSpec((tm, d), lambda i, f: (i, 0)),
        out_shape=jax.ShapeDtypeStruct((t, d), F32),
        scratch_shapes=[pltpu.VMEM((tm, d), BF16)],
        compiler_params=_params("parallel", "arbitrary"),
        name="mlp_block",
    )(h, g.reshape(1, d), w1, w2, g_final.reshape(1, d))


def _s5_gate(y, w_glu):
    g = 0.5 * y * (1.0 + jnp.tanh(math.sqrt(2.0 / math.pi) * (y + 0.044715 * (y * y * y))))
    z = jnp.dot(g.astype(BF16), w_glu, preferred_element_type=F32)
    return (g * jax.nn.sigmoid(z)).astype(BF16)


def _short_conv(hc, gb, gc, z_prev, w_ref):
    z = gc * hc
    row = lax.broadcasted_iota(jnp.int32, z.shape, 0)
    y = w_ref[CONV_K - 1:CONV_K, :] * z
    for lag in range(1, CONV_K):
        zl = pltpu.roll(z, lag, 0)
        for r in range(lag):
            src = SUBLANES - lag + r
            zl = jnp.where(row == r, z_prev[src:src + 1, :], zl)
        y = y + w_ref[CONV_K - 1 - lag:CONV_K - lag, :] * zl
    return (gb * y).astype(BF16)


def _mix_xattn_kernel(*refs, odd, per_batch):
    if odd:
        (h_ref, y_ref, hc_ref, gb_ref, gc_ref, hcp_ref, gcp_ref, cw_ref, wglu_ref,
         w1_ref, w2_ref, g_ref, wq_ref, k_ref, v_ref, wo_ref, o_ref) = refs
        y = jnp.concatenate([y_ref[s] for s in range(y_ref.shape[0])], axis=1)
        a1 = _s5_gate(y, wglu_ref[...])
        first = pl.program_id(0) % per_batch == 0
        z_prev = jnp.where(first, 0.0, gcp_ref[...] * hcp_ref[...])
        a2 = _short_conv(hc_ref[...], gb_ref[...], gc_ref[...], z_prev, cw_ref)
    else:
        h_ref, a1_ref, a2_ref, w1_ref, w2_ref, g_ref, wq_ref, k_ref, v_ref, wo_ref, o_ref = refs
        a1 = a1_ref[...]
        a2 = a2_ref[...]
    x = h_ref[...] + jnp.dot(a1, w1_ref[...], preferred_element_type=F32)
    x = x + jnp.dot(a2, w2_ref[...], preferred_element_type=F32)
    xn = _rms(x, g_ref[...]).astype(BF16)
    q = jnp.dot(xn, wq_ref[...], preferred_element_type=F32).astype(BF16)
    heads = []
    for hd in range(XA_HEADS):
        sl = slice(hd * HEAD_DIM, (hd + 1) * HEAD_DIM)
        s = lax.dot_general(q[:, sl], k_ref[0, :, sl], (((1,), (1,)), ((), ())),
                            preferred_element_type=F32)
        m = jnp.max(s, axis=-1, keepdims=True)
        p = jnp.exp(s - m)
        l = jnp.sum(p, axis=-1, keepdims=True)
        oh = jnp.dot(p.astype(BF16), v_ref[0, :, sl], preferred_element_type=F32)
        heads.append((oh * (1.0 / l)).astype(BF16))
    o = jnp.concatenate(heads, axis=-1)
    o_ref[...] = x + jnp.dot(o, wo_ref[...], preferred_element_type=F32)


def mix_xattn_block(h, mixer_inputs, w_out, mixer_idx, g, wq, kv, wo, layer, seq, tm, odd):
    t, d = h.shape
    k1 = w_out.shape[1] // 2
    k2 = k1
    mlen = kv.shape[1]
    xw = wq.shape[2]
    per_batch = seq // tm

    def fixed(shape, *index):
        return pl.BlockSpec(shape, lambda i: index, pipeline_mode=pl.Buffered(1))

    def rows(width, col=0):
        return pl.BlockSpec((tm, width), lambda i: (i, col))

    if odd:
        y, proj, conv_w, w_glu = mixer_inputs
        halo = tm // SUBLANES

        def prev_rows(col):
            return pl.BlockSpec((SUBLANES, k2), lambda i: (jnp.maximum(i * halo - 1, 0), col))

        mixer_args = (y, proj, proj, proj, proj, proj, conv_w, w_glu)
        mixer_specs = [pl.BlockSpec((y.shape[0], tm, LANES), lambda i: (0, i, 0)),
                       rows(k2, 0), rows(k2, 1), rows(k2, 2), prev_rows(0), prev_rows(2),
                       fixed((None,) + conv_w.shape[1:], mixer_idx, 0, 0),
                       fixed((None,) + w_glu.shape[1:], mixer_idx, 0, 0)]
    else:
        mixer_args = mixer_inputs
        mixer_specs = [rows(k1), rows(k2)]

    return pl.pallas_call(
        functools.partial(_mix_xattn_kernel, odd=odd, per_batch=per_batch),
        grid=(t // tm,),
        in_specs=[rows(d)] + mixer_specs + [
            fixed((None, k1, d), mixer_idx, 0, 0), fixed((None, k2, d), mixer_idx, 1, 0),
            fixed((1, d), 0, 0), fixed((None, d, xw), layer, 0, 0),
            pl.BlockSpec((1, mlen, xw), lambda i: (i // per_batch, 0, 2 * layer)),
            pl.BlockSpec((1, mlen, xw), lambda i: (i // per_batch, 0, 2 * layer + 1)),
            fixed((None, xw, d), layer, 0, 0),
        ],
        out_specs=rows(d),
        out_shape=jax.ShapeDtypeStruct((t, d), F32),
        compiler_params=_params("parallel"),
        name="mix_xattn_block",
    )(h, *mixer_args, w_out, w_out, g.reshape(1, d), wq, kv, kv, wo)


def _split3(x):
    hi = x.astype(BF16)
    r1 = x - hi.astype(F32)
    mid = r1.astype(BF16)
    lo = (r1 - mid.astype(F32)).astype(BF16)
    return hi, mid, lo


def _fox_cumsum_kernel(fl_ref, b_ref, call_ref, *, blk):
    seq = fl_ref.shape[0]
    row = lax.broadcasted_iota(jnp.int32, (blk, blk), 0)
    col = lax.broadcasted_iota(jnp.int32, (blk, blk), 1)
    tri = jnp.where(row >= col, 1.0, 0.0).astype(BF16)
    carry = jnp.zeros((1, LANES), F32)
    for i in range(seq // blk):
        rows = slice(i * blk, (i + 1) * blk)
        x = fl_ref[rows, :] + b_ref[...]
        ls = (jnp.minimum(x, 0.0) - jnp.log(1.0 + jnp.exp(-jnp.abs(x)))) * LOG2E
        c = carry
        for piece in _split3(ls):
            c = c + jnp.dot(tri, piece, preferred_element_type=F32)
        call_ref[rows, :] = c
        carry = c[blk - 1:blk, :]


def fox_cumsum(fl, bias, batch, seq):
    blk = min(MXU_TILE, seq)
    return pl.pallas_call(
        functools.partial(_fox_cumsum_kernel, blk=blk),
        grid=(batch,),
        in_specs=[
            pl.BlockSpec((seq, LANES), lambda b: (b, 0)),
            pl.BlockSpec((1, LANES), lambda b: (0, 0)),
        ],
        out_specs=pl.BlockSpec((seq, LANES), lambda b: (b, 0)),
        out_shape=jax.ShapeDtypeStruct((batch * seq, LANES), F32),
        compiler_params=_params("parallel"),
        name="fox_cumsum",
    )(fl, bias)


def _lane_tile(x, reps):
    return jnp.concatenate([x] * reps, axis=1)


def _gate_lanes(c_col, lane, own_first):
    hi, mid, lo = [p.astype(F32) for p in _split3(c_col)]
    base = 0 if own_first else 3
    pieces = jnp.where(lane == base, hi, jnp.where(lane == base + 1, mid, jnp.where(lane == base + 2, lo, 0.0)))
    ones = jnp.where((lane >= 3 - base) & (lane < 6 - base), 1.0, 0.0)
    return (pieces + ones).astype(BF16)


def _fox_attn_kernel(q_ref, k_ref, v_ref, c_ref, o_ref, qaug_ref, kaug_ref, vaug_ref, m_ref, acc_ref,
                     *, blk, hp):
    g = pl.program_id(1)
    i = pl.program_id(2)
    seq = k_ref.shape[0]
    nt = (((1,), (1,)), ((), ()))

    def head_cols(t):
        return slice(t * HEAD_DIM, (t + 1) * HEAD_DIM)

    @pl.when(i == 0)
    def _():
        lane = lax.broadcasted_iota(jnp.int32, (seq, LANES), 1)
        for t in range(hp):
            c_key = jnp.sum(jnp.where(lane == g * hp + t, c_ref[...], 0.0), axis=1, keepdims=True)
            kaug_ref[t, :, :HEAD_DIM] = k_ref[:, head_cols(t)]
            kaug_ref[t, :, HEAD_DIM:] = _gate_lanes(-c_key, lane, own_first=False)
            vaug_ref[t, :, :HEAD_DIM] = v_ref[:, head_cols(t)]
            vaug_ref[t, :, HEAD_DIM:] = jnp.ones((seq, LANES), BF16)

    lane = lax.broadcasted_iota(jnp.int32, (blk, LANES), 1)
    c_rows = c_ref[pl.ds(pl.multiple_of(i * blk, blk), blk), :]
    for t in range(hp):
        c_query = jnp.sum(jnp.where(lane == g * hp + t, c_rows, 0.0), axis=1, keepdims=True)
        qaug_ref[t, :, :HEAD_DIM] = q_ref[:, head_cols(t)]
        qaug_ref[t, :, HEAD_DIM:] = _gate_lanes(c_query, lane, own_first=True)
    m_ref[...] = jnp.full(m_ref.shape, NEG_BIG, F32)
    acc_ref[...] = jnp.zeros(acc_ref.shape, F32)

    def update(rows, keys, diag_shift):
        nr, nk = rows.stop - rows.start, keys.size
        scores = [lax.dot_general(qaug_ref[t, rows], kaug_ref[t, keys, :], nt,
                                  preferred_element_type=F32) for t in range(hp)]
        for t in range(hp):
            s = scores[t]
            if diag_shift is not None:
                r = lax.broadcasted_iota(jnp.int32, (nr, nk), 0)
                c = lax.broadcasted_iota(jnp.int32, (nr, nk), 1)
                s = jnp.where(r + diag_shift >= c, s, -jnp.inf)
            m_prev = m_ref[t, rows]
            m_new = jnp.maximum(m_prev, jnp.max(s, axis=-1, keepdims=True))
            alpha = jnp.exp2(m_prev - m_new)
            p = jnp.exp2(s - _lane_tile(m_new, nk // LANES)).astype(BF16)
            acc_ref[t, rows] = (_lane_tile(alpha, acc_ref.shape[2] // LANES) * acc_ref[t, rows]
                                + jnp.dot(p, vaug_ref[t, keys, :], preferred_element_type=F32))
            m_ref[t, rows] = m_new

    def body(j, carry):
        update(slice(0, blk), pl.ds(pl.multiple_of(j * blk, blk), blk), None)
        return carry

    lax.fori_loop(0, i, body, 0)
    half = blk // 2
    start = pl.multiple_of(i * blk, blk)
    update(slice(0, half), pl.ds(start, half), 0)
    update(slice(half, blk), pl.ds(start, blk), half)
    for t in range(hp):
        acc = acc_ref[t]
        o_ref[:, head_cols(t)] = (acc[:, :HEAD_DIM] * (1.0 / acc[:, HEAD_DIM:])).astype(o_ref.dtype)


def fox_attention(proj, call, batch, seq, q_col, k_col, v_col, blk, hp):
    nq = seq // blk
    width = hp * HEAD_DIM
    aug = HEAD_DIM + LANES
    return pl.pallas_call(
        functools.partial(_fox_attn_kernel, blk=blk, hp=hp),
        grid=(batch, FOX_HEADS // hp, nq),
        in_specs=[
            pl.BlockSpec((blk, width), lambda b, g, i: (b * nq + i, q_col // hp + g)),
            pl.BlockSpec((seq, width), lambda b, g, i: (b, k_col // hp + g)),
            pl.BlockSpec((seq, width), lambda b, g, i: (b, v_col // hp + g)),
            pl.BlockSpec((seq, LANES), lambda b, g, i: (b, 0)),
        ],
        out_specs=pl.BlockSpec((blk, width), lambda b, g, i: (b * nq + i, g)),
        out_shape=jax.ShapeDtypeStruct((batch * seq, FOX_HEADS * HEAD_DIM), BF16),
        scratch_shapes=[
            pltpu.VMEM((hp, blk, aug), BF16),
            pltpu.VMEM((hp, seq, aug), BF16),
            pltpu.VMEM((hp, seq, aug), BF16),
            pltpu.VMEM((hp, blk, LANES), F32),
            pltpu.VMEM((hp, blk, aug), F32),
        ],
        compiler_params=_params("parallel", "parallel", "arbitrary"),
        name="fox_attention",
    )(proj, proj, proj, call)


def _retention_kernel(q_ref, k_ref, v_ref, g_ref, cos_ref, sin_ref, dm_ref, aux_ref, o_ref,
                      r_ref, qr_ref, qx_ref, kr_ref, kz_ref, of_ref, *, cs, hp):
    rows_per_step = q_ref.shape[0]
    n_chunks = rows_per_step // cs
    half = HEAD_DIM // 2
    nt = (((1,), (1,)), ((), ()))

    @pl.when(pl.program_id(2) == 0)
    def _():
        r_ref[...] = jnp.zeros(r_ref.shape, F32)

    def head_cols(t):
        return slice(t * HEAD_DIM, (t + 1) * HEAD_DIM)

    cos = cos_ref[...]
    sin = sin_ref[...]
    for t in range(hp):
        q = q_ref[:, head_cols(t)].astype(F32)
        k = k_ref[:, head_cols(t)].astype(F32)
        qr = q * cos + pltpu.roll(q, half, 1) * sin
        kr = k * cos + pltpu.roll(k, half, 1) * sin
        xi = jnp.concatenate([aux_ref[t, 0]] * n_chunks, axis=0)
        zeta = jnp.concatenate([aux_ref[t, 1]] * n_chunks, axis=0)
        qr_ref[:, head_cols(t)] = qr.astype(BF16)
        qx_ref[:, head_cols(t)] = (qr * xi).astype(BF16)
        kr_ref[:, head_cols(t)] = kr.astype(BF16)
        kz_ref[:, head_cols(t)] = (kr * zeta).astype(BF16)

    def body(c, carry):
        rows = pl.ds(pl.multiple_of(c * cs, cs), cs)
        for t in range(hp):
            v = v_ref[rows, head_cols(t)]
            s = lax.dot_general(qr_ref[rows, head_cols(t)], kr_ref[rows, head_cols(t)], nt,
                                preferred_element_type=F32) * dm_ref[t]
            r = r_ref[t]
            lhs = jnp.concatenate([s.astype(BF16), qx_ref[rows, head_cols(t)]], axis=1)
            rhs = jnp.concatenate([v, r.astype(BF16)], axis=0)
            of_ref[rows, head_cols(t)] = jnp.dot(lhs, rhs, preferred_element_type=F32)
            r_ref[t] = aux_ref[t, 2, :HEAD_DIM] * r + lax.dot_general(
                kz_ref[rows, head_cols(t)], v, (((0,), (0,)), ((), ())), preferred_element_type=F32)
        return carry

    lax.fori_loop(0, n_chunks, body, 0)

    for t in range(hp):
        o = of_ref[:, head_cols(t)]
        mu = jnp.mean(o, axis=-1, keepdims=True)
        oc = o - mu
        var = jnp.mean(oc * oc, axis=-1, keepdims=True)
        on = oc * lax.rsqrt(var + EPS)
        g = g_ref[:, head_cols(t)].astype(F32)
        o_ref[:, head_cols(t)] = (g * jax.nn.sigmoid(g) * on).astype(o_ref.dtype)


def retention(proj, cos, sin, dm, aux, batch, seq, q_col, k_col, v_col, g_col, rows, hp):
    width = hp * HEAD_DIM
    nr = seq // rows

    def col(c0):
        return pl.BlockSpec((rows, width), lambda b, g, i: (b * nr + i, c0 // hp + g))

    return pl.pallas_call(
        functools.partial(_retention_kernel, cs=RET_CHUNK, hp=hp),
        grid=(batch, RET_HEADS // hp, nr),
        in_specs=[
            col(q_col), col(k_col), col(v_col), col(g_col),
            pl.BlockSpec((rows, HEAD_DIM), lambda b, g, i: (i, 0)),
            pl.BlockSpec((rows, HEAD_DIM), lambda b, g, i: (i, 0)),
            pl.BlockSpec((hp, RET_CHUNK, RET_CHUNK), lambda b, g, i: (g, 0, 0)),
            pl.BlockSpec((hp, 3, RET_CHUNK, LANES), lambda b, g, i: (g, 0, 0, 0)),
        ],
        out_specs=pl.BlockSpec((rows, width), lambda b, g, i: (b * nr + i, g)),
        out_shape=jax.ShapeDtypeStruct((batch * seq, RET_HEADS * HEAD_DIM), BF16),
        scratch_shapes=[
            pltpu.VMEM((hp, HEAD_DIM, HEAD_DIM), F32),
            pltpu.VMEM((rows, width), BF16),
            pltpu.VMEM((rows, width), BF16),
            pltpu.VMEM((rows, width), BF16),
            pltpu.VMEM((rows, width), BF16),
            pltpu.VMEM((rows, width), F32),
        ],
        compiler_params=_params("parallel", "parallel", "arbitrary"),
        name="retention",
    )(proj, proj, proj, proj, cos, sin, dm, aux)


def _retention_tables(seq):
    half = HEAD_DIM // 2
    inv = ROPE_BASE ** (-jnp.arange(half, dtype=F32) / half)
    ang = jnp.arange(seq, dtype=F32)[:, None] * inv[None, :]
    cos, sin = jnp.cos(ang), jnp.sin(ang)
    cos_t = jnp.concatenate([cos, cos], axis=-1)
    sin_t = jnp.concatenate([-sin, sin], axis=-1)
    cs = RET_CHUNK
    log_g = jnp.log1p(-jnp.exp2(-5.0 - jnp.arange(RET_HEADS, dtype=F32)))
    pos = jnp.arange(cs, dtype=F32)
    diff = pos[:, None] - pos[None, :]
    dm = jnp.where(diff >= 0, jnp.exp(log_g[:, None, None] * jnp.maximum(diff, 0.0)), 0.0)
    zeta = jnp.exp(log_g[:, None] * (cs - 1 - pos)[None, :])
    xi = jnp.exp(log_g[:, None] * (pos + 1)[None, :])
    g_chunk = jnp.broadcast_to(jnp.exp(log_g * cs)[:, None], (RET_HEADS, cs))
    aux = jnp.broadcast_to(jnp.stack([xi, zeta, g_chunk], axis=1)[..., None], (RET_HEADS, 3, cs, LANES))
    return cos_t, sin_t, dm, aux


def _s5_kernel(u_ref, are_ref, aim_ref, ls_ref, bre_ref, bim_ref, cre_ref, cim_ref, d_ref, y_ref,
               toep_ref, win_ref, woutt_ref, aq_ref, wf_ref, ucat_ref, inc_ref, x_ref, *, q):
    m = u_ref.shape[0] // q
    sl = STATE_LANES

    @pl.when(pl.program_id(1) == 0)
    def _():
        _s5_build_tables(are_ref, aim_ref, ls_ref, bre_ref, bim_ref, cre_ref, cim_ref,
                         toep_ref, win_ref, woutt_ref, aq_ref, wf_ref, q=q)

    for s in range(q):
        ucat_ref[:, s * LANES:(s + 1) * LANES] = u_ref[pl.ds(s, m, stride=q), :].astype(BF16)
    ucat = ucat_ref[...]
    inc_ref[...] = jnp.dot(ucat, win_ref[...], preferred_element_type=F32)
    a_re = aq_ref[0:1, :]
    a_im = aq_ref[1:2, :]

    def body(n, carry):
        x_re, x_im = carry
        x_ref[pl.ds(n, 1), 0:sl] = x_re
        x_ref[pl.ds(n, 1), sl:2 * sl] = x_im
        i_re = inc_ref[pl.ds(n, 1), 0:sl]
        i_im = inc_ref[pl.ds(n, 1), sl:2 * sl]
        return (a_re * x_re - a_im * x_im + i_re, a_re * x_im + a_im * x_re + i_im)

    zero = jnp.zeros((1, sl), F32)
    lax.fori_loop(0, m, body, (zero, zero))
    xb = x_ref[...].astype(BF16)
    nt = (((1,), (1,)), ((), ()))
    for t0 in range(0, q, S5_OUT_STEPS):
        hi = (t0 + S5_OUT_STEPS) * LANES
        cols = slice(t0 * LANES, hi)
        y = jnp.dot(ucat_ref[:, :hi], toep_ref[:hi, cols], preferred_element_type=F32)
        y = y + lax.dot_general(xb, woutt_ref[cols, :], nt, preferred_element_type=F32)
        for t in range(t0, t0 + S5_OUT_STEPS):
            rows = pl.ds(t, m, stride=q)
            y_ref[rows, :] = (y[:, (t - t0) * LANES:(t - t0 + 1) * LANES]
                              + d_ref[...] * u_ref[rows, :])


def s5_core(u_slabs, a_re, a_im, b_re, b_im, c_re, c_im, log_step, d_skip, batch, seq):
    q = S5_CHUNK
    m = seq // q
    g, p = a_re.shape
    gb = GROUPS_PER_BLOCK
    nblk = g // gb
    sl = gb * p

    def rows(t):
        return t.astype(F32).reshape(nblk, 1, sl)

    def b_mat(t):
        return t.astype(F32).reshape(nblk, gb, p, S5_GROUP).transpose(0, 3, 1, 2).reshape(nblk, S5_GROUP, sl)

    def c_mat(t):
        return t.astype(F32).reshape(nblk, gb, S5_GROUP, p).transpose(0, 2, 1, 3).reshape(nblk, S5_GROUP, sl)

    row_spec = pl.BlockSpec((1, 1, sl), lambda j, b: (j, 0, 0))
    mat_spec = pl.BlockSpec((1, S5_GROUP, sl), lambda j, b: (j, 0, 0))
    return pl.pallas_call(
        functools.partial(_s5_kernel, q=q),
        grid=(nblk, batch),
        in_specs=[
            pl.BlockSpec((None, seq, LANES), lambda j, b: (j, b, 0)),
            row_spec, row_spec, row_spec, mat_spec, mat_spec, mat_spec, mat_spec,
            pl.BlockSpec((1, LANES), lambda j, b: (0, j)),
        ],
        out_specs=pl.BlockSpec((None, seq, LANES), lambda j, b: (j, b, 0)),
        out_shape=jax.ShapeDtypeStruct((nblk, batch * seq, LANES), F32),
        scratch_shapes=[
            pltpu.VMEM((q * LANES, q * LANES), BF16),
            pltpu.VMEM((q * LANES, 2 * sl), BF16),
            pltpu.VMEM((q * LANES, 2 * sl), BF16),
            pltpu.VMEM((2, sl), F32),
            pltpu.VMEM((q * LANES, 2 * sl), F32),
            pltpu.VMEM((m, q * LANES), BF16),
            pltpu.VMEM((m, 2 * sl), F32),
            pltpu.VMEM((m, 2 * sl), F32),
        ],
        compiler_params=_params("parallel", "arbitrary"),
        name="s5_core",
    )(u_slabs, rows(a_re), rows(a_im), rows(log_step), b_mat(b_re), b_mat(b_im), c_mat(c_re), c_mat(c_im),
      d_skip)


def _dot_nt_split(x, y):
    nt = (((1,), (1,)), ((), ()))
    xh = x.astype(BF16)
    xl = (x - xh.astype(F32)).astype(BF16)
    yh = y.astype(BF16)
    yl = (y - yh.astype(F32)).astype(BF16)
    out = lax.dot_general(xh, yh, nt, preferred_element_type=F32)
    out = out + lax.dot_general(xh, yl, nt, preferred_element_type=F32)
    return out + lax.dot_general(xl, yh, nt, preferred_element_type=F32)


def _s5_build_tables(are_ref, aim_ref, ls_ref, bre_ref, bim_ref, cre_ref, cim_ref,
                     toep_ref, win_ref, woutt_ref, aq_ref, wf_ref, *, q):
    sl = STATE_LANES
    lam_re = jnp.minimum(are_ref[0], -1e-4)
    lam_im = aim_ref[0]
    step = jnp.exp(ls_ref[0])
    mag = jnp.exp(lam_re * step)
    a_re = mag * jnp.cos(lam_im * step)
    a_im = mag * jnp.sin(lam_im * step)
    den = lam_re * lam_re + lam_im * lam_im
    f_re = ((a_re - 1.0) * lam_re + a_im * lam_im) / den
    f_im = (a_im * lam_re - (a_re - 1.0) * lam_im) / den
    b_re = bre_ref[0]
    b_im = bim_ref[0]
    bb_re = f_re * b_re - f_im * b_im
    bb_im = f_re * b_im + f_im * b_re
    c_re = cre_ref[0]
    c_im = cim_ref[0]

    pows = [(jnp.ones((1, sl), F32), jnp.zeros((1, sl), F32))]
    for _ in range(q):
        p_re, p_im = pows[-1]
        pows.append((p_re * a_re - p_im * a_im, p_re * a_im + p_im * a_re))

    row_group = lax.shift_right_logical(lax.broadcasted_iota(jnp.int32, (LANES, sl), 0), S5_GROUP_SHIFT)
    col_group = lax.shift_right_logical(lax.broadcasted_iota(jnp.int32, (LANES, sl), 1), S5_STATE_SHIFT)
    same_group = row_group == col_group

    def tile(k, m_re, m_im, im_sign):
        p_re, p_im = pows[k]
        v_re = p_re * m_re - p_im * m_im
        v_im = (p_re * m_im + p_im * m_re) * im_sign
        e_re = jnp.where(same_group, jnp.concatenate([v_re] * GROUPS_PER_BLOCK, axis=0), 0.0)
        e_im = jnp.where(same_group, jnp.concatenate([v_im] * GROUPS_PER_BLOCK, axis=0), 0.0)
        return jnp.concatenate([e_re, e_im], axis=1)

    for s in range(q):
        rows = slice(s * LANES, (s + 1) * LANES)
        wf_ref[rows, :] = tile(q - 1 - s, bb_re, bb_im, 1.0)
        woutt_ref[rows, :] = tile(s + 1, c_re, c_im, -1.0).astype(BF16)
    win_ref[...] = wf_ref[...].astype(BF16)
    taps = _dot_nt_split(wf_ref[...], tile(0, c_re, c_im, -1.0))
    toep_ref[...] = jnp.zeros(toep_ref.shape, BF16)
    for s in range(q):
        for t in range(s, q):
            lag_rows = slice((q - 1 - (t - s)) * LANES, (q - (t - s)) * LANES)
            toep_ref[s * LANES:(s + 1) * LANES, t * LANES:(t + 1) * LANES] = taps[lag_rows, :].astype(BF16)
    aq_ref[0:1, :] = pows[q][0]
    aq_ref[1:2, :] = pows[q][1]


def _even_weights(w_in):
    fw = FOX_HEADS * HEAD_DIM
    rw = RET_HEADS * HEAD_DIM
    scale = HEAD_DIM ** -0.5
    c0 = 3 * fw
    c1 = c0 + FOX_HEADS
    fox_scale = jnp.concatenate([jnp.full((fw,), scale * LOG2E, F32), jnp.ones((2 * fw,), F32)])
    ret_scale = jnp.concatenate([jnp.ones((rw,), F32), jnp.full((rw,), scale, F32), jnp.ones((2 * rw,), F32)])
    w_t = jnp.swapaxes(w_in, 1, 2)
    w_main = jnp.concatenate([w_t[:, :c0] * fox_scale[:, None], w_t[:, c1:] * ret_scale[:, None]],
                             axis=1).astype(BF16)
    w_forget = jnp.pad(w_t[:, c0:c1], ((0, 0), (0, LANES - FOX_HEADS), (0, 0))).astype(BF16)
    return w_main, w_forget


def _even_mixer(h, g, weights, e, b_forget, tables, batch, seq):
    w_main, w_forget = weights
    bias = jnp.pad(b_forget.astype(F32), (0, LANES - FOX_HEADS)).reshape(1, LANES)
    t = h.shape[0]
    proj, fl = norm_matmul(h, g, w_main, e, 1, BF16, tm=_row_tile(t, PROJ_ROWS), tn=PROJ_COLS,
                           w_side=w_forget, transposed=True)
    call = fox_cumsum(fl, bias, batch, seq)
    nh = FOX_HEADS
    fox = fox_attention(proj, call, batch, seq, 0, nh, 2 * nh, blk=_row_tile(seq, FOX_BLOCK),
                        hp=FOX_HEADS_PER_STEP)
    cos_t, sin_t, dm, aux = tables
    ret = retention(proj, cos_t, sin_t, dm, aux, batch, seq, 3 * nh, 4 * nh, 5 * nh, 6 * nh,
                    rows=_row_tile(seq, RET_ROWS), hp=RET_HEADS_PER_STEP)
    return fox, ret


def _odd_mixer(h, g, w_in, o, a_re, a_im, b_re, b_im, c_re, c_im, d_skip, log_step, batch, seq):
    sw = a_re.shape[0] * S5_GROUP
    t = h.shape[0]
    conv_proj, u_slabs = norm_matmul(h, g, w_in, o, 1, F32, tm=_row_tile(t, PROJ_ROWS), tn=sw,
                                     slab_first=True)
    y_slabs = s5_core(u_slabs, a_re, a_im, b_re, b_im, c_re, c_im, log_step,
                      d_skip.reshape(1, sw).astype(F32), batch, seq)
    return y_slabs, conv_proj


def kernel(x, mem, norm_mix, norm_xattn, norm_mlp, norm_mem, norm_final, ab_w_in, ab_b_forget, ab_w_out,
           cd_w_in, s5_a_re, s5_a_im, s5_b_re, s5_b_im, s5_c_re, s5_c_im, s5_d, s5_log_step, s5_w_glu,
           conv_w, cd_w_out, xa_wq, xa_wkv, xa_wo, mlp_w1, mlp_w2):
    batch, seq, d = x.shape
    depth = norm_mix.shape[0]
    mlen = mem.shape[1]
    xw = xa_wq.shape[2]
    h = x.reshape(batch * seq, d)
    memf = mem.reshape(batch * mlen, d)
    tables = _retention_tables(seq)
    even_w = _even_weights(ab_w_in)
    odd_w_in = cd_w_in.astype(BF16)
    even_w_out = ab_w_out.astype(BF16)
    odd_w_out = cd_w_out.astype(BF16)
    w_glu = s5_w_glu.astype(BF16)
    conv_wf = conv_w.astype(F32)
    wq = (xa_wq * HEAD_DIM ** -0.5).astype(BF16)
    wo = xa_wo.astype(BF16)
    w1 = mlp_w1
    w2 = mlp_w2
    kv_all = norm_matmul(memf, norm_mem, xa_wkv, 0, depth, BF16,
                         tm=_row_tile(batch * mlen, PROJ_ROWS), tn=PROJ_COLS)
    kv_all = kv_all.reshape(batch, mlen, depth * 2 * xw)
    for layer in range(depth):
        odd = layer % 2 == 1
        idx = layer // 2
        if odd:
            y, proj = _odd_mixer(h, norm_mix[layer], odd_w_in, idx, s5_a_re[idx], s5_a_im[idx],
                                 s5_b_re[idx], s5_b_im[idx], s5_c_re[idx], s5_c_im[idx], s5_d[idx],
                                 s5_log_step[idx], batch, seq)
            mixer_out = (y, proj, conv_wf, w_glu)
            w_out = odd_w_out
        else:
            mixer_out = _even_mixer(h, norm_mix[layer], even_w, idx, ab_b_forget[idx], tables, batch, seq)
            w_out = even_w_out
        h = mix_xattn_block(h, mixer_out, w_out, idx, norm_xattn[layer], wq, kv_all, wo, layer,
                            seq, tm=_row_tile(seq, MIX_ROWS), odd=odd)
        h = mlp_block(h, norm_mlp[layer], w1, w2, layer, norm_final, layer == depth - 1,
                      tm=_row_tile(batch * seq, MLP_ROWS), tf=MLP_COLS)
    return h.reshape(batch, seq, d)
```

```python
import functools
import math

import jax
import jax.numpy as jnp
from jax import lax
from jax.experimental import pallas as pl
from jax.experimental.pallas import tpu as pltpu

F32 = jnp.float32
BF16 = jnp.bfloat16

EPS = 1e-6
ROPE_BASE = 10000.0
LANES = 128
SUBLANES = 8
HEAD_DIM = 128
FOX_HEADS = 8
RET_HEADS = 8
RET_CHUNK = 256
XA_HEADS = 4
S5_GROUP = 16
S5_STATE = 64
S5_CHUNK = 16
S5_SCAN_UNROLL = 8
MXU_TILE = 256
S5_OUT_STEPS = MXU_TILE // LANES
CONV_K = 3
S5_GROUP_SHIFT = S5_GROUP.bit_length() - 1
S5_STATE_SHIFT = S5_STATE.bit_length() - 1
GROUPS_PER_BLOCK = LANES // S5_GROUP
STATE_LANES = GROUPS_PER_BLOCK * S5_STATE
VMEM_LIMIT = 56 * 1024 * 1024
NEG_BIG = -1e30
LOG2E = math.log2(math.e)

PROJ_ROWS, PROJ_COLS = 1024, 1024
MIX_ROWS = 512
MLP_ROWS, MLP_COLS = 1024, 512
FOX_BLOCK = 1024
FOX_HEADS_PER_STEP = 2
RET_ROWS = 1024
RET_HEADS_PER_STEP = 8


def _row_tile(rows, want):
    tile = min(rows, want)
    while rows % tile:
        tile //= 2
    return tile


def _params(*sem):
    return pltpu.CompilerParams(dimension_semantics=sem, vmem_limit_bytes=VMEM_LIMIT)


def _rms(x, g):
    ms = jnp.mean(x * x, axis=-1, keepdims=True)
    return x * lax.rsqrt(ms + EPS) * g


def _norm_matmul_kernel(*refs, side, transposed):
    if side:
        x_ref, g_ref, w_ref, ws_ref, o_ref, os_ref, xn_ref = refs
    else:
        x_ref, g_ref, w_ref, o_ref, xn_ref = refs
    dims = (((1,), (1 if transposed else 0,)), ((), ()))

    @pl.when(pl.program_id(1) == 0)
    def _():
        xn_ref[...] = _rms(x_ref[...], g_ref[...]).astype(BF16)
        if side:
            os_ref[...] = lax.dot_general(xn_ref[...], ws_ref[...], dims, preferred_element_type=F32)

    o_ref[...] = lax.dot_general(xn_ref[...], w_ref[...].astype(BF16), dims,
                                 preferred_element_type=F32).astype(o_ref.dtype)


def norm_matmul(x, g, w, layer, n_layers, out_dtype, tm, tn, w_side=None, transposed=False):
    t, d = x.shape
    n = w.shape[1 if transposed else 2]
    nj = n // tn

    def w_spec(cols, index):
        if transposed:
            return pl.BlockSpec((None, cols, d), lambda i, j: (index(j)[0], index(j)[1], 0))
        return pl.BlockSpec((None, d, cols), lambda i, j: (index(j)[0], 0, index(j)[1]))

    side = w_side is not None
    in_specs = [
        pl.BlockSpec((tm, d), lambda i, j: (i, 0)),
        pl.BlockSpec((1, d), lambda i, j: (0, 0)),
        w_spec(tn, lambda j: (layer + j // nj, j % nj)),
    ]
    out_specs = pl.BlockSpec((tm, tn), lambda i, j: (i, j))
    out_shape = jax.ShapeDtypeStruct((t, n_layers * n), out_dtype)
    args = (x, g.reshape(1, d), w)
    if side:
        ns = w_side.shape[1 if transposed else 2]
        in_specs.append(w_spec(ns, lambda j: (layer, 0)))
        out_specs = [out_specs, pl.BlockSpec((tm, ns), lambda i, j: (i, 0))]
        out_shape = [out_shape, jax.ShapeDtypeStruct((t, ns), F32)]
        args = args + (w_side,)
    return pl.pallas_call(
        functools.partial(_norm_matmul_kernel, side=side, transposed=transposed),
        grid=(t // tm, n_layers * nj),
        in_specs=in_specs,
        out_specs=out_specs,
        out_shape=out_shape,
        scratch_shapes=[pltpu.VMEM((tm, d), BF16)],
        compiler_params=_params("parallel", "arbitrary"),
        name="norm_matmul",
    )(*args)


def _mlp_kernel(h_ref, g_ref, w1_ref, w2_ref, gf_ref, o_ref, xn_ref, *, final_norm):
    f = pl.program_id(1)

    @pl.when(f == 0)
    def _():
        x = h_ref[...]
        xn_ref[...] = _rms(x, g_ref[...]).astype(BF16)
        o_ref[...] = x

    a = jnp.dot(xn_ref[...], w1_ref[...].astype(BF16), preferred_element_type=F32)
    a = jnp.square(jnp.maximum(a, 0.0)).astype(BF16)
    o_ref[...] += jnp.dot(a, w2_ref[...].astype(BF16), preferred_element_type=F32)

    if final_norm:
        @pl.when(f == pl.num_programs(1) - 1)
        def _():
            o_ref[...] = _rms(o_ref[...], gf_ref[...])


def mlp_block(h, g, w1, w2, layer, g_final, final_norm, tm, tf):
    t, d = h.shape
    dff = w1.shape[2]
    return pl.pallas_call(
        functools.partial(_mlp_kernel, final_norm=final_norm),
        grid=(t // tm, dff // tf),
        in_specs=[
            pl.BlockSpec((tm, d), lambda i, f: (i, 0)),
            pl.BlockSpec((1, d), lambda i, f: (0, 0)),
            pl.BlockSpec((None, d, tf), lambda i, f: (layer, 0, f)),
            pl.BlockSpec((None, tf, d), lambda i, f: (layer, f, 0)),
            pl.BlockSpec((1, d), lambda i, f: (0, 0)),
        ],
        out_specs=pl.BlockSpec((tm, d), lambda i, f: (i, 0)),
        out_shape=jax.ShapeDtypeStruct((t, d), F32),
        scratch_shapes=[pltpu.VMEM((tm, d), BF16)],
        compiler_params=_params("parallel", "arbitrary"),
        name="mlp_block",
    )(h, g.reshape(1, d), w1, w2, g_final.reshape(1, d))


def _s5_gate(y, w_glu):
    g = 0.5 * y * (1.0 + jnp.tanh(math.sqrt(2.0 / math.pi) * (y + 0.044715 * (y * y * y))))
    z = jnp.dot(g.astype(BF16), w_glu, preferred_element_type=F32)
    return (g * jax.nn.sigmoid(z)).astype(BF16)


def _short_conv(hc, gb, gc, z_prev, w_ref):
    z = gc * hc
    row = lax.broadcasted_iota(jnp.int32, z.shape, 0)
    y = w_ref[CONV_K - 1:CONV_K, :] * z
    for lag in range(1, CONV_K):
        zl = pltpu.roll(z, lag, 0)
        for r in range(lag):
            src = SUBLANES - lag + r
            zl = jnp.where(row == r, z_prev[src:src + 1, :], zl)
        y = y + w_ref[CONV_K - 1 - lag:CONV_K - lag, :] * zl
    return (gb * y).astype(BF16)


def _mix_xattn_kernel(*refs, odd, per_batch):
    if odd:
        (h_ref, y_ref, hc_ref, gb_ref, gc_ref, hcp_ref, gcp_ref, cw_ref, wglu_ref,
         w1_ref, w2_ref, g_ref, wq_ref, k_ref, v_ref, wo_ref, o_ref) = refs
        a1 = _s5_gate(y_ref[...], wglu_ref[...])
        first = pl.program_id(0) % per_batch == 0
        z_prev = jnp.where(first, 0.0, gcp_ref[...] * hcp_ref[...])
        a2 = _short_conv(hc_ref[...], gb_ref[...], gc_ref[...], z_prev, cw_ref)
    else:
        h_ref, a1_ref, a2_ref, w1_ref, w2_ref, g_ref, wq_ref, k_ref, v_ref, wo_ref, o_ref = refs
        a1 = a1_ref[...]
        a2 = a2_ref[...]
    x = h_ref[...] + jnp.dot(a1, w1_ref[...], preferred_element_type=F32)
    x = x + jnp.dot(a2, w2_ref[...], preferred_element_type=F32)
    xn = _rms(x, g_ref[...]).astype(BF16)
    q = jnp.dot(xn, wq_ref[...], preferred_element_type=F32).astype(BF16)
    heads = []
    for hd in range(XA_HEADS):
        sl = slice(hd * HEAD_DIM, (hd + 1) * HEAD_DIM)
        s = lax.dot_general(q[:, sl], k_ref[0, :, sl], (((1,), (1,)), ((), ())),
                            preferred_element_type=F32)
        m = jnp.max(s, axis=-1, keepdims=True)
        p = jnp.exp(s - m)
        l = jnp.sum(p, axis=-1, keepdims=True)
        oh = jnp.dot(p.astype(BF16), v_ref[0, :, sl], preferred_element_type=F32)
        heads.append((oh * (1.0 / l)).astype(BF16))
    o = jnp.concatenate(heads, axis=-1)
    o_ref[...] = x + jnp.dot(o, wo_ref[...], preferred_element_type=F32)


def mix_xattn_block(h, mixer_inputs, w_out, mixer_idx, g, wq, kv, wo, layer, seq, tm, odd):
    t, d = h.shape
    k1 = w_out.shape[1] // 2
    k2 = k1
    mlen = kv.shape[1]
    xw = wq.shape[2]
    per_batch = seq // tm

    def fixed(shape, *index):
        return pl.BlockSpec(shape, lambda i: index, pipeline_mode=pl.Buffered(1))

    def rows(width, col=0):
        return pl.BlockSpec((tm, width), lambda i: (i, col))

    if odd:
        y, proj, conv_w, w_glu = mixer_inputs
        halo = tm // SUBLANES

        def prev_rows(col):
            return pl.BlockSpec((SUBLANES, k2), lambda i: (jnp.maximum(i * halo - 1, 0), col))

        mixer_args = (y, proj, proj, proj, proj, proj, conv_w, w_glu)
        mixer_specs = [rows(k1), rows(k2, 1), rows(k2, 2), rows(k2, 3), prev_rows(1), prev_rows(3),
                       fixed((None,) + conv_w.shape[1:], mixer_idx, 0, 0),
                       fixed((None,) + w_glu.shape[1:], mixer_idx, 0, 0)]
    else:
        mixer_args = mixer_inputs
        mixer_specs = [rows(k1), rows(k2)]

    return pl.pallas_call(
        functools.partial(_mix_xattn_kernel, odd=odd, per_batch=per_batch),
        grid=(t // tm,),
        in_specs=[rows(d)] + mixer_specs + [
            fixed((None, k1, d), mixer_idx, 0, 0), fixed((None, k2, d), mixer_idx, 1, 0),
            fixed((1, d), 0, 0), fixed((None, d, xw), layer, 0, 0),
            pl.BlockSpec((1, mlen, xw), lambda i: (i // per_batch, 0, 2 * layer)),
            pl.BlockSpec((1, mlen, xw), lambda i: (i // per_batch, 0, 2 * layer + 1)),
            fixed((None, xw, d), layer, 0, 0),
        ],
        out_specs=rows(d),
        out_shape=jax.ShapeDtypeStruct((t, d), F32),
        compiler_params=_params("parallel"),
        name="mix_xattn_block",
    )(h, *mixer_args, w_out, w_out, g.reshape(1, d), wq, kv, kv, wo)


def _split3(x):
    hi = x.astype(BF16)
    r1 = x - hi.astype(F32)
    mid = r1.astype(BF16)
    lo = (r1 - mid.astype(F32)).astype(BF16)
    return hi, mid, lo


def _fox_cumsum_kernel(fl_ref, b_ref, call_ref, *, blk):
    seq = fl_ref.shape[0]
    row = lax.broadcasted_iota(jnp.int32, (blk, blk), 0)
    col = lax.broadcasted_iota(jnp.int32, (blk, blk), 1)
    tri = jnp.where(row >= col, 1.0, 0.0).astype(BF16)
    carry = jnp.zeros((1, LANES), F32)
    for i in range(seq // blk):
        rows = slice(i * blk, (i + 1) * blk)
        x = fl_ref[rows, :] + b_ref[...]
        ls = (jnp.minimum(x, 0.0) - jnp.log(1.0 + jnp.exp(-jnp.abs(x)))) * LOG2E
        c = carry
        for piece in _split3(ls):
            c = c + jnp.dot(tri, piece, preferred_element_type=F32)
        call_ref[rows, :] = c
        carry = c[blk - 1:blk, :]


def fox_cumsum(fl, bias, batch, seq):
    blk = min(MXU_TILE, seq)
    return pl.pallas_call(
        functools.partial(_fox_cumsum_kernel, blk=blk),
        grid=(batch,),
        in_specs=[
            pl.BlockSpec((seq, LANES), lambda b: (b, 0)),
            pl.BlockSpec((1, LANES), lambda b: (0, 0)),
        ],
        out_specs=pl.BlockSpec((seq, LANES), lambda b: (b, 0)),
        out_shape=jax.ShapeDtypeStruct((batch * seq, LANES), F32),
        compiler_params=_params("parallel"),
        name="fox_cumsum",
    )(fl, bias)


def _lane_tile(x, reps):
    return jnp.concatenate([x] * reps, axis=1)


def _gate_lanes(c_col, lane, own_first):
    hi, mid, lo = [p.astype(F32) for p in _split3(c_col)]
    base = 0 if own_first else 3
    pieces = jnp.where(lane == base, hi, jnp.where(lane == base + 1, mid, jnp.where(lane == base + 2, lo, 0.0)))
    ones = jnp.where((lane >= 3 - base) & (lane < 6 - base), 1.0, 0.0)
    return (pieces + ones).astype(BF16)


def _fox_attn_kernel(q_ref, k_ref, v_ref, c_ref, o_ref, qaug_ref, kaug_ref, vaug_ref, m_ref, acc_ref,
                     *, blk, hp):
    g = pl.program_id(1)
    i = pl.program_id(2)
    seq = k_ref.shape[0]
    nt = (((1,), (1,)), ((), ()))

    def head_cols(t):
        return slice(t * HEAD_DIM, (t + 1) * HEAD_DIM)

    @pl.when(i == 0)
    def _():
        lane = lax.broadcasted_iota(jnp.int32, (seq, LANES), 1)
        for t in range(hp):
            c_key = jnp.sum(jnp.where(lane == g * hp + t, c_ref[...], 0.0), axis=1, keepdims=True)
            kaug_ref[t, :, :HEAD_DIM] = k_ref[:, head_cols(t)]
            kaug_ref[t, :, HEAD_DIM:] = _gate_lanes(-c_key, lane, own_first=False)
            vaug_ref[t, :, :HEAD_DIM] = v_ref[:, head_cols(t)]
            vaug_ref[t, :, HEAD_DIM:] = jnp.ones((seq, LANES), BF16)

    lane = lax.broadcasted_iota(jnp.int32, (blk, LANES), 1)
    c_rows = c_ref[pl.ds(pl.multiple_of(i * blk, blk), blk), :]
    for t in range(hp):
        c_query = jnp.sum(jnp.where(lane == g * hp + t, c_rows, 0.0), axis=1, keepdims=True)
        qaug_ref[t, :, :HEAD_DIM] = q_ref[:, head_cols(t)]
        qaug_ref[t, :, HEAD_DIM:] = _gate_lanes(c_query, lane, own_first=True)
    m_ref[...] = jnp.full(m_ref.shape, NEG_BIG, F32)
    acc_ref[...] = jnp.zeros(acc_ref.shape, F32)

    def update(rows, keys, diag_shift):
        nr, nk = rows.stop - rows.start, keys.size
        scores = [lax.dot_general(qaug_ref[t, rows], kaug_ref[t, keys, :], nt,
                                  preferred_element_type=F32) for t in range(hp)]
        for t in range(hp):
            s = scores[t]
            if diag_shift is not None:
                r = lax.broadcasted_iota(jnp.int32, (nr, nk), 0)
                c = lax.broadcasted_iota(jnp.int32, (nr, nk), 1)
                s = jnp.where(r + diag_shift >= c, s, -jnp.inf)
            m_prev = m_ref[t, rows]
            m_new = jnp.maximum(m_prev, jnp.max(s, axis=-1, keepdims=True))
            alpha = jnp.exp2(m_prev - m_new)
            p = jnp.exp2(s - _lane_tile(m_new, nk // LANES)).astype(BF16)
            acc_ref[t, rows] = (_lane_tile(alpha, acc_ref.shape[2] // LANES) * acc_ref[t, rows]
                                + jnp.dot(p, vaug_ref[t, keys, :], preferred_element_type=F32))
            m_ref[t, rows] = m_new

    def body(j, carry):
        update(slice(0, blk), pl.ds(pl.multiple_of(j * blk, blk), blk), None)
        return carry

    lax.fori_loop(0, i, body, 0)
    half = blk // 2
    start = pl.multiple_of(i * blk, blk)
    update(slice(0, half), pl.ds(start, half), 0)
    update(slice(half, blk), pl.ds(start, blk), half)
    for t in range(hp):
        acc = acc_ref[t]
        o_ref[:, head_cols(t)] = (acc[:, :HEAD_DIM] * (1.0 / acc[:, HEAD_DIM:])).astype(o_ref.dtype)


def fox_attention(proj, call, batch, seq, q_col, k_col, v_col, blk, hp):
    nq = seq // blk
    width = hp * HEAD_DIM
    aug = HEAD_DIM + LANES
    return pl.pallas_call(
        functools.partial(_fox_attn_kernel, blk=blk, hp=hp),
        grid=(batch, FOX_HEADS // hp, nq),
        in_specs=[
            pl.BlockSpec((blk, width), lambda b, g, i: (b * nq + i, q_col // hp + g)),
            pl.BlockSpec((seq, width), lambda b, g, i: (b, k_col // hp + g)),
            pl.BlockSpec((seq, width), lambda b, g, i: (b, v_col // hp + g)),
            pl.BlockSpec((seq, LANES), lambda b, g, i: (b, 0)),
        ],
        out_specs=pl.BlockSpec((blk, width), lambda b, g, i: (b * nq + i, g)),
        out_shape=jax.ShapeDtypeStruct((batch * seq, FOX_HEADS * HEAD_DIM), BF16),
        scratch_shapes=[
            pltpu.VMEM((hp, blk, aug), BF16),
            pltpu.VMEM((hp, seq, aug), BF16),
            pltpu.VMEM((hp, seq, aug), BF16),
            pltpu.VMEM((hp, blk, LANES), F32),
            pltpu.VMEM((hp, blk, aug), F32),
        ],
        compiler_params=_params("parallel", "parallel", "arbitrary"),
        name="fox_attention",
    )(proj, proj, proj, call)


def _retention_kernel(q_ref, k_ref, v_ref, g_ref, cos_ref, sin_ref, dm_ref, aux_ref, o_ref,
                      r_ref, qr_ref, qx_ref, kr_ref, kz_ref, of_ref, *, cs, hp):
    rows_per_step = q_ref.shape[0]
    n_chunks = rows_per_step // cs
    half = HEAD_DIM // 2
    nt = (((1,), (1,)), ((), ()))

    @pl.when(pl.program_id(2) == 0)
    def _():
        r_ref[...] = jnp.zeros(r_ref.shape, F32)

    def head_cols(t):
        return slice(t * HEAD_DIM, (t + 1) * HEAD_DIM)

    cos = cos_ref[...]
    sin = sin_ref[...]
    for t in range(hp):
        q = q_ref[:, head_cols(t)].astype(F32)
        k = k_ref[:, head_cols(t)].astype(F32)
        qr = q * cos + pltpu.roll(q, half, 1) * sin
        kr = k * cos + pltpu.roll(k, half, 1) * sin
        xi = jnp.concatenate([aux_ref[t, 0]] * n_chunks, axis=0)
        zeta = jnp.concatenate([aux_ref[t, 1]] * n_chunks, axis=0)
        qr_ref[:, head_cols(t)] = qr.astype(BF16)
        qx_ref[:, head_cols(t)] = (qr * xi).astype(BF16)
        kr_ref[:, head_cols(t)] = kr.astype(BF16)
        kz_ref[:, head_cols(t)] = (kr * zeta).astype(BF16)

    def body(c, carry):
        rows = pl.ds(pl.multiple_of(c * cs, cs), cs)
        for t in range(hp):
            v = v_ref[rows, head_cols(t)]
            s = lax.dot_general(qr_ref[rows, head_cols(t)], kr_ref[rows, head_cols(t)], nt,
                                preferred_element_type=F32) * dm_ref[t]
            r = r_ref[t]
            lhs = jnp.concatenate([s.astype(BF16), qx_ref[rows, head_cols(t)]], axis=1)
            rhs = jnp.concatenate([v, r.astype(BF16)], axis=0)
            of_ref[rows, head_cols(t)] = jnp.dot(lhs, rhs, preferred_element_type=F32)
            r_ref[t] = aux_ref[t, 2, :HEAD_DIM] * r + lax.dot_general(
                kz_ref[rows, head_cols(t)], v, (((0,), (0,)), ((), ())), preferred_element_type=F32)
        return carry

    lax.fori_loop(0, n_chunks, body, 0)

    for t in range(hp):
        o = of_ref[:, head_cols(t)]
        mu = jnp.mean(o, axis=-1, keepdims=True)
        oc = o - mu
        var = jnp.mean(oc * oc, axis=-1, keepdims=True)
        on = oc * lax.rsqrt(var + EPS)
        g = g_ref[:, head_cols(t)].astype(F32)
        o_ref[:, head_cols(t)] = (g * jax.nn.sigmoid(g) * on).astype(o_ref.dtype)


def retention(proj, cos, sin, dm, aux, batch, seq, q_col, k_col, v_col, g_col, rows, hp):
    width = hp * HEAD_DIM
    nr = seq // rows

    def col(c0):
        return pl.BlockSpec((rows, width), lambda b, g, i: (b * nr + i, c0 // hp + g))

    return pl.pallas_call(
        functools.partial(_retention_kernel, cs=RET_CHUNK, hp=hp),
        grid=(batch, RET_HEADS // hp, nr),
        in_specs=[
            col(q_col), col(k_col), col(v_col), col(g_col),
            pl.BlockSpec((rows, HEAD_DIM), lambda b, g, i: (i, 0)),
            pl.BlockSpec((rows, HEAD_DIM), lambda b, g, i: (i, 0)),
            pl.BlockSpec((hp, RET_CHUNK, RET_CHUNK), lambda b, g, i: (g, 0, 0)),
            pl.BlockSpec((hp, 3, RET_CHUNK, LANES), lambda b, g, i: (g, 0, 0, 0)),
        ],
        out_specs=pl.BlockSpec((rows, width), lambda b, g, i: (b * nr + i, g)),
        out_shape=jax.ShapeDtypeStruct((batch * seq, RET_HEADS * HEAD_DIM), BF16),
        scratch_shapes=[
            pltpu.VMEM((hp, HEAD_DIM, HEAD_DIM), F32),
            pltpu.VMEM((rows, width), BF16),
            pltpu.VMEM((rows, width), BF16),
            pltpu.VMEM((rows, width), BF16),
            pltpu.VMEM((rows, width), BF16),
            pltpu.VMEM((rows, width), F32),
        ],
        compiler_params=_params("parallel", "parallel", "arbitrary"),
        name="retention",
    )(proj, proj, proj, proj, cos, sin, dm, aux)


def _retention_tables(seq):
    half = HEAD_DIM // 2
    inv = ROPE_BASE ** (-jnp.arange(half, dtype=F32) / half)
    ang = jnp.arange(seq, dtype=F32)[:, None] * inv[None, :]
    cos, sin = jnp.cos(ang), jnp.sin(ang)
    cos_t = jnp.concatenate([cos, cos], axis=-1)
    sin_t = jnp.concatenate([-sin, sin], axis=-1)
    cs = RET_CHUNK
    log_g = jnp.log1p(-jnp.exp2(-5.0 - jnp.arange(RET_HEADS, dtype=F32)))
    pos = jnp.arange(cs, dtype=F32)
    diff = pos[:, None] - pos[None, :]
    dm = jnp.where(diff >= 0, jnp.exp(log_g[:, None, None] * jnp.maximum(diff, 0.0)), 0.0)
    zeta = jnp.exp(log_g[:, None] * (cs - 1 - pos)[None, :])
    xi = jnp.exp(log_g[:, None] * (pos + 1)[None, :])
    g_chunk = jnp.broadcast_to(jnp.exp(log_g * cs)[:, None], (RET_HEADS, cs))
    aux = jnp.broadcast_to(jnp.stack([xi, zeta, g_chunk], axis=1)[..., None], (RET_HEADS, 3, cs, LANES))
    return cos_t, sin_t, dm, aux


def _s5_kernel(u_ref, are_ref, aim_ref, ls_ref, bre_ref, bim_ref, cre_ref, cim_ref, d_ref, y_ref,
               toep_ref, win_ref, woutt_ref, aq_ref, wf_ref, ucat_ref, inc_ref, x_ref, *, q):
    m = u_ref.shape[0] // q
    sl = STATE_LANES

    @pl.when(pl.program_id(1) == 0)
    def _():
        _s5_build_tables(are_ref, aim_ref, ls_ref, bre_ref, bim_ref, cre_ref, cim_ref,
                         toep_ref, win_ref, woutt_ref, aq_ref, wf_ref, q=q)

    for s in range(q):
        ucat_ref[:, s * LANES:(s + 1) * LANES] = u_ref[pl.ds(s, m, stride=q), :].astype(BF16)
    ucat = ucat_ref[...]
    inc_ref[...] = jnp.dot(ucat, win_ref[...], preferred_element_type=F32)
    a_re = aq_ref[0:1, :]
    a_im = aq_ref[1:2, :]

    def body(n, carry):
        x_re, x_im = carry
        x_ref[pl.ds(n, 1), 0:sl] = x_re
        x_ref[pl.ds(n, 1), sl:2 * sl] = x_im
        i_re = inc_ref[pl.ds(n, 1), 0:sl]
        i_im = inc_ref[pl.ds(n, 1), sl:2 * sl]
        return (a_re * x_re - a_im * x_im + i_re, a_re * x_im + a_im * x_re + i_im)

    zero = jnp.zeros((1, sl), F32)
    lax.fori_loop(0, m, body, (zero, zero), unroll=S5_SCAN_UNROLL)
    xb = x_ref[...].astype(BF16)
    nt = (((1,), (1,)), ((), ()))
    for t0 in range(0, q, S5_OUT_STEPS):
        hi = (t0 + S5_OUT_STEPS) * LANES
        cols = slice(t0 * LANES, hi)
        y = jnp.dot(ucat_ref[:, :hi], toep_ref[:hi, cols], preferred_element_type=F32)
        y = y + lax.dot_general(xb, woutt_ref[cols, :], nt, preferred_element_type=F32)
        for t in range(t0, t0 + S5_OUT_STEPS):
            rows = pl.ds(t, m, stride=q)
            y_ref[rows, :] = (y[:, (t - t0) * LANES:(t - t0 + 1) * LANES]
                              + d_ref[...] * u_ref[rows, :])


def s5_core(proj, a_re, a_im, b_re, b_im, c_re, c_im, log_step, d_skip, batch, seq, u_col):
    q = S5_CHUNK
    m = seq // q
    g, p = a_re.shape
    gb = GROUPS_PER_BLOCK
    nblk = g // gb
    sl = gb * p

    def rows(t):
        return t.astype(F32).reshape(nblk, 1, sl)

    def b_mat(t):
        return t.astype(F32).reshape(nblk, gb, p, S5_GROUP).transpose(0, 3, 1, 2).reshape(nblk, S5_GROUP, sl)

    def c_mat(t):
        return t.astype(F32).reshape(nblk, gb, S5_GROUP, p).transpose(0, 2, 1, 3).reshape(nblk, S5_GROUP, sl)

    row_spec = pl.BlockSpec((1, 1, sl), lambda j, b: (j, 0, 0))
    mat_spec = pl.BlockSpec((1, S5_GROUP, sl), lambda j, b: (j, 0, 0))
    return pl.pallas_call(
        functools.partial(_s5_kernel, q=q),
        grid=(nblk, batch),
        in_specs=[
            pl.BlockSpec((seq, LANES), lambda j, b: (b, u_col + j)),
            row_spec, row_spec, row_spec, mat_spec, mat_spec, mat_spec, mat_spec,
            pl.BlockSpec((1, LANES), lambda j, b: (0, j)),
        ],
        out_specs=pl.BlockSpec((seq, LANES), lambda j, b: (b, j)),
        out_shape=jax.ShapeDtypeStruct((batch * seq, nblk * LANES), F32),
        scratch_shapes=[
            pltpu.VMEM((q * LANES, q * LANES), BF16),
            pltpu.VMEM((q * LANES, 2 * sl), BF16),
            pltpu.VMEM((q * LANES, 2 * sl), BF16),
            pltpu.VMEM((2, sl), F32),
            pltpu.VMEM((q * LANES, 2 * sl), F32),
            pltpu.VMEM((m, q * LANES), BF16),
            pltpu.VMEM((m, 2 * sl), F32),
            pltpu.VMEM((m, 2 * sl), F32),
        ],
        compiler_params=_params("parallel", "arbitrary"),
        name="s5_core",
    )(proj, rows(a_re), rows(a_im), rows(log_step), b_mat(b_re), b_mat(b_im), c_mat(c_re), c_mat(c_im),
      d_skip)


def _dot_nt_split(x, y):
    nt = (((1,), (1,)), ((), ()))
    xh = x.astype(BF16)
    xl = (x - xh.astype(F32)).astype(BF16)
    yh = y.astype(BF16)
    yl = (y - yh.astype(F32)).astype(BF16)
    out = lax.dot_general(xh, yh, nt, preferred_element_type=F32)
    out = out + lax.dot_general(xh, yl, nt, preferred_element_type=F32)
    return out + lax.dot_general(xl, yh, nt, preferred_element_type=F32)


def _s5_build_tables(are_ref, aim_ref, ls_ref, bre_ref, bim_ref, cre_ref, cim_ref,
                     toep_ref, win_ref, woutt_ref, aq_ref, wf_ref, *, q):
    sl = STATE_LANES
    lam_re = jnp.minimum(are_ref[0], -1e-4)
    lam_im = aim_ref[0]
    step = jnp.exp(ls_ref[0])
    mag = jnp.exp(lam_re * step)
    a_re = mag * jnp.cos(lam_im * step)
    a_im = mag * jnp.sin(lam_im * step)
    den = lam_re * lam_re + lam_im * lam_im
    f_re = ((a_re - 1.0) * lam_re + a_im * lam_im) / den
    f_im = (a_im * lam_re - (a_re - 1.0) * lam_im) / den
    b_re = bre_ref[0]
    b_im = bim_ref[0]
    bb_re = f_re * b_re - f_im * b_im
    bb_im = f_re * b_im + f_im * b_re
    c_re = cre_ref[0]
    c_im = cim_ref[0]

    pows = [(jnp.ones((1, sl), F32), jnp.zeros((1, sl), F32))]
    for _ in range(q):
        p_re, p_im = pows[-1]
        pows.append((p_re * a_re - p_im * a_im, p_re * a_im + p_im * a_re))

    row_group = lax.shift_right_logical(lax.broadcasted_iota(jnp.int32, (LANES, sl), 0), S5_GROUP_SHIFT)
    col_group = lax.shift_right_logical(lax.broadcasted_iota(jnp.int32, (LANES, sl), 1), S5_STATE_SHIFT)
    same_group = row_group == col_group

    def tile(k, m_re, m_im, im_sign):
        p_re, p_im = pows[k]
        v_re = p_re * m_re - p_im * m_im
        v_im = (p_re * m_im + p_im * m_re) * im_sign
        e_re = jnp.where(same_group, jnp.concatenate([v_re] * GROUPS_PER_BLOCK, axis=0), 0.0)
        e_im = jnp.where(same_group, jnp.concatenate([v_im] * GROUPS_PER_BLOCK, axis=0), 0.0)
        return jnp.concatenate([e_re, e_im], axis=1)

    for s in range(q):
        rows = slice(s * LANES, (s + 1) * LANES)
        wf_ref[rows, :] = tile(q - 1 - s, bb_re, bb_im, 1.0)
        woutt_ref[rows, :] = tile(s + 1, c_re, c_im, -1.0).astype(BF16)
    win_ref[...] = wf_ref[...].astype(BF16)
    taps = _dot_nt_split(wf_ref[...], tile(0, c_re, c_im, -1.0))
    toep_ref[...] = jnp.zeros(toep_ref.shape, BF16)
    for s in range(q):
        for t in range(s, q):
            lag_rows = slice((q - 1 - (t - s)) * LANES, (q - (t - s)) * LANES)
            toep_ref[s * LANES:(s + 1) * LANES, t * LANES:(t + 1) * LANES] = taps[lag_rows, :].astype(BF16)
    aq_ref[0:1, :] = pows[q][0]
    aq_ref[1:2, :] = pows[q][1]


def _even_weights(w_in):
    fw = FOX_HEADS * HEAD_DIM
    rw = RET_HEADS * HEAD_DIM
    scale = HEAD_DIM ** -0.5
    c0 = 3 * fw
    c1 = c0 + FOX_HEADS
    fox_scale = jnp.concatenate([jnp.full((fw,), scale * LOG2E, F32), jnp.ones((2 * fw,), F32)])
    ret_scale = jnp.concatenate([jnp.ones((rw,), F32), jnp.full((rw,), scale, F32), jnp.ones((2 * rw,), F32)])
    w_t = jnp.swapaxes(w_in, 1, 2)
    w_main = jnp.concatenate([w_t[:, :c0] * fox_scale[:, None], w_t[:, c1:] * ret_scale[:, None]],
                             axis=1).astype(BF16)
    w_forget = jnp.pad(w_t[:, c0:c1], ((0, 0), (0, LANES - FOX_HEADS), (0, 0))).astype(BF16)
    return w_main, w_forget


def _even_mixer(h, g, weights, e, b_forget, tables, batch, seq):
    w_main, w_forget = weights
    bias = jnp.pad(b_forget.astype(F32), (0, LANES - FOX_HEADS)).reshape(1, LANES)
    t = h.shape[0]
    proj, fl = norm_matmul(h, g, w_main, e, 1, BF16, tm=_row_tile(t, PROJ_ROWS), tn=PROJ_COLS,
                           w_side=w_forget, transposed=True)
    call = fox_cumsum(fl, bias, batch, seq)
    nh = FOX_HEADS
    fox = fox_attention(proj, call, batch, seq, 0, nh, 2 * nh, blk=_row_tile(seq, FOX_BLOCK),
                        hp=FOX_HEADS_PER_STEP)
    cos_t, sin_t, dm, aux = tables
    ret = retention(proj, cos_t, sin_t, dm, aux, batch, seq, 3 * nh, 4 * nh, 5 * nh, 6 * nh,
                    rows=_row_tile(seq, RET_ROWS), hp=RET_HEADS_PER_STEP)
    return fox, ret


def _odd_mixer(h, g, w_in, o, a_re, a_im, b_re, b_im, c_re, c_im, d_skip, log_step, batch, seq):
    sw = a_re.shape[0] * S5_GROUP
    t = h.shape[0]
    proj = norm_matmul(h, g, w_in, o, 1, F32, tm=_row_tile(t, PROJ_ROWS), tn=PROJ_COLS)
    y = s5_core(proj, a_re, a_im, b_re, b_im, c_re, c_im, log_step, d_skip.reshape(1, sw).astype(F32),
                batch, seq, 0)
    return y, proj


def kernel(x, mem, norm_mix, norm_xattn, norm_mlp, norm_mem, norm_final, ab_w_in, ab_b_forget, ab_w_out,
           cd_w_in, s5_a_re, s5_a_im, s5_b_re, s5_b_im, s5_c_re, s5_c_im, s5_d, s5_log_step, s5_w_glu,
           conv_w, cd_w_out, xa_wq, xa_wkv, xa_wo, mlp_w1, mlp_w2):
    batch, seq, d = x.shape
    depth = norm_mix.shape[0]
    mlen = mem.shape[1]
    xw = xa_wq.shape[2]
    h = x.reshape(batch * seq, d)
    memf = mem.reshape(batch * mlen, d)
    tables = _retention_tables(seq)
    even_w = _even_weights(ab_w_in)
    odd_w_in = cd_w_in.astype(BF16)
    even_w_out = ab_w_out.astype(BF16)
    odd_w_out = cd_w_out.astype(BF16)
    w_glu = s5_w_glu.astype(BF16)
    conv_wf = conv_w.astype(F32)
    wq = (xa_wq * HEAD_DIM ** -0.5).astype(BF16)
    wo = xa_wo.astype(BF16)
    w1 = mlp_w1
    w2 = mlp_w2
    kv_all = norm_matmul(memf, norm_mem, xa_wkv, 0, depth, BF16,
                         tm=_row_tile(batch * mlen, PROJ_ROWS), tn=PROJ_COLS)
    kv_all = kv_all.reshape(batch, mlen, depth * 2 * xw)
    for layer in range(depth):
        odd = layer % 2 == 1
        idx = layer // 2
        if odd:
            y, proj = _odd_mixer(h, norm_mix[layer], odd_w_in, idx, s5_a_re[idx], s5_a_im[idx],
                                 s5_b_re[idx], s5_b_im[idx], s5_c_re[idx], s5_c_im[idx], s5_d[idx],
                                 s5_log_step[idx], batch, seq)
            mixer_out = (y, proj, conv_wf, w_glu)
            w_out = odd_w_out
        else:
            mixer_out = _even_mixer(h, norm_mix[layer], even_w, idx, ab_b_forget[idx], tables, batch, seq)
            w_out = even_w_out
        h = mix_xattn_block(h, mixer_out, w_out, idx, norm_xattn[layer], wq, kv_all, wo, layer,
                            seq, tm=_row_tile(seq, MIX_ROWS), odd=odd)
        h = mlp_block(h, norm_mlp[layer], w1, w2, layer, norm_final, layer == depth - 1,
                      tm=_row_tile(batch * seq, MLP_ROWS), tf=MLP_COLS)
    return h.reshape(batch, seq, d)
```

```python
import functools
import math

import jax
import jax.numpy as jnp
from jax import lax
from jax.experimental import pallas as pl
from jax.experimental.pallas import tpu as pltpu

F32 = jnp.float32
BF16 = jnp.bfloat16

EPS = 1e-6
ROPE_BASE = 10000.0
LANES = 128
SUBLANES = 8
HEAD_DIM = 128
FOX_HEADS = 8
RET_HEADS = 8
RET_CHUNK = 256
XA_HEADS = 4
S5_GROUP = 16
S5_STATE = 64
S5_CHUNK = 16
S5_SCAN_UNROLL = 8
MXU_TILE = 256
S5_OUT_STEPS = MXU_TILE // LANES
CONV_K = 3
S5_GROUP_SHIFT = S5_GROUP.bit_length() - 1
S5_STATE_SHIFT = S5_STATE.bit_length() - 1
GROUPS_PER_BLOCK = LANES // S5_GROUP
STATE_LANES = GROUPS_PER_BLOCK * S5_STATE
VMEM_LIMIT = 56 * 1024 * 1024
NEG_BIG = -1e30
LOG2E = math.log2(math.e)

PROJ_ROWS, PROJ_COLS = 1024, 1024
MIX_ROWS = 512
MLP_ROWS, MLP_COLS = 1024, 512
FOX_BLOCK = 1024
FOX_HEADS_PER_STEP = 2
RET_ROWS = 1024
RET_HEADS_PER_STEP = 8


def _row_tile(rows, want):
    tile = min(rows, want)
    while rows % tile:
        tile //= 2
    return tile


def _params(*sem):
    return pltpu.CompilerParams(dimension_semantics=sem, vmem_limit_bytes=VMEM_LIMIT)


def _rms(x, g):
    ms = jnp.mean(x * x, axis=-1, keepdims=True)
    return x * lax.rsqrt(ms + EPS) * g


def _norm_matmul_kernel(*refs, side, transposed):
    if side:
        x_ref, g_ref, w_ref, ws_ref, o_ref, os_ref, xn_ref = refs
    else:
        x_ref, g_ref, w_ref, o_ref, xn_ref = refs
    dims = (((1,), (1 if transposed else 0,)), ((), ()))

    @pl.when(pl.program_id(1) == 0)
    def _():
        xn_ref[...] = _rms(x_ref[...], g_ref[...]).astype(BF16)
        if side:
            os_ref[...] = lax.dot_general(xn_ref[...], ws_ref[...], dims, preferred_element_type=F32)

    o_ref[...] = lax.dot_general(xn_ref[...], w_ref[...].astype(BF16), dims,
                                 preferred_element_type=F32).astype(o_ref.dtype)


def norm_matmul(x, g, w, layer, n_layers, out_dtype, tm, tn, w_side=None, transposed=False):
    t, d = x.shape
    n = w.shape[1 if transposed else 2]
    nj = n // tn

    def w_spec(cols, index):
        if transposed:
            return pl.BlockSpec((None, cols, d), lambda i, j: (index(j)[0], index(j)[1], 0))
        return pl.BlockSpec((None, d, cols), lambda i, j: (index(j)[0], 0, index(j)[1]))

    side = w_side is not None
    in_specs = [
        pl.BlockSpec((tm, d), lambda i, j: (i, 0)),
        pl.BlockSpec((1, d), lambda i, j: (0, 0)),
        w_spec(tn, lambda j: (layer + j // nj, j % nj)),
    ]
    out_specs = pl.BlockSpec((tm, tn), lambda i, j: (i, j))
    out_shape = jax.ShapeDtypeStruct((t, n_layers * n), out_dtype)
    args = (x, g.reshape(1, d), w)
    if side:
        ns = w_side.shape[1 if transposed else 2]
        in_specs.append(w_spec(ns, lambda j: (layer, 0)))
        out_specs = [out_specs, pl.BlockSpec((tm, ns), lambda i, j: (i, 0))]
        out_shape = [out_shape, jax.ShapeDtypeStruct((t, ns), F32)]
        args = args + (w_side,)
    return pl.pallas_call(
        functools.partial(_norm_matmul_kernel, side=side, transposed=transposed),
        grid=(t // tm, n_layers * nj),
        in_specs=in_specs,
        out_specs=out_specs,
        out_shape=out_shape,
        scratch_shapes=[pltpu.VMEM((tm, d), BF16)],
        compiler_params=_params("parallel", "arbitrary"),
        name="norm_matmul",
    )(*args)


def _mlp_kernel(h_ref, g_ref, w1_ref, w2_ref, gf_ref, o_ref, xn_ref, *, final_norm):
    f = pl.program_id(1)

    @pl.when(f == 0)
    def _():
        x = h_ref[...]
        xn_ref[...] = _rms(x, g_ref[...]).astype(BF16)
        o_ref[...] = x

    a = jnp.dot(xn_ref[...], w1_ref[...].astype(BF16), preferred_element_type=F32)
    a = jnp.square(jnp.maximum(a, 0.0)).astype(BF16)
    o_ref[...] += jnp.dot(a, w2_ref[...].astype(BF16), preferred_element_type=F32)

    if final_norm:
        @pl.when(f == pl.num_programs(1) - 1)
        def _():
            o_ref[...] = _rms(o_ref[...], gf_ref[...])


def mlp_block(h, g, w1, w2, layer, g_final, final_norm, tm, tf):
    t, d = h.shape
    dff = w1.shape[2]
    return pl.pallas_call(
        functools.partial(_mlp_kernel, final_norm=final_norm),
        grid=(t // tm, dff // tf),
        in_specs=[
            pl.BlockSpec((tm, d), lambda i, f: (i, 0)),
            pl.BlockSpec((1, d), lambda i, f: (0, 0)),
            pl.BlockSpec((None, d, tf), lambda i, f: (layer, 0, f)),
            pl.BlockSpec((None, tf, d), lambda i, f: (layer, f, 0)),
            pl.BlockSpec((1, d), lambda i, f: (0, 0)),
        ],
        out_specs=pl.BlockSpec((tm, d), lambda i, f: (i, 0)),
        out_shape=jax.ShapeDtypeStruct((t, d), F32),
        scratch_shapes=[pltpu.VMEM((tm, d), BF16)],
        compiler_params=_params("parallel", "arbitrary"),
        name="mlp_block",
    )(h, g.reshape(1, d), w1, w2, g_final.reshape(1, d))


def _s5_gate(y, w_glu):
    g = 0.5 * y * (1.0 + jnp.tanh(math.sqrt(2.0 / math.pi) * (y + 0.044715 * (y * y * y))))
    z = jnp.dot(g.astype(BF16), w_glu, preferred_element_type=F32)
    return (g * jax.nn.sigmoid(z)).astype(BF16)


def _short_conv(hc, gb, gc, z_prev, w_ref):
    z = gc * hc
    row = lax.broadcasted_iota(jnp.int32, z.shape, 0)
    y = w_ref[CONV_K - 1:CONV_K, :] * z
    for lag in range(1, CONV_K):
        zl = pltpu.roll(z, lag, 0)
        for r in range(lag):
            src = SUBLANES - lag + r
            zl = jnp.where(row == r, z_prev[src:src + 1, :], zl)
        y = y + w_ref[CONV_K - 1 - lag:CONV_K - lag, :] * zl
    return (gb * y).astype(BF16)


def _mix_xattn_kernel(*refs, odd, per_batch):
    if odd:
        (h_ref, y_ref, hc_ref, gb_ref, gc_ref, hcp_ref, gcp_ref, cw_ref, wglu_ref,
         w1_ref, w2_ref, g_ref, wq_ref, k_ref, v_ref, wo_ref, o_ref) = refs
        a1 = _s5_gate(y_ref[...], wglu_ref[...])
        first = pl.program_id(0) % per_batch == 0
        z_prev = jnp.where(first, 0.0, gcp_ref[...] * hcp_ref[...])
        a2 = _short_conv(hc_ref[...], gb_ref[...], gc_ref[...], z_prev, cw_ref)
    else:
        h_ref, a1_ref, a2_ref, w1_ref, w2_ref, g_ref, wq_ref, k_ref, v_ref, wo_ref, o_ref = refs
        a1 = a1_ref[...]
        a2 = a2_ref[...]
    x = h_ref[...] + jnp.dot(a1, w1_ref[...], preferred_element_type=F32)
    x = x + jnp.dot(a2, w2_ref[...], preferred_element_type=F32)
    xn = _rms(x, g_ref[...]).astype(BF16)
    q = jnp.dot(xn, wq_ref[...], preferred_element_type=F32).astype(BF16)
    heads = []
    for hd in range(XA_HEADS):
        sl = slice(hd * HEAD_DIM, (hd + 1) * HEAD_DIM)
        s = lax.dot_general(q[:, sl], k_ref[0, :, sl], (((1,), (1,)), ((), ())),
                            preferred_element_type=F32)
        m = jnp.max(s, axis=-1, keepdims=True)
        p = jnp.exp(s - m)
        l = jnp.sum(p, axis=-1, keepdims=True)
        oh = jnp.dot(p.astype(BF16), v_ref[0, :, sl], preferred_element_type=F32)
        heads.append((oh * (1.0 / l)).astype(BF16))
    o = jnp.concatenate(heads, axis=-1)
    o_ref[...] = x + jnp.dot(o, wo_ref[...], preferred_element_type=F32)


def mix_xattn_block(h, mixer_inputs, w_out, mixer_idx, g, wq, kv, wo, layer, seq, tm, odd):
    t, d = h.shape
    k1 = w_out.shape[1] // 2
    k2 = k1
    mlen = kv.shape[1]
    xw = wq.shape[2]
    per_batch = seq // tm

    def fixed(shape, *index):
        return pl.BlockSpec(shape, lambda i: index, pipeline_mode=pl.Buffered(1))

    def rows(width, col=0):
        return pl.BlockSpec((tm, width), lambda i: (i, col))

    if odd:
        y, proj, conv_w, w_glu = mixer_inputs
        halo = tm // SUBLANES

        def prev_rows(col):
            return pl.BlockSpec((SUBLANES, k2), lambda i: (jnp.maximum(i * halo - 1, 0), col))

        mixer_args = (y, proj, proj, proj, proj, proj, conv_w, w_glu)
        mixer_specs = [rows(k1), rows(k2, 1), rows(k2, 2), rows(k2, 3), prev_rows(1), prev_rows(3),
                       fixed((None,) + conv_w.shape[1:], mixer_idx, 0, 0),
                       fixed((None,) + w_glu.shape[1:], mixer_idx, 0, 0)]
    else:
        mixer_args = mixer_inputs
        mixer_specs = [rows(k1), rows(k2)]

    return pl.pallas_call(
        functools.partial(_mix_xattn_kernel, odd=odd, per_batch=per_batch),
        grid=(t // tm,),
        in_specs=[rows(d)] + mixer_specs + [
            fixed((None, k1, d), mixer_idx, 0, 0), fixed((None, k2, d), mixer_idx, 1, 0),
            fixed((1, d), 0, 0), fixed((None, d, xw), layer, 0, 0),
            pl.BlockSpec((1, mlen, xw), lambda i: (i // per_batch, 0, 2 * layer)),
            pl.BlockSpec((1, mlen, xw), lambda i: (i // per_batch, 0, 2 * layer + 1)),
            fixed((None, xw, d), layer, 0, 0),
        ],
        out_specs=rows(d),
        out_shape=jax.ShapeDtypeStruct((t, d), F32),
        compiler_params=_params("parallel"),
        name="mix_xattn_block",
    )(h, *mixer_args, w_out, w_out, g.reshape(1, d), wq, kv, kv, wo)


def _split3(x):
    hi = x.astype(BF16)
    r1 = x - hi.astype(F32)
    mid = r1.astype(BF16)
    lo = (r1 - mid.astype(F32)).astype(BF16)
    return hi, mid, lo


def _fox_cumsum_kernel(fl_ref, b_ref, call_ref, *, blk):
    seq = fl_ref.shape[0]
    row = lax.broadcasted_iota(jnp.int32, (blk, blk), 0)
    col = lax.broadcasted_iota(jnp.int32, (blk, blk), 1)
    tri = jnp.where(row >= col, 1.0, 0.0).astype(BF16)
    carry = jnp.zeros((1, LANES), F32)
    for i in range(seq // blk):
        rows = slice(i * blk, (i + 1) * blk)
        x = fl_ref[rows, :] + b_ref[...]
        ls = (jnp.minimum(x, 0.0) - jnp.log(1.0 + jnp.exp(-jnp.abs(x)))) * LOG2E
        c = carry
        for piece in _split3(ls):
            c = c + jnp.dot(tri, piece, preferred_element_type=F32)
        call_ref[rows, :] = c
        carry = c[blk - 1:blk, :]


def fox_cumsum(fl, bias, batch, seq):
    blk = min(MXU_TILE, seq)
    return pl.pallas_call(
        functools.partial(_fox_cumsum_kernel, blk=blk),
        grid=(batch,),
        in_specs=[
            pl.BlockSpec((seq, LANES), lambda b: (b, 0)),
            pl.BlockSpec((1, LANES), lambda b: (0, 0)),
        ],
        out_specs=pl.BlockSpec((seq, LANES), lambda b: (b, 0)),
        out_shape=jax.ShapeDtypeStruct((batch * seq, LANES), F32),
        compiler_params=_params("parallel"),
        name="fox_cumsum",
    )(fl, bias)


def _lane_tile(x, reps):
    return jnp.concatenate([x] * reps, axis=1)


def _gate_lanes(c_col, lane, own_first):
    hi, mid, lo = [p.astype(F32) for p in _split3(c_col)]
    base = 0 if own_first else 3
    pieces = jnp.where(lane == base, hi, jnp.where(lane == base + 1, mid, jnp.where(lane == base + 2, lo, 0.0)))
    ones = jnp.where((lane >= 3 - base) & (lane < 6 - base), 1.0, 0.0)
    return (pieces + ones).astype(BF16)


def _fox_attn_kernel(q_ref, k_ref, v_ref, c_ref, o_ref, qaug_ref, kaug_ref, vaug_ref, m_ref, acc_ref,
                     *, blk, hp):
    g = pl.program_id(1)
    i = pl.program_id(2)
    seq = k_ref.shape[0]
    nt = (((1,), (1,)), ((), ()))

    def head_cols(t):
        return slice(t * HEAD_DIM, (t + 1) * HEAD_DIM)

    @pl.when(i == 0)
    def _():
        lane = lax.broadcasted_iota(jnp.int32, (seq, LANES), 1)
        for t in range(hp):
            c_key = jnp.sum(jnp.where(lane == g * hp + t, c_ref[...], 0.0), axis=1, keepdims=True)
            kaug_ref[t, :, :HEAD_DIM] = k_ref[:, head_cols(t)]
            kaug_ref[t, :, HEAD_DIM:] = _gate_lanes(-c_key, lane, own_first=False)
            vaug_ref[t, :, :HEAD_DIM] = v_ref[:, head_cols(t)]
            vaug_ref[t, :, HEAD_DIM:] = jnp.ones((seq, LANES), BF16)

    lane = lax.broadcasted_iota(jnp.int32, (blk, LANES), 1)
    c_rows = c_ref[pl.ds(pl.multiple_of(i * blk, blk), blk), :]
    for t in range(hp):
        c_query = jnp.sum(jnp.where(lane == g * hp + t, c_rows, 0.0), axis=1, keepdims=True)
        qaug_ref[t, :, :HEAD_DIM] = q_ref[:, head_cols(t)]
        qaug_ref[t, :, HEAD_DIM:] = _gate_lanes(c_query, lane, own_first=True)
    m_ref[...] = jnp.full(m_ref.shape, NEG_BIG, F32)
    acc_ref[...] = jnp.zeros(acc_ref.shape, F32)

    def update(rows, keys, diag_shift):
        nr, nk = rows.stop - rows.start, keys.size
        scores = [lax.dot_general(qaug_ref[t, rows], kaug_ref[t, keys, :], nt,
                                  preferred_element_type=F32) for t in range(hp)]
        for t in range(hp):
            s = scores[t]
            if diag_shift is not None:
                r = lax.broadcasted_iota(jnp.int32, (nr, nk), 0)
                c = lax.broadcasted_iota(jnp.int32, (nr, nk), 1)
                s = jnp.where(r + diag_shift >= c, s, -jnp.inf)
            m_prev = m_ref[t, rows]
            m_new = jnp.maximum(m_prev, jnp.max(s, axis=-1, keepdims=True))
            alpha = jnp.exp2(m_prev - m_new)
            p = jnp.exp2(s - _lane_tile(m_new, nk // LANES)).astype(BF16)
            acc_ref[t, rows] = (_lane_tile(alpha, acc_ref.shape[2] // LANES) * acc_ref[t, rows]
                                + jnp.dot(p, vaug_ref[t, keys, :], preferred_element_type=F32))
            m_ref[t, rows] = m_new

    def body(j, carry):
        update(slice(0, blk), pl.ds(pl.multiple_of(j * blk, blk), blk), None)
        return carry

    lax.fori_loop(0, i, body, 0)
    half = blk // 2
    start = pl.multiple_of(i * blk, blk)
    update(slice(0, half), pl.ds(start, half), 0)
    update(slice(half, blk), pl.ds(start, blk), half)
    for t in range(hp):
        acc = acc_ref[t]
        o_ref[:, head_cols(t)] = (acc[:, :HEAD_DIM] * (1.0 / acc[:, HEAD_DIM:])).astype(o_ref.dtype)


def fox_attention(proj, call, batch, seq, q_col, k_col, v_col, blk, hp):
    nq = seq // blk
    width = hp * HEAD_DIM
    aug = HEAD_DIM + LANES
    return pl.pallas_call(
        functools.partial(_fox_attn_kernel, blk=blk, hp=hp),
        grid=(batch, FOX_HEADS // hp, nq),
        in_specs=[
            pl.BlockSpec((blk, width), lambda b, g, i: (b * nq + i, q_col // hp + g)),
            pl.BlockSpec((seq, width), lambda b, g, i: (b, k_col // hp + g)),
            pl.BlockSpec((seq, width), lambda b, g, i: (b, v_col // hp + g)),
            pl.BlockSpec((seq, LANES), lambda b, g, i: (b, 0)),
        ],
        out_specs=pl.BlockSpec((blk, width), lambda b, g, i: (b * nq + i, g)),
        out_shape=jax.ShapeDtypeStruct((batch * seq, FOX_HEADS * HEAD_DIM), BF16),
        scratch_shapes=[
            pltpu.VMEM((hp, blk, aug), BF16),
            pltpu.VMEM((hp, seq, aug), BF16),
            pltpu.VMEM((hp, seq, aug), BF16),
            pltpu.VMEM((hp, blk, LANES), F32),
            pltpu.VMEM((hp, blk, aug), F32),
        ],
        compiler_params=_params("parallel", "parallel", "arbitrary"),
        name="fox_attention",
    )(proj, proj, proj, call)


def _retention_kernel(q_ref, k_ref, v_ref, g_ref, cos_ref, sin_ref, dm_ref, aux_ref, o_ref,
                      r_ref, qr_ref, qx_ref, kr_ref, kz_ref, of_ref, *, cs, hp):
    rows_per_step = q_ref.shape[0]
    n_chunks = rows_per_step // cs
    half = HEAD_DIM // 2
    nt = (((1,), (1,)), ((), ()))

    @pl.when(pl.program_id(2) == 0)
    def _():
        r_ref[...] = jnp.zeros(r_ref.shape, F32)

    def head_cols(t):
        return slice(t * HEAD_DIM, (t + 1) * HEAD_DIM)

    cos = cos_ref[...]
    sin = sin_ref[...]
    for t in range(hp):
        q = q_ref[:, head_cols(t)].astype(F32)
        k = k_ref[:, head_cols(t)].astype(F32)
        qr = q * cos + pltpu.roll(q, half, 1) * sin
        kr = k * cos + pltpu.roll(k, half, 1) * sin
        xi = jnp.concatenate([aux_ref[t, 0]] * n_chunks, axis=0)
        zeta = jnp.concatenate([aux_ref[t, 1]] * n_chunks, axis=0)
        qr_ref[:, head_cols(t)] = qr.astype(BF16)
        qx_ref[:, head_cols(t)] = (qr * xi).astype(BF16)
        kr_ref[:, head_cols(t)] = kr.astype(BF16)
        kz_ref[:, head_cols(t)] = (kr * zeta).astype(BF16)

    def body(c, carry):
        rows = pl.ds(pl.multiple_of(c * cs, cs), cs)
        for t in range(hp):
            v = v_ref[rows, head_cols(t)]
            s = lax.dot_general(qr_ref[rows, head_cols(t)], kr_ref[rows, head_cols(t)], nt,
                                preferred_element_type=F32) * dm_ref[t]
            r = r_ref[t]
            lhs = jnp.concatenate([s.astype(BF16), qx_ref[rows, head_cols(t)]], axis=1)
            rhs = jnp.concatenate([v, r.astype(BF16)], axis=0)
            of_ref[rows, head_cols(t)] = jnp.dot(lhs, rhs, preferred_element_type=F32)
            r_ref[t] = aux_ref[t, 2, :HEAD_DIM] * r + lax.dot_general(
                kz_ref[rows, head_cols(t)], v, (((0,), (0,)), ((), ())), preferred_element_type=F32)
        return carry

    lax.fori_loop(0, n_chunks, body, 0, unroll=True)

    for t in range(hp):
        o = of_ref[:, head_cols(t)]
        mu = jnp.mean(o, axis=-1, keepdims=True)
        oc = o - mu
        var = jnp.mean(oc * oc, axis=-1, keepdims=True)
        on = oc * lax.rsqrt(var + EPS)
        g = g_ref[:, head_cols(t)].astype(F32)
        o_ref[:, head_cols(t)] = (g * jax.nn.sigmoid(g) * on).astype(o_ref.dtype)


def retention(proj, cos, sin, dm, aux, batch, seq, q_col, k_col, v_col, g_col, rows, hp):
    width = hp * HEAD_DIM
    nr = seq // rows

    def col(c0):
        return pl.BlockSpec((rows, width), lambda b, g, i: (b * nr + i, c0 // hp + g))

    return pl.pallas_call(
        functools.partial(_retention_kernel, cs=RET_CHUNK, hp=hp),
        grid=(batch, RET_HEADS // hp, nr),
        in_specs=[
            col(q_col), col(k_col), col(v_col), col(g_col),
            pl.BlockSpec((rows, HEAD_DIM), lambda b, g, i: (i, 0)),
            pl.BlockSpec((rows, HEAD_DIM), lambda b, g, i: (i, 0)),
            pl.BlockSpec((hp, RET_CHUNK, RET_CHUNK), lambda b, g, i: (g, 0, 0)),
            pl.BlockSpec((hp, 3, RET_CHUNK, LANES), lambda b, g, i: (g, 0, 0, 0)),
        ],
        out_specs=pl.BlockSpec((rows, width), lambda b, g, i: (b * nr + i, g)),
        out_shape=jax.ShapeDtypeStruct((batch * seq, RET_HEADS * HEAD_DIM), BF16),
        scratch_shapes=[
            pltpu.VMEM((hp, HEAD_DIM, HEAD_DIM), F32),
            pltpu.VMEM((rows, width), BF16),
            pltpu.VMEM((rows, width), BF16),
            pltpu.VMEM((rows, width), BF16),
            pltpu.VMEM((rows, width), BF16),
            pltpu.VMEM((rows, width), F32),
        ],
        compiler_params=_params("parallel", "parallel", "arbitrary"),
        name="retention",
    )(proj, proj, proj, proj, cos, sin, dm, aux)


def _retention_tables(seq):
    half = HEAD_DIM // 2
    inv = ROPE_BASE ** (-jnp.arange(half, dtype=F32) / half)
    ang = jnp.arange(seq, dtype=F32)[:, None] * inv[None, :]
    cos, sin = jnp.cos(ang), jnp.sin(ang)
    cos_t = jnp.concatenate([cos, cos], axis=-1)
    sin_t = jnp.concatenate([-sin, sin], axis=-1)
    cs = RET_CHUNK
    log_g = jnp.log1p(-jnp.exp2(-5.0 - jnp.arange(RET_HEADS, dtype=F32)))
    pos = jnp.arange(cs, dtype=F32)
    diff = pos[:, None] - pos[None, :]
    dm = jnp.where(diff >= 0, jnp.exp(log_g[:, None, None] * jnp.maximum(diff, 0.0)), 0.0)
    zeta = jnp.exp(log_g[:, None] * (cs - 1 - pos)[None, :])
    xi = jnp.exp(log_g[:, None] * (pos + 1)[None, :])
    g_chunk = jnp.broadcast_to(jnp.exp(log_g * cs)[:, None], (RET_HEADS, cs))
    aux = jnp.broadcast_to(jnp.stack([xi, zeta, g_chunk], axis=1)[..., None], (RET_HEADS, 3, cs, LANES))
    return cos_t, sin_t, dm, aux


def _s5_kernel(u_ref, are_ref, aim_ref, ls_ref, bre_ref, bim_ref, cre_ref, cim_ref, d_ref, y_ref,
               toep_ref, win_ref, woutt_ref, aq_ref, wf_ref, ucat_ref, inc_ref, x_ref, *, q):
    m = u_ref.shape[0] // q
    sl = STATE_LANES

    @pl.when(pl.program_id(1) == 0)
    def _():
        _s5_build_tables(are_ref, aim_ref, ls_ref, bre_ref, bim_ref, cre_ref, cim_ref,
                         toep_ref, win_ref, woutt_ref, aq_ref, wf_ref, q=q)

    for s in range(q):
        ucat_ref[:, s * LANES:(s + 1) * LANES] = u_ref[pl.ds(s, m, stride=q), :].astype(BF16)
    ucat = ucat_ref[...]
    inc_ref[...] = jnp.dot(ucat, win_ref[...], preferred_element_type=F32)
    a_re = aq_ref[0:1, :]
    a_im = aq_ref[1:2, :]

    def body(n, carry):
        x_re, x_im = carry
        x_ref[pl.ds(n, 1), 0:sl] = x_re
        x_ref[pl.ds(n, 1), sl:2 * sl] = x_im
        i_re = inc_ref[pl.ds(n, 1), 0:sl]
        i_im = inc_ref[pl.ds(n, 1), sl:2 * sl]
        return (a_re * x_re - a_im * x_im + i_re, a_re * x_im + a_im * x_re + i_im)

    zero = jnp.zeros((1, sl), F32)
    lax.fori_loop(0, m, body, (zero, zero), unroll=S5_SCAN_UNROLL)
    xb = x_ref[...].astype(BF16)
    nt = (((1,), (1,)), ((), ()))
    for t0 in range(0, q, S5_OUT_STEPS):
        hi = (t0 + S5_OUT_STEPS) * LANES
        cols = slice(t0 * LANES, hi)
        y = jnp.dot(ucat_ref[:, :hi], toep_ref[:hi, cols], preferred_element_type=F32)
        y = y + lax.dot_general(xb, woutt_ref[cols, :], nt, preferred_element_type=F32)
        for t in range(t0, t0 + S5_OUT_STEPS):
            rows = pl.ds(t, m, stride=q)
            y_ref[rows, :] = (y[:, (t - t0) * LANES:(t - t0 + 1) * LANES]
                              + d_ref[...] * u_ref[rows, :])


def s5_core(proj, a_re, a_im, b_re, b_im, c_re, c_im, log_step, d_skip, batch, seq, u_col):
    q = S5_CHUNK
    m = seq // q
    g, p = a_re.shape
    gb = GROUPS_PER_BLOCK
    nblk = g // gb
    sl = gb * p

    def rows(t):
        return t.astype(F32).reshape(nblk, 1, sl)

    def b_mat(t):
        return t.astype(F32).reshape(nblk, gb, p, S5_GROUP).transpose(0, 3, 1, 2).reshape(nblk, S5_GROUP, sl)

    def c_mat(t):
        return t.astype(F32).reshape(nblk, gb, S5_GROUP, p).transpose(0, 2, 1, 3).reshape(nblk, S5_GROUP, sl)

    row_spec = pl.BlockSpec((1, 1, sl), lambda j, b: (j, 0, 0))
    mat_spec = pl.BlockSpec((1, S5_GROUP, sl), lambda j, b: (j, 0, 0))
    return pl.pallas_call(
        functools.partial(_s5_kernel, q=q),
        grid=(nblk, batch),
        in_specs=[
            pl.BlockSpec((seq, LANES), lambda j, b: (b, u_col + j)),
            row_spec, row_spec, row_spec, mat_spec, mat_spec, mat_spec, mat_spec,
            pl.BlockSpec((1, LANES), lambda j, b: (0, j)),
        ],
        out_specs=pl.BlockSpec((seq, LANES), lambda j, b: (b, j)),
        out_shape=jax.ShapeDtypeStruct((batch * seq, nblk * LANES), F32),
        scratch_shapes=[
            pltpu.VMEM((q * LANES, q * LANES), BF16),
            pltpu.VMEM((q * LANES, 2 * sl), BF16),
            pltpu.VMEM((q * LANES, 2 * sl), BF16),
            pltpu.VMEM((2, sl), F32),
            pltpu.VMEM((q * LANES, 2 * sl), F32),
            pltpu.VMEM((m, q * LANES), BF16),
            pltpu.VMEM((m, 2 * sl), F32),
            pltpu.VMEM((m, 2 * sl), F32),
        ],
        compiler_params=_params("parallel", "arbitrary"),
        name="s5_core",
    )(proj, rows(a_re), rows(a_im), rows(log_step), b_mat(b_re), b_mat(b_im), c_mat(c_re), c_mat(c_im),
      d_skip)


def _dot_nt_split(x, y):
    nt = (((1,), (1,)), ((), ()))
    xh = x.astype(BF16)
    xl = (x - xh.astype(F32)).astype(BF16)
    yh = y.astype(BF16)
    yl = (y - yh.astype(F32)).astype(BF16)
    out = lax.dot_general(xh, yh, nt, preferred_element_type=F32)
    out = out + lax.dot_general(xh, yl, nt, preferred_element_type=F32)
    return out + lax.dot_general(xl, yh, nt, preferred_element_type=F32)


def _s5_build_tables(are_ref, aim_ref, ls_ref, bre_ref, bim_ref, cre_ref, cim_ref,
                     toep_ref, win_ref, woutt_ref, aq_ref, wf_ref, *, q):
    sl = STATE_LANES
    lam_re = jnp.minimum(are_ref[0], -1e-4)
    lam_im = aim_ref[0]
    step = jnp.exp(ls_ref[0])
    mag = jnp.exp(lam_re * step)
    a_re = mag * jnp.cos(lam_im * step)
    a_im = mag * jnp.sin(lam_im * step)
    den = lam_re * lam_re + lam_im * lam_im
    f_re = ((a_re - 1.0) * lam_re + a_im * lam_im) / den
    f_im = (a_im * lam_re - (a_re - 1.0) * lam_im) / den
    b_re = bre_ref[0]
    b_im = bim_ref[0]
    bb_re = f_re * b_re - f_im * b_im
    bb_im = f_re * b_im + f_im * b_re
    c_re = cre_ref[0]
    c_im = cim_ref[0]

    pows = [(jnp.ones((1, sl), F32), jnp.zeros((1, sl), F32))]
    for _ in range(q):
        p_re, p_im = pows[-1]
        pows.append((p_re * a_re - p_im * a_im, p_re * a_im + p_im * a_re))

    row_group = lax.shift_right_logical(lax.broadcasted_iota(jnp.int32, (LANES, sl), 0), S5_GROUP_SHIFT)
    col_group = lax.shift_right_logical(lax.broadcasted_iota(jnp.int32, (LANES, sl), 1), S5_STATE_SHIFT)
    same_group = row_group == col_group

    def tile(k, m_re, m_im, im_sign):
        p_re, p_im = pows[k]
        v_re = p_re * m_re - p_im * m_im
        v_im = (p_re * m_im + p_im * m_re) * im_sign
        e_re = jnp.where(same_group, jnp.concatenate([v_re] * GROUPS_PER_BLOCK, axis=0), 0.0)
        e_im = jnp.where(same_group, jnp.concatenate([v_im] * GROUPS_PER_BLOCK, axis=0), 0.0)
        return jnp.concatenate([e_re, e_im], axis=1)

    for s in range(q):
        rows = slice(s * LANES, (s + 1) * LANES)
        wf_ref[rows, :] = tile(q - 1 - s, bb_re, bb_im, 1.0)
        woutt_ref[rows, :] = tile(s + 1, c_re, c_im, -1.0).astype(BF16)
    win_ref[...] = wf_ref[...].astype(BF16)
    taps = _dot_nt_split(wf_ref[...], tile(0, c_re, c_im, -1.0))
    toep_ref[...] = jnp.zeros(toep_ref.shape, BF16)
    for s in range(q):
        for t in range(s, q):
            lag_rows = slice((q - 1 - (t - s)) * LANES, (q - (t - s)) * LANES)
            toep_ref[s * LANES:(s + 1) * LANES, t * LANES:(t + 1) * LANES] = taps[lag_rows, :].astype(BF16)
    aq_ref[0:1, :] = pows[q][0]
    aq_ref[1:2, :] = pows[q][1]


def _even_weights(w_in):
    fw = FOX_HEADS * HEAD_DIM
    rw = RET_HEADS * HEAD_DIM
    scale = HEAD_DIM ** -0.5
    c0 = 3 * fw
    c1 = c0 + FOX_HEADS
    fox_scale = jnp.concatenate([jnp.full((fw,), scale * LOG2E, F32), jnp.ones((2 * fw,), F32)])
    ret_scale = jnp.concatenate([jnp.ones((rw,), F32), jnp.full((rw,), scale, F32), jnp.ones((2 * rw,), F32)])
    w_t = jnp.swapaxes(w_in, 1, 2)
    w_main = jnp.concatenate([w_t[:, :c0] * fox_scale[:, None], w_t[:, c1:] * ret_scale[:, None]],
                             axis=1).astype(BF16)
    w_forget = jnp.pad(w_t[:, c0:c1], ((0, 0), (0, LANES - FOX_HEADS), (0, 0))).astype(BF16)
    return w_main, w_forget


def _even_mixer(h, g, weights, e, b_forget, tables, batch, seq):
    w_main, w_forget = weights
    bias = jnp.pad(b_forget.astype(F32), (0, LANES - FOX_HEADS)).reshape(1, LANES)
    t = h.shape[0]
    proj, fl = norm_matmul(h, g, w_main, e, 1, BF16, tm=_row_tile(t, PROJ_ROWS), tn=PROJ_COLS,
                           w_side=w_forget, transposed=True)
    call = fox_cumsum(fl, bias, batch, seq)
    nh = FOX_HEADS
    fox = fox_attention(proj, call, batch, seq, 0, nh, 2 * nh, blk=_row_tile(seq, FOX_BLOCK),
                        hp=FOX_HEADS_PER_STEP)
    cos_t, sin_t, dm, aux = tables
    ret = retention(proj, cos_t, sin_t, dm, aux, batch, seq, 3 * nh, 4 * nh, 5 * nh, 6 * nh,
                    rows=_row_tile(seq, RET_ROWS), hp=RET_HEADS_PER_STEP)
    return fox, ret


def _odd_mixer(h, g, w_in, o, a_re, a_im, b_re, b_im, c_re, c_im, d_skip, log_step, batch, seq):
    sw = a_re.shape[0] * S5_GROUP
    t = h.shape[0]
    proj = norm_matmul(h, g, w_in, o, 1, F32, tm=_row_tile(t, PROJ_ROWS), tn=PROJ_COLS)
    y = s5_core(proj, a_re, a_im, b_re, b_im, c_re, c_im, log_step, d_skip.reshape(1, sw).astype(F32),
                batch, seq, 0)
    return y, proj


def kernel(x, mem, norm_mix, norm_xattn, norm_mlp, norm_mem, norm_final, ab_w_in, ab_b_forget, ab_w_out,
           cd_w_in, s5_a_re, s5_a_im, s5_b_re, s5_b_im, s5_c_re, s5_c_im, s5_d, s5_log_step, s5_w_glu,
           conv_w, cd_w_out, xa_wq, xa_wkv, xa_wo, mlp_w1, mlp_w2):
    batch, seq, d = x.shape
    depth = norm_mix.shape[0]
    mlen = mem.shape[1]
    xw = xa_wq.shape[2]
    h = x.reshape(batch * seq, d)
    memf = mem.reshape(batch * mlen, d)
    tables = _retention_tables(seq)
    even_w = _even_weights(ab_w_in)
    odd_w_in = cd_w_in.astype(BF16)
    even_w_out = ab_w_out.astype(BF16)
    odd_w_out = cd_w_out.astype(BF16)
    w_glu = s5_w_glu.astype(BF16)
    conv_wf = conv_w.astype(F32)
    wq = (xa_wq * HEAD_DIM ** -0.5).astype(BF16)
    wo = xa_wo.astype(BF16)
    w1 = mlp_w1
    w2 = mlp_w2
    kv_all = norm_matmul(memf, norm_mem, xa_wkv, 0, depth, BF16,
                         tm=_row_tile(batch * mlen, PROJ_ROWS), tn=PROJ_COLS)
    kv_all = kv_all.reshape(batch, mlen, depth * 2 * xw)
    for layer in range(depth):
        odd = layer % 2 == 1
        idx = layer // 2
        if odd:
            y, proj = _odd_mixer(h, norm_mix[layer], odd_w_in, idx, s5_a_re[idx], s5_a_im[idx],
                                 s5_b_re[idx], s5_b_im[idx], s5_c_re[idx], s5_c_im[idx], s5_d[idx],
                                 s5_log_step[idx], batch, seq)
            mixer_out = (y, proj, conv_wf, w_glu)
            w_out = odd_w_out
        else:
            mixer_out = _even_mixer(h, norm_mix[layer], even_w, idx, ab_b_forget[idx], tables, batch, seq)
            w_out = even_w_out
        h = mix_xattn_block(h, mixer_out, w_out, idx, norm_xattn[layer], wq, kv_all, wo, layer,
                            seq, tm=_row_tile(seq, MIX_ROWS), odd=odd)
        h = mlp_block(h, norm_mlp[layer], w1, w2, layer, norm_final, layer == depth - 1,
                      tm=_row_tile(batch * seq, MLP_ROWS), tf=MLP_COLS)
    return h.reshape(batch, seq, d)
```
